```python
import math
import jax
import jax.numpy as jnp
from jax import lax
import numpy as np

D_MODEL = 1024
BATCH = 8
SEQ = 4096
DEPTH = 4

N_MIXERS = 3
N_LAYERS_A = (DEPTH + 2) // 3
N_LAYERS_B = (DEPTH + 1) // 3
N_LAYERS_C = DEPTH // 3

HEAD_DIM = 64
ROT_DIM = HEAD_DIM // 4
ROPE_THETA = 500000.0
NORM_EPS = 1e-6
NEG_INF = -1e30
Q_BLOCK = 128

DIFF_HEADS = D_MODEL // (2 * HEAD_DIM)
DIFF_QK_WIDTH = DIFF_HEADS * 2 * HEAD_DIM
DIFF_V_DIM = 2 * HEAD_DIM
DIFF_SUBLN_EPS = 1e-5

MOBA_HEADS = D_MODEL // HEAD_DIM
MOBA_BLOCK = 256
MOBA_TOPK = 3

NSA_HEADS = D_MODEL // HEAD_DIM
NSA_GROUPS = NSA_HEADS // 4
NSA_REP = NSA_HEADS // NSA_GROUPS
NSA_Q_BLOCK = 64
CMP_LEN = 32
CMP_STRIDE = 16
CMP_HIDDEN = 4 * HEAD_DIM
SLC_BLOCK = 64
SLC_TOPK = 16
WINDOW = 512
FORCED_SCORE = 1e9
NSA_KV_WIDTH = NSA_GROUPS * HEAD_DIM
NSA_IN_WIDTH = NSA_HEADS * HEAD_DIM + 6 * NSA_KV_WIDTH + 3 * NSA_HEADS

D_FF = 2816

kernel_name = "hybrid_diff_moba_nsa_macaron_trunk"


def rms_norm(x, g, eps=NORM_EPS):
    xf = x.astype(jnp.float32)
    y = xf * lax.rsqrt(jnp.mean(xf * xf, axis=-1, keepdims=True) + eps)
    return (y * g.astype(jnp.float32)).astype(x.dtype)


def partial_rope(x, pos):
    half = ROT_DIM // 2
    inv_freq = ROPE_THETA ** (-jnp.arange(half, dtype=jnp.float32) / half)
    ang = pos.astype(jnp.float32)[:, None] * inv_freq[None, :]
    cos = jnp.cos(ang)[None, :, None, :].astype(x.dtype)
    sin = jnp.sin(ang)[None, :, None, :].astype(x.dtype)
    x1, x2, rest = x[..., :half], x[..., half:ROT_DIM], x[..., ROT_DIM:]
    return jnp.concatenate([x1 * cos - x2 * sin, x2 * cos + x1 * sin, rest], axis=-1)


def masked_probs(scores, mask):
    p = jax.nn.softmax(jnp.where(mask, scores, NEG_INF), axis=-1)
    return jnp.where(mask, p, 0.0)


def gather_blocks(blocks, idx):
    return jax.vmap(jax.vmap(lambda bl, ix: bl[ix]))(blocks, idx)


def chunked(fn, n_chunks):
    o = jnp.moveaxis(lax.map(fn, jnp.arange(n_chunks)), 0, 1)
    return o.reshape((o.shape[0], o.shape[1] * o.shape[2]) + o.shape[3:])


def swiglu(x, w_in, w_out):
    gate, up = jnp.split(x @ w_in, 2, axis=-1)
    return (jax.nn.silu(gate) * up) @ w_out


def diff_attention(h, w_in, w_out, lam, subln_g, lambda_init, pos):
    B, S, _ = h.shape
    H, Dh = DIFF_HEADS, HEAD_DIM
    q, k, v = jnp.split(h @ w_in, 3, axis=-1)
    q = partial_rope(q.reshape(B, S, 2 * H, Dh), pos).reshape(B, S, H, 2, Dh)
    k = partial_rope(k.reshape(B, S, 2 * H, Dh), pos).reshape(B, S, H, 2, Dh)
    v = v.reshape(B, S, H, DIFF_V_DIM)
    lam = lam.astype(jnp.float32)
    lam_full = jnp.exp(jnp.sum(lam[0] * lam[1])) - jnp.exp(jnp.sum(lam[2] * lam[3])) + lambda_init
    scale = Dh ** -0.5

    def chunk(i):
        s0 = i * Q_BLOCK
        qc = lax.dynamic_slice_in_dim(q, s0, Q_BLOCK, axis=1)
        qpos = lax.dynamic_slice_in_dim(pos, s0, Q_BLOCK)
        sc = jnp.einsum('bqhcd,bkhcd->bhcqk', qc, k, preferred_element_type=jnp.float32) * scale
        p = masked_probs(sc, pos[None, :] <= qpos[:, None])
        a = p[:, :, 0] - lam_full * p[:, :, 1]
        return jnp.einsum('bhqk,bkhe->bqhe', a.astype(v.dtype), v)

    o = chunked(chunk, S // Q_BLOCK)
    o = rms_norm(o, subln_g, DIFF_SUBLN_EPS) * (1.0 - lambda_init)
    return o.reshape(B, S, H * DIFF_V_DIM) @ w_out


def moba_attention(h, w_in, w_out, pos):
    B, S, _ = h.shape
    H, Dh, L = MOBA_HEADS, HEAD_DIM, MOBA_BLOCK
    q, k, v = jnp.split(h @ w_in, 3, axis=-1)
    q = partial_rope(q.reshape(B, S, H, Dh), pos)
    k = partial_rope(k.reshape(B, S, H, Dh), pos)
    v = v.reshape(B, S, H, Dh)
    nb = -(-S // L)
    pad = nb * L - S

    def to_blocks(t):
        t = jnp.pad(t, ((0, 0), (0, pad), (0, 0), (0, 0)))
        return t.reshape(B, nb, L, H, Dh).transpose(0, 3, 1, 2, 4)

    kb, vb = to_blocks(k), to_blocks(v)
    k_mean = jnp.mean(kb.astype(jnp.float32), axis=3)
    topk = min(MOBA_TOPK, nb)
    blk_ids = jnp.arange(nb)
    scale = Dh ** -0.5

    def chunk(i):
        s0 = i * Q_BLOCK
        qidx = s0 + jnp.arange(Q_BLOCK)
        cur = s0 // L
        qc = lax.dynamic_slice_in_dim(q, s0, Q_BLOCK, axis=1)
        gate = jnp.einsum('bqhd,bhnd->bhqn', qc.astype(jnp.float32), k_mean)
        gate = jnp.where(blk_ids < cur, gate, NEG_INF)
        top_s, top_i = lax.top_k(gate, topk)
        sel_mask = jnp.broadcast_to((top_s > 0.5 * NEG_INF)[..., None],
                                    top_s.shape + (L,)).reshape(B, H, Q_BLOCK, topk * L)
        ksel = gather_blocks(kb, top_i).reshape(B, H, Q_BLOCK, topk * L, Dh)
        vsel = gather_blocks(vb, top_i).reshape(B, H, Q_BLOCK, topk * L, Dh)
        kown = lax.dynamic_index_in_dim(kb, cur, axis=2, keepdims=False)
        vown = lax.dynamic_index_in_dim(vb, cur, axis=2, keepdims=False)
        own_mask = jnp.broadcast_to((cur * L + jnp.arange(L))[None, :] <= qidx[:, None],
                                    (B, H, Q_BLOCK, L))
        s_sel = jnp.einsum('bqhd,bhqkd->bhqk', qc, ksel, preferred_element_type=jnp.float32)
        s_own = jnp.einsum('bqhd,bhkd->bhqk', qc, kown, preferred_element_type=jnp.float32)
        scores = jnp.concatenate([s_sel, s_own], axis=-1) * scale
        p = masked_probs(scores, jnp.concatenate([sel_mask, own_mask], axis=-1)).astype(v.dtype)
        return (jnp.einsum('bhqk,bhqkd->bqhd', p[..., :topk * L], vsel)
                + jnp.einsum('bhqk,bhkd->bqhd', p[..., topk * L:], vown))

    o = chunked(chunk, S // Q_BLOCK)
    return o.reshape(B, S, H * Dh) @ w_out


def compress(t, pe, w1, w2):
    B, S, G, Dh = t.shape
    r = CMP_LEN // CMP_STRIDE
    c = t.reshape(B, S // CMP_STRIDE, CMP_STRIDE, G, Dh)
    nc = S // CMP_STRIDE - r + 1
    blocks = jnp.concatenate([c[:, j:j + nc] for j in range(r)], axis=2)
    blocks = blocks + pe[None, None, :, None, :].astype(t.dtype)
    flat = blocks.transpose(0, 1, 3, 2, 4).reshape(B, nc, G, CMP_LEN * Dh)
    return jax.nn.silu(flat @ w1) @ w2


def nsa_attention(h, w_in, w_out, cmp_pe, cmp_w1, cmp_w2, pos):
    B, S, _ = h.shape
    H, G, R, Dh = NSA_HEADS, NSA_GROUPS, NSA_REP, HEAD_DIM
    Qb = NSA_Q_BLOCK
    splits = [H * Dh + j * NSA_KV_WIDTH for j in range(7)]
    q, kc, vc, ks, vs, kw, vw, g_logits = jnp.split(h @ w_in, splits, axis=-1)
    q = partial_rope(q.reshape(B, S, H, Dh), pos).reshape(B, S, G, R, Dh)

    def kv(t):
        return t.reshape(B, S, G, Dh)

    kcmp = compress(partial_rope(kv(kc), pos), cmp_pe[0], cmp_w1[0], cmp_w2[0])
    vcmp = compress(kv(vc), cmp_pe[1], cmp_w1[1], cmp_w2[1])
    nc = kcmp.shape[1]
    cmp_end = jnp.arange(nc) * CMP_STRIDE + CMP_LEN - 1
    nsel = S // SLC_BLOCK
    n_top = min(SLC_TOPK, nsel)
    cmp_start = jnp.arange(nc)[:, None] * CMP_STRIDE
    blk_start = jnp.arange(nsel)[None, :] * SLC_BLOCK
    overlap = ((cmp_start < blk_start + SLC_BLOCK) & (cmp_start + CMP_LEN > blk_start)).astype(jnp.float32)

    def sel_blocks(t):
        return t.reshape(B, nsel, SLC_BLOCK, G, Dh).transpose(0, 3, 1, 2, 4)

    kb = sel_blocks(partial_rope(kv(ks), pos))
    vb = sel_blocks(kv(vs))
    def padw(t):
        return jnp.pad(t, ((0, 0), (WINDOW, 0), (0, 0), (0, 0)))

    kwp = padw(partial_rope(kv(kw), pos))
    vwp = padw(kv(vw))
    gates = jax.nn.sigmoid(g_logits).reshape(B, S, G, R, 3)
    blk_ids = jnp.arange(nsel)
    scale = Dh ** -0.5

    def chunk(i):
        s0 = i * Qb
        qidx = s0 + jnp.arange(Qb)
        qc = lax.dynamic_slice_in_dim(q, s0, Qb, axis=1)
        s_c = jnp.einsum('bqgrd,bngd->bgrqn', qc, kcmp, preferred_element_type=jnp.float32) * scale
        p_c = masked_probs(s_c, cmp_end[None, :] <= qidx[:, None])
        o_c = jnp.einsum('bgrqn,bngd->bqgrd', p_c.astype(vcmp.dtype), vcmp)
        imp = jnp.einsum('bgrqn,nj->bgqj', p_c, overlap)
        qblk = (qidx // SLC_BLOCK)[:, None]
        forced = (blk_ids == 0) | (blk_ids == qblk) | (blk_ids == qblk - 1)
        imp = jnp.where(blk_ids <= qblk, jnp.where(forced, FORCED_SCORE, imp), NEG_INF)
        top_s, top_i = lax.top_k(imp, n_top)
        kidx = top_i[..., None] * SLC_BLOCK + jnp.arange(SLC_BLOCK)
        m_s = ((top_s > 0.5 * NEG_INF)[..., None] & (kidx <= qidx[:, None, None])
               ).reshape(B, G, 1, Qb, n_top * SLC_BLOCK)
        ksel = gather_blocks(kb, top_i).reshape(B, G, Qb, n_top * SLC_BLOCK, Dh)
        vsel = gather_blocks(vb, top_i).reshape(B, G, Qb, n_top * SLC_BLOCK, Dh)
        s_s = jnp.einsum('bqgrd,bgqkd->bgrqk', qc, ksel, preferred_element_type=jnp.float32) * scale
        p_s = masked_probs(s_s, m_s)
        o_s = jnp.einsum('bgrqk,bgqkd->bqgrd', p_s.astype(vsel.dtype), vsel)
        kwc = lax.dynamic_slice_in_dim(kwp, s0, WINDOW + Qb, axis=1)
        vwc = lax.dynamic_slice_in_dim(vwp, s0, WINDOW + Qb, axis=1)
        kidx_w = s0 - WINDOW + jnp.arange(WINDOW + Qb)
        dist = qidx[:, None] - kidx_w[None, :]
        m_w = (kidx_w[None, :] >= 0) & (dist >= 0) & (dist < WINDOW)
        s_w = jnp.einsum('bqgrd,bkgd->bgrqk', qc, kwc, preferred_element_type=jnp.float32) * scale
        p_w = masked_probs(s_w, m_w)
        o_w = jnp.einsum('bgrqk,bkgd->bqgrd', p_w.astype(vwc.dtype), vwc)
        gc = lax.dynamic_slice_in_dim(gates, s0, Qb, axis=1)
        return gc[..., 0:1] * o_c + gc[..., 1:2] * o_s + gc[..., 2:3] * o_w

    o = chunked(chunk, S // Qb)
    return o.reshape(B, S, H * Dh) @ w_out


def setup_inputs(seed: int = 0) -> dict:
    key = jax.random.key(seed)
    ks = jax.random.split(key, 15)

    def nrm(k, shape):
        return jax.random.normal(k, shape, jnp.float32)

    def w(k, shape, fan_in):
        return nrm(k, shape) * fan_in ** -0.5

    return {
        "x": nrm(ks[0], (BATCH, SEQ, D_MODEL)),
        "norm_g": 1.0 + 0.05 * nrm(ks[1], (DEPTH, 6, D_MODEL)),
        "ffn_w_in": w(ks[2], (DEPTH, 2, D_MODEL, 2 * D_FF), D_MODEL),
        "ffn_w_out": w(ks[3], (DEPTH, 2, D_FF, D_MODEL), D_FF),
        "diff_w_in": w(ks[4], (N_LAYERS_A, D_MODEL, 3 * DIFF_QK_WIDTH), D_MODEL),
        "diff_w_out": w(ks[5], (N_LAYERS_A, DIFF_HEADS * DIFF_V_DIM, D_MODEL), DIFF_HEADS * DIFF_V_DIM),
        "diff_lambda": 0.1 * nrm(ks[6], (N_LAYERS_A, 4, HEAD_DIM)),
        "diff_subln": 1.0 + 0.05 * nrm(ks[7], (N_LAYERS_A, DIFF_V_DIM)),
        "moba_w_in": w(ks[8], (N_LAYERS_B, D_MODEL, 3 * MOBA_HEADS * HEAD_DIM), D_MODEL),
        "moba_w_out": w(ks[9], (N_LAYERS_B, MOBA_HEADS * HEAD_DIM, D_MODEL), MOBA_HEADS * HEAD_DIM),
        "nsa_w_in": w(ks[10], (N_LAYERS_C, D_MODEL, NSA_IN_WIDTH), D_MODEL),
        "nsa_w_out": w(ks[11], (N_LAYERS_C, NSA_HEADS * HEAD_DIM, D_MODEL), NSA_HEADS * HEAD_DIM),
        "nsa_cmp_pe": 0.2 * nrm(ks[12], (N_LAYERS_C, 2, CMP_LEN, HEAD_DIM)),
        "nsa_cmp_w1": w(ks[13], (N_LAYERS_C, 2, CMP_LEN * HEAD_DIM, CMP_HIDDEN), CMP_LEN * HEAD_DIM),
        "nsa_cmp_w2": w(ks[14], (N_LAYERS_C, 2, CMP_HIDDEN, HEAD_DIM), CMP_HIDDEN),
    }


def reference(x, norm_g, ffn_w_in, ffn_w_out, diff_w_in, diff_w_out, diff_lambda, diff_subln,
              moba_w_in, moba_w_out, nsa_w_in, nsa_w_out, nsa_cmp_pe, nsa_cmp_w1, nsa_cmp_w2):
    pos = jnp.arange(x.shape[1])
    h = x
    for i in range(DEPTH):
        g = norm_g[i]
        h = h + 0.5 * rms_norm(swiglu(rms_norm(h, g[0]), ffn_w_in[i, 0], ffn_w_out[i, 0]), g[1])
        m = rms_norm(h, g[2])
        kind, j = i % N_MIXERS, i // N_MIXERS
        if kind == 0:
            lambda_init = 0.8 - 0.6 * math.exp(-0.3 * i)
            m = diff_attention(m, diff_w_in[j], diff_w_out[j], diff_lambda[j], diff_subln[j],
                               lambda_init, pos)
        elif kind == 1:
            m = moba_attention(m, moba_w_in[j], moba_w_out[j], pos)
        else:
            m = nsa_attention(m, nsa_w_in[j], nsa_w_out[j], nsa_cmp_pe[j], nsa_cmp_w1[j],
                              nsa_cmp_w2[j], pos)
        h = h + rms_norm(m, g[3])
        h = h + 0.5 * rms_norm(swiglu(rms_norm(h, g[4]), ffn_w_in[i, 1], ffn_w_out[i, 1]), g[5])
    return h
```

```python
import functools
import math

import jax
import jax.numpy as jnp
import numpy as np
from jax import lax
from jax.experimental import pallas as pl
from jax.experimental.pallas import tpu as pltpu

D_MODEL = 1024
DEPTH = 4
HEAD_DIM = 64
ROT_DIM = HEAD_DIM // 4
ROPE_THETA = 500000.0
NORM_EPS = 1e-6
NEG_INF = -1e30
REMOVED = -3e38

DIFF_HEADS = 8
DIFF_SUBLN_EPS = 1e-5
MOBA_BLOCK = 256
MOBA_TOPK = 3
NSA_HEADS = 16
NSA_GROUPS = 4
CMP_LEN = 32
CMP_STRIDE = 16
CMP_HIDDEN = 256
SLC_BLOCK = 64
SLC_SHIFT = 6
SLC_TOPK = 16
WINDOW = 512
FORCED_SCORE = 1e9
D_FF = 2816

LANES = 128
FF_CHUNK = 256
SCALE = HEAD_DIM ** -0.5

BF = jnp.bfloat16
F32 = jnp.float32
VMEM_LIMIT = 56 * 1024 * 1024

_NT = (((1,), (1,)), ((), ()))


def _cparams(sem):
    return pltpu.CompilerParams(dimension_semantics=sem, vmem_limit_bytes=VMEM_LIMIT)


def _rms(x, g, eps):
    return x * lax.rsqrt(jnp.mean(x * x, axis=-1, keepdims=True) + eps) * g


def _resident(shape):
    nd = len(shape)
    return pl.BlockSpec(shape, lambda *_: (0,) * nd, pipeline_mode=pl.Buffered(1))


def _ffn_kernel(h_ref, g_ref, wg_ref, wu_ref, wo_ref, o_ref, acc_ref, *, n_chunks):
    h = h_ref[...]
    xn = _rms(h, g_ref[0:1, :], NORM_EPS).astype(BF)
    acc_ref[...] = jnp.zeros_like(acc_ref)

    def body(c, carry):
        gate = jnp.dot(xn, wg_ref[c], preferred_element_type=F32)
        up = jnp.dot(xn, wu_ref[c], preferred_element_type=F32)
        act = (gate * (1.0 / (1.0 + jnp.exp(-gate)))) * up
        acc_ref[...] += jnp.dot(act.astype(BF), wo_ref[c], preferred_element_type=F32)
        return carry

    lax.fori_loop(0, n_chunks, body, 0)
    o_ref[...] = h + 0.5 * _rms(acc_ref[...], g_ref[1:2, :], NORM_EPS)


def _ffn(h, g2, wg, wu, wo, tm):
    n, d = h.shape
    nc = wg.shape[0]
    return pl.pallas_call(
        functools.partial(_ffn_kernel, n_chunks=nc),
        grid=(n // tm,),
        in_specs=[
            pl.BlockSpec((tm, d), lambda i: (i, 0)),
            _resident(g2.shape),
            _resident(wg.shape),
            _resident(wu.shape),
            _resident(wo.shape),
        ],
        out_specs=pl.BlockSpec((tm, d), lambda i: (i, 0)),
        out_shape=jax.ShapeDtypeStruct((n, d), F32),
        scratch_shapes=[pltpu.VMEM((tm, d), F32)],
        compiler_params=_cparams(("parallel",)),
        name="ffn_halfstep",
    )(h, g2, wg, wu, wo)


def _proj_kernel(h_ref, g_ref, w_ref, cos_ref, sa_ref, sb_ref, *out_refs, plan, tm, n_out):
    xn = _rms(h_ref[...], g_ref[...], NORM_EPS).astype(BF)
    cos = cos_ref[...]
    sa = sa_ref[...]
    sb = sb_ref[...]
    for col, width, rope, dest, off, km_off in plan:
        y = jnp.dot(xn, w_ref[:, col:col + width], preferred_element_type=F32)
        for k in range(width // LANES):
            yk = y[:, k * LANES:(k + 1) * LANES]
            if rope:
                yk = yk * cos + pltpu.roll(yk, LANES - ROT_DIM // 2, 1) * sa + pltpu.roll(yk, ROT_DIM // 2, 1) * sb
            o_ref = out_refs[dest]
            o_ref[:, off + k * LANES:off + (k + 1) * LANES] = yk.astype(o_ref.dtype)
            if km_off is not None:
                km_ref = out_refs[n_out]
                for r in range(tm // MOBA_BLOCK):
                    blk = yk[r * MOBA_BLOCK:(r + 1) * MOBA_BLOCK, :]
                    km_ref[0, r:r + 1, km_off + k * LANES:km_off + (k + 1) * LANES] = jnp.mean(
                        blk, axis=0, keepdims=True)


def _proj(h, g, w, tables, plan, outs, tm, seq, kmean_width=None):
    n, d = h.shape
    per_seq = seq // tm
    out_shape = [jax.ShapeDtypeStruct((n, wd), dt) for wd, dt in outs]
    out_specs = [pl.BlockSpec((tm, wd), lambda i: (i, 0)) for wd, _ in outs]
    if kmean_width is not None:
        nb = tm // MOBA_BLOCK
        out_shape.append(jax.ShapeDtypeStruct((n // tm, nb, kmean_width), F32))
        out_specs.append(pl.BlockSpec((1, nb, kmean_width), lambda i: (i, 0, 0)))
    tab_spec = pl.BlockSpec((tm, LANES), lambda i: (i % per_seq, 0))
    return pl.pallas_call(
        functools.partial(_proj_kernel, plan=tuple(plan), tm=tm, n_out=len(outs)),
        grid=(n // tm,),
        in_specs=[
            pl.BlockSpec((tm, d), lambda i: (i, 0)),
            _resident(g.shape),
            _resident(w.shape),
            tab_spec, tab_spec, tab_spec,
        ],
        out_specs=out_specs,
        out_shape=out_shape,
        compiler_params=_cparams(("parallel",)),
        name="norm_proj",
    )(h, g, w, *tables)


def _outproj_kernel(*refs, n_parts):
    h_ref = refs[0]
    parts = refs[1:1 + n_parts]
    w_ref, g_ref, o_ref = refs[1 + n_parts:]
    a = parts[0][...]
    if n_parts > 1:
        a = a.astype(F32)
        for p in parts[1:]:
            a = a + p[...].astype(F32)
        a = a.astype(BF)
    y = jnp.dot(a, w_ref[...], preferred_element_type=F32)
    o_ref[...] = h_ref[...] + _rms(y, g_ref[...], NORM_EPS)


def _outproj(h, parts, w, g, tm):
    n, d = h.shape
    row = pl.BlockSpec((tm, d), lambda i: (i, 0))
    return pl.pallas_call(
        functools.partial(_outproj_kernel, n_parts=len(parts)),
        grid=(n // tm,),
        in_specs=[row] + [row] * len(parts) + [_resident(w.shape), _resident(g.shape)],
        out_specs=row,
        out_shape=jax.ShapeDtypeStruct((n, d), F32),
        compiler_params=_cparams(("parallel",)),
        name="out_proj",
    )(h, *parts, w, g)


def _softmax_step(s, v, m_ref, l_ref, acc_ref):
    m_prev = m_ref[...]
    m_new = jnp.maximum(m_prev, jnp.max(s, axis=1, keepdims=True))
    alpha = jnp.exp(m_prev - m_new)
    p = jnp.exp(s - m_new)
    l_ref[...] = alpha * l_ref[...] + jnp.sum(p, axis=1, keepdims=True)
    acc_ref[...] = alpha * acc_ref[...] + jnp.dot(p.astype(BF), v, preferred_element_type=F32)
    m_ref[...] = m_new


def _init_stats(m_ref, l_ref, acc_ref):
    m_ref[...] = jnp.full(m_ref.shape, NEG_INF, F32)
    l_ref[...] = jnp.zeros(l_ref.shape, F32)
    acc_ref[...] = jnp.zeros(acc_ref.shape, F32)


def _split_halves(q):
    lane = lax.broadcasted_iota(jnp.int32, q.shape, 1)
    zero = jnp.zeros_like(q)
    return jnp.concatenate([jnp.where(lane < HEAD_DIM, q, zero), jnp.where(lane >= HEAD_DIM, q, zero)], axis=0)


def _diff_kernel(q_ref, k_ref, v_ref, lam_ref, sg_ref, o_ref, m_ref, l_ref, acc_ref, *, t, lambda_init):
    qi = pl.program_id(2)
    qs = _split_halves(q_ref[0] * jnp.asarray(SCALE, BF))
    _init_stats(m_ref, l_ref, acc_ref)

    def tile(j, causal):
        start = pl.multiple_of(j * t, t)
        k = k_ref[0, pl.ds(start, t), :]
        v = v_ref[0, pl.ds(start, t), :]
        s = lax.dot_general(qs, k, _NT, preferred_element_type=F32)
        if causal:
            row = lax.broadcasted_iota(jnp.int32, s.shape, 0) & (t - 1)
            col = lax.broadcasted_iota(jnp.int32, s.shape, 1)
            s = jnp.where(col <= row, s, NEG_INF)
        _softmax_step(s, v, m_ref, l_ref, acc_ref)

    tile(qi, True)

    def body(j, carry):
        tile(j, False)
        return carry

    lax.fori_loop(0, qi, body, 0)

    o = acc_ref[...] * (1.0 / l_ref[...])
    lam = lam_ref[...]
    lam_full = (jnp.exp(jnp.sum(lam[0:1] * lam[1:2], axis=1, keepdims=True))
                - jnp.exp(jnp.sum(lam[2:3] * lam[3:4], axis=1, keepdims=True)) + lambda_init)
    od = o[:t] - lam_full * o[t:]
    od = _rms(od, sg_ref[...], DIFF_SUBLN_EPS) * (1.0 - lambda_init)
    o_ref[0] = od.astype(BF)


def _diff_attention(qkv, lam, subln, batch, seq, t, lambda_init):
    nh = DIFF_HEADS
    qkv3 = qkv.reshape(batch, seq, 3 * nh * LANES)
    out = pl.pallas_call(
        functools.partial(_diff_kernel, t=t, lambda_init=lambda_init),
        grid=(batch, nh, seq // t),
        in_specs=[
            pl.BlockSpec((1, t, LANES), lambda b, h, i: (b, i, h)),
            pl.BlockSpec((1, seq, LANES), lambda b, h, i: (b, 0, nh + h)),
            pl.BlockSpec((1, seq, LANES), lambda b, h, i: (b, 0, 2 * nh + h)),
            pl.BlockSpec(lam.shape, lambda b, h, i: (0, 0)),
            pl.BlockSpec(subln.shape, lambda b, h, i: (0, 0)),
        ],
        out_specs=pl.BlockSpec((1, t, LANES), lambda b, h, i: (b, i, h)),
        out_shape=jax.ShapeDtypeStruct((batch, seq, nh * LANES), BF),
        scratch_shapes=[pltpu.VMEM((2 * t, 1), F32), pltpu.VMEM((2 * t, 1), F32), pltpu.VMEM((2 * t, LANES), F32)],
        compiler_params=_cparams(("parallel", "parallel", "arbitrary")),
        name="diff_attention",
    )(qkv3, qkv3, qkv3, lam, subln)
    return out.reshape(batch * seq, nh * LANES)


def _moba_kernel(q_ref, k_ref, v_ref, km_ref, o_ref, m_ref, l_ref, acc_ref, *, t, nb):
    qi = pl.program_id(2)
    q2 = _split_halves(q_ref[0])
    qs = q2 * jnp.asarray(SCALE, BF)
    _init_stats(m_ref, l_ref, acc_ref)

    km = jnp.concatenate([km_ref[0], jnp.zeros((LANES - nb, LANES), F32)], axis=0)
    km_hi = km.astype(BF)
    km_lo = (km - km_hi.astype(F32)).astype(BF)
    gate = (lax.dot_general(q2, km_hi, _NT, preferred_element_type=F32)
            + lax.dot_general(q2, km_lo, _NT, preferred_element_type=F32))
    blk = lax.broadcasted_iota(jnp.int32, gate.shape, 1)
    blk_f = blk.astype(F32)
    gate = jnp.where(blk < qi, gate, NEG_INF)
    sel = jnp.zeros(gate.shape, F32)
    for _ in range(min(MOBA_TOPK, nb)):
        mx = jnp.max(gate, axis=1, keepdims=True)
        idx = jnp.min(jnp.where(gate == mx, blk_f, float(LANES)), axis=1, keepdims=True)
        hit = blk_f == idx
        sel = jnp.where(hit & (mx > 0.5 * NEG_INF), 1.0, sel)
        gate = jnp.where(hit, REMOVED, gate)

    def load(j):
        start = pl.multiple_of(j * t, t)
        return k_ref[0, pl.ds(start, t), :], v_ref[0, pl.ds(start, t), :]

    k, v = load(qi)
    s = lax.dot_general(qs, k, _NT, preferred_element_type=F32)
    row = lax.broadcasted_iota(jnp.int32, s.shape, 0) & (t - 1)
    col = lax.broadcasted_iota(jnp.int32, s.shape, 1)
    _softmax_step(jnp.where(col <= row, s, NEG_INF), v, m_ref, l_ref, acc_ref)

    def body(j, carry):
        k, v = load(j)
        s = lax.dot_general(qs, k, _NT, preferred_element_type=F32)
        chosen = jnp.sum(jnp.where(blk == j, sel, 0.0), axis=1, keepdims=True)
        _softmax_step(jnp.where(chosen > 0.5, s, NEG_INF), v, m_ref, l_ref, acc_ref)
        return carry

    lax.fori_loop(0, qi, body, 0)

    o = acc_ref[...] * (1.0 / l_ref[...])
    lane = lax.broadcasted_iota(jnp.int32, (t, LANES), 1)
    o_ref[0] = jnp.where(lane < HEAD_DIM, o[:t], o[t:]).astype(BF)


def _moba_attention(qkv, kmean, batch, seq):
    t = MOBA_BLOCK
    nb = seq // t
    npair = D_MODEL // LANES
    qkv3 = qkv.reshape(batch, seq, 3 * D_MODEL)
    km3 = kmean.reshape(batch, nb, D_MODEL)
    out = pl.pallas_call(
        functools.partial(_moba_kernel, t=t, nb=nb),
        grid=(batch, npair, nb),
        in_specs=[
            pl.BlockSpec((1, t, LANES), lambda b, p, i: (b, i, p)),
            pl.BlockSpec((1, seq, LANES), lambda b, p, i: (b, 0, npair + p)),
            pl.BlockSpec((1, seq, LANES), lambda b, p, i: (b, 0, 2 * npair + p)),
            pl.BlockSpec((1, nb, LANES), lambda b, p, i: (b, 0, p)),
        ],
        out_specs=pl.BlockSpec((1, t, LANES), lambda b, p, i: (b, i, p)),
        out_shape=jax.ShapeDtypeStruct((batch, seq, D_MODEL), BF),
        scratch_shapes=[pltpu.VMEM((2 * t, 1), F32), pltpu.VMEM((2 * t, 1), F32), pltpu.VMEM((2 * t, LANES), F32)],
        compiler_params=_cparams(("parallel", "parallel", "arbitrary")),
        name="moba_attention",
    )(qkv3, qkv3, qkv3, km3)
    return out.reshape(batch * seq, D_MODEL)


def _compress_kernel(r_ref, pe_ref, w1_ref, w2_ref, o_ref):
    r = r_ref[0, 0, 0]
    ng = r.shape[0]
    a0 = (r + pe_ref[0, 0]).astype(BF)
    a1 = (r + pe_ref[0, 1]).astype(BF)
    y0 = jnp.dot(a0, w1_ref[0, 0], preferred_element_type=F32)
    y1 = jnp.dot(a1, w1_ref[0, 1], preferred_element_type=F32)
    pre = y0 + pltpu.roll(y1, ng - 1, 0)
    hid = pre * (1.0 / (1.0 + jnp.exp(-pre)))
    o_ref[0, 0, 0] = jnp.dot(hid.astype(BF), w2_ref[0], preferred_element_type=F32)


def _compress(r, pe, w1, w2):
    two, batch, ngrp, ng, wd = r.shape
    return pl.pallas_call(
        _compress_kernel,
        grid=(two, batch, ngrp),
        in_specs=[
            pl.BlockSpec((1, 1, 1, ng, wd), lambda a, b, g: (a, b, g, 0, 0)),
            pl.BlockSpec((1, 2, 1, wd), lambda a, b, g: (a, 0, 0, 0)),
            pl.BlockSpec((1, 2, wd, CMP_HIDDEN), lambda a, b, g: (a, 0, 0, 0)),
            pl.BlockSpec((1, CMP_HIDDEN, HEAD_DIM), lambda a, b, g: (a, 0, 0)),
        ],
        out_specs=pl.BlockSpec((1, 1, 1, ng, HEAD_DIM), lambda a, b, g: (a, b, g, 0, 0)),
        out_shape=jax.ShapeDtypeStruct((two, batch, ngrp, ng, HEAD_DIM), F32),
        compiler_params=_cparams(("parallel", "parallel", "parallel")),
        name="nsa_compress",
    )(r, pe, w1, w2)


def _nsa_stack_q(q, t):
    lane = lax.broadcasted_iota(jnp.int32, (t, LANES), 1)
    zero = jnp.zeros((t, LANES), q.dtype)
    parts = []
    for half in range(2):
        keep = (lane < HEAD_DIM) if half == 0 else (lane >= HEAD_DIM)
        for r in range(4):
            parts.append(jnp.where(keep, q[:, r * LANES:(r + 1) * LANES], zero))
    return jnp.concatenate(parts, axis=0)


def _nsa_write(o, gl, branch, o_ref, t):
    lane = lax.broadcasted_iota(jnp.int32, (t, LANES), 1)
    for r in range(4):
        c0 = r * 3 + branch
        c1 = 12 + r * 3 + branch
        g0 = 1.0 / (1.0 + jnp.exp(-gl[:, c0:c0 + 1]))
        g1 = 1.0 / (1.0 + jnp.exp(-gl[:, c1:c1 + 1]))
        blk = jnp.where(lane < HEAD_DIM, g0 * o[r * t:(r + 1) * t], g1 * o[(4 + r) * t:(5 + r) * t])
        o_ref[0, :, r * LANES:(r + 1) * LANES] = blk.astype(BF)


def _nsa_cmp_kernel(q_ref, kc_ref, vc_ref, gl_ref, ovl_ref, o_ref, sel_ref, *, t, ncmp):
    qi = pl.program_id(2)
    qs = _nsa_stack_q(q_ref[0] * jnp.asarray(SCALE, BF), t)
    s = lax.dot_general(qs, kc_ref[0], _NT, preferred_element_type=F32)
    n_idx = lax.broadcasted_iota(jnp.int32, s.shape, 1)
    qidx = qi * t + (lax.broadcasted_iota(jnp.int32, s.shape, 0) & (t - 1))
    mask = (n_idx * CMP_STRIDE + (CMP_LEN - 1)) <= qidx
    sm = jnp.where(mask, s, NEG_INF)
    p = jnp.where(mask, jnp.exp(sm - jnp.max(sm, axis=1, keepdims=True)), 0.0)
    l = jnp.sum(p, axis=1, keepdims=True)
    pn = p * (1.0 / jnp.where(l > 0.0, l, 1.0))
    o = jnp.dot(pn.astype(BF), vc_ref[0], preferred_element_type=F32)
    _nsa_write(o, gl_ref[0], 0, o_ref, t)

    nsel = ovl_ref.shape[0]
    jb = lax.broadcasted_iota(jnp.int32, (nsel, t), 0)
    qblk = (qi * t + lax.broadcasted_iota(jnp.int32, (nsel, t), 1)) >> SLC_SHIFT
    forced = (jb == 0) | (jb == qblk) | (jb == qblk - 1)
    ovl = ovl_ref[...]
    for half in range(2):
        ps = pn[(half * 4) * t:(half * 4 + 1) * t]
        for r in range(1, 4):
            ps = ps + pn[(half * 4 + r) * t:(half * 4 + r + 1) * t]
        ps_hi = ps.astype(BF)
        ps_lo = (ps - ps_hi.astype(F32)).astype(BF)
        imp = (lax.dot_general(ovl, ps_hi, _NT, preferred_element_type=F32)
               + lax.dot_general(ovl, ps_lo, _NT, preferred_element_type=F32))
        val = jnp.where(jb <= qblk, jnp.where(forced, FORCED_SCORE, imp), NEG_INF)
        cnt = jnp.zeros((nsel, t), F32)
        for i in range(nsel):
            vi = val[i:i + 1, :]
            ahead = (vi > val) | ((vi == val) & (jb > i))
            cnt = cnt + jnp.where(ahead, 1.0, 0.0)
        chosen = jnp.where((cnt < float(min(SLC_TOPK, nsel))) & (jb <= qblk), 1.0, 0.0)
        pad = jnp.zeros((LANES - nsel, t), F32)
        sel_ref[0, 0, half] = jnp.concatenate([chosen, pad], axis=0).T.astype(BF)


def _nsa_sel_kernel(q_ref, k_ref, v_ref, gl_ref, sel_ref, o_ref, m_ref, l_ref, acc_ref, *, t, tk):
    qi = pl.program_id(2)
    qs = _nsa_stack_q(q_ref[0] * jnp.asarray(SCALE, BF), t)
    sel = jnp.concatenate([sel_ref[0, 0, 0]] * 4 + [sel_ref[0, 0, 1]] * 4, axis=0)
    _init_stats(m_ref, l_ref, acc_ref)
    jd = (qi * t) // tk

    def tile(j, causal):
        start = pl.multiple_of(j * tk, tk)
        k = k_ref[0, pl.ds(start, tk), :]
        v = v_ref[0, pl.ds(start, tk), :]
        s = lax.dot_general(qs, k, _NT, preferred_element_type=F32)
        eb = lax.broadcasted_iota(jnp.int32, (LANES, tk), 0) - j * (tk // SLC_BLOCK)
        ec = lax.broadcasted_iota(jnp.int32, (LANES, tk), 1) >> SLC_SHIFT
        expand = jnp.where(eb == ec, 1.0, 0.0).astype(BF)
        keep = jnp.dot(sel, expand, preferred_element_type=F32) > 0.5
        if causal:
            qidx = qi * t + (lax.broadcasted_iota(jnp.int32, s.shape, 0) & (t - 1))
            kidx = j * tk + lax.broadcasted_iota(jnp.int32, s.shape, 1)
            keep = keep & (kidx <= qidx)
        _softmax_step(jnp.where(keep, s, NEG_INF), v, m_ref, l_ref, acc_ref)

    tile(jd, True)

    def body(j, carry):
        tile(j, False)
        return carry

    lax.fori_loop(0, jd, body, 0)
    _nsa_write(acc_ref[...] * (1.0 / l_ref[...]), gl_ref[0], 1, o_ref, t)


def _nsa_win_kernel(q_ref, k_ref, v_ref, gl_ref, o_ref, m_ref, l_ref, acc_ref, *, t, tk):
    qi = pl.program_id(2)
    qs = _nsa_stack_q(q_ref[0] * jnp.asarray(SCALE, BF), t)
    _init_stats(m_ref, l_ref, acc_ref)
    jd = (qi * t) // tk

    def tile(j):
        start = pl.multiple_of(j * tk, tk)
        k = k_ref[0, pl.ds(start, tk), :]
        v = v_ref[0, pl.ds(start, tk), :]
        s = lax.dot_general(qs, k, _NT, preferred_element_type=F32)
        qidx = qi * t + (lax.broadcasted_iota(jnp.int32, s.shape, 0) & (t - 1))
        kidx = j * tk + lax.broadcasted_iota(jnp.int32, s.shape, 1)
        dist = qidx - kidx
        _softmax_step(jnp.where((dist >= 0) & (dist < WINDOW), s, NEG_INF), v, m_ref, l_ref, acc_ref)

    tile(jd)
    lo = jnp.maximum(jd - (WINDOW + tk - 1) // tk, 0)

    def body(j, carry):
        tile(j)
        return carry

    lax.fori_loop(lo, jd, body, 0)
    _nsa_write(acc_ref[...] * (1.0 / l_ref[...]), gl_ref[0], 2, o_ref, t)


def _nsa_attention(pa, pf, kcmp, vcmp, ovl_t, batch, seq, t, tk):
    pa3 = pa.reshape(batch, seq, pa.shape[1])
    pf3 = pf.reshape(batch, seq, pf.shape[1])
    nq = seq // t
    grid = (batch, 2, nq)
    ncmp = kcmp.shape[1]
    sem = _cparams(("parallel", "parallel", "arbitrary"))
    q_spec = pl.BlockSpec((1, t, 4 * LANES), lambda b, g, i: (b, i, g))
    gl_spec = pl.BlockSpec((1, t, LANES), lambda b, g, i: (b, i, 4 + g))
    o_spec = pl.BlockSpec((1, t, 4 * LANES), lambda b, g, i: (b, i, g))
    o_shape = jax.ShapeDtypeStruct((batch, seq, D_MODEL), BF)
    stats = [pltpu.VMEM((8 * t, 1), F32), pltpu.VMEM((8 * t, 1), F32), pltpu.VMEM((8 * t, LANES), F32)]

    def kv_spec(col_block):
        return pl.BlockSpec((1, seq, LANES), lambda b, g, i: (b, 0, col_block + g))

    o_c, sel = pl.pallas_call(
        functools.partial(_nsa_cmp_kernel, t=t, ncmp=ncmp),
        grid=grid,
        in_specs=[
            q_spec,
            pl.BlockSpec((1, ncmp, LANES), lambda b, g, i: (b, 0, g)),
            pl.BlockSpec((1, ncmp, LANES), lambda b, g, i: (b, 0, g)),
            gl_spec,
            pl.BlockSpec(ovl_t.shape, lambda b, g, i: (0, 0)),
        ],
        out_specs=[o_spec, pl.BlockSpec((1, 1, 2, t, LANES), lambda b, g, i: (b, g, 0, i, 0))],
        out_shape=[o_shape, jax.ShapeDtypeStruct((batch, 2, 2, seq, LANES), BF)],
        compiler_params=sem,
        name="nsa_compressed",
    )(pa3, kcmp, vcmp, pf3, ovl_t)

    o_s = pl.pallas_call(
        functools.partial(_nsa_sel_kernel, t=t, tk=tk),
        grid=grid,
        in_specs=[q_spec, kv_spec(8), kv_spec(12), gl_spec,
                  pl.BlockSpec((1, 1, 2, t, LANES), lambda b, g, i: (b, g, 0, i, 0))],
        out_specs=o_spec,
        out_shape=o_shape,
        scratch_shapes=stats,
        compiler_params=sem,
        name="nsa_selected",
    )(pa3, pa3, pa3, pf3, sel)

    o_w = pl.pallas_call(
        functools.partial(_nsa_win_kernel, t=t, tk=tk),
        grid=grid,
        in_specs=[q_spec, kv_spec(10), kv_spec(14), gl_spec],
        out_specs=o_spec,
        out_shape=o_shape,
        scratch_shapes=stats,
        compiler_params=sem,
        name="nsa_window",
    )(pa3, pa3, pa3, pf3)
    n = batch * seq
    return [o_c.reshape(n, D_MODEL), o_s.reshape(n, D_MODEL), o_w.reshape(n, D_MODEL)]


def _rope_tables(seq):
    half = ROT_DIM // 2
    inv_freq = ROPE_THETA ** (-jnp.arange(half, dtype=F32) / half)
    ang = jnp.arange(seq).astype(F32)[:, None] * inv_freq[None, :]
    cos, sin = jnp.cos(ang), jnp.sin(ang)
    d = np.arange(LANES) % HEAD_DIM
    idx = d % half
    cos_t = jnp.where(d[None, :] < ROT_DIM, cos[:, idx], 1.0)
    sa_t = jnp.where(d[None, :] < half, -sin[:, idx], 0.0)
    sb_t = jnp.where((d[None, :] >= half) & (d[None, :] < ROT_DIM), sin[:, idx], 0.0)
    return cos_t.astype(F32), sa_t.astype(F32), sb_t.astype(F32)


def _nsa_q_perm():
    cols = []
    for gp in range(2):
        for r in range(4):
            for half in range(2):
                head = 8 * gp + 4 * half + r
                cols.extend(range(head * HEAD_DIM, (head + 1) * HEAD_DIM))
    return np.asarray(cols, np.int32)


def _nsa_gate_cols():
    src = -np.ones(2 * LANES, np.int32)
    for gp in range(2):
        for half in range(2):
            for r in range(4):
                for br in range(3):
                    src[gp * LANES + half * 12 + r * 3 + br] = (4 * (2 * gp + half) + r) * 3 + br
    return src


def _ffn_weights(w_in, w_out):
    d = w_in.shape[0]
    nc = D_FF // FF_CHUNK
    wg = w_in[:, :D_FF].reshape(d, nc, FF_CHUNK).transpose(1, 0, 2).astype(BF)
    wu = w_in[:, D_FF:].reshape(d, nc, FF_CHUNK).transpose(1, 0, 2).astype(BF)
    wo = w_out.reshape(nc, FF_CHUNK, d).astype(BF)
    return wg, wu, wo


def kernel(x, norm_g, ffn_w_in, ffn_w_out, diff_w_in, diff_w_out, diff_lambda, diff_subln, moba_w_in, moba_w_out,
           nsa_w_in, nsa_w_out, nsa_cmp_pe, nsa_cmp_w1, nsa_cmp_w2):
    batch, seq, d = x.shape
    n = batch * seq
    tm = 512
    h = x.reshape(n, d)
    tables = _rope_tables(seq)

    for i in range(DEPTH):
        g = norm_g[i]
        wg, wu, wo = _ffn_weights(ffn_w_in[i, 0], ffn_w_out[i, 0])
        h = _ffn(h, g[0:2], wg, wu, wo, tm)

        kind, j = i % 3, i // 3
        if kind == 0:
            lambda_init = 0.8 - 0.6 * math.exp(-0.3 * i)
            plan = [(c * 256, 256, c < 8, 0, c * 256, None) for c in range(12)]
            (qkv,) = _proj(h, g[2:3], diff_w_in[j].astype(BF), tables, plan, [(3 * D_MODEL, BF)], tm, seq)
            attn = _diff_attention(qkv, diff_lambda[j], diff_subln[j].reshape(1, LANES), batch, seq, 256,
                                   lambda_init)
            h = _outproj(h, [attn], diff_w_out[j].astype(BF), g[3:4], tm)
        elif kind == 1:
            plan = [(c * 256, 256, c < 8, 0, c * 256, (c - 4) * 256 if 4 <= c < 8 else None) for c in range(12)]
            qkv, kmean = _proj(h, g[2:3], moba_w_in[j].astype(BF), tables, plan, [(3 * D_MODEL, BF)], tm, seq,
                               kmean_width=D_MODEL)
            attn = _moba_attention(qkv, kmean, batch, seq)
            h = _outproj(h, [attn], moba_w_out[j].astype(BF), g[3:4], tm)
        else:
            w = nsa_w_in[j]
            perm = _nsa_q_perm()
            kvw = NSA_GROUPS * HEAD_DIM
            base = NSA_HEADS * HEAD_DIM
            seg = {name: w[:, base + k * kvw: base + (k + 1) * kvw]
                   for k, name in enumerate(["kc", "vc", "ks", "vs", "kw", "vw"])}
            gsrc = _nsa_gate_cols()
            glog = w[:, base + 6 * kvw:]
            gate_w = jnp.where(gsrc[None, :] >= 0, glog[:, np.maximum(gsrc, 0)], 0.0)
            w_all = jnp.concatenate([w[:, perm], seg["ks"], seg["kw"], seg["kc"], seg["vs"], seg["vw"], seg["vc"],
                                     gate_w], axis=1).astype(BF)
            plan = [(c * 256, 256, True, 0, c * 256, None) for c in range(6)]
            plan.append((1536, 256, True, 1, 0, None))
            plan.append((1792, 256, False, 0, 1536, None))
            plan.append((2048, 256, False, 0, 1792, None))
            plan.append((2304, 256, False, 1, 256, None))
            plan.append((2560, 256, False, 1, 512, None))
            pa, pf = _proj(h, g[2:3], w_all, tables, plan, [(2048, BF), (768, F32)], tm, seq)

            ng = seq // CMP_STRIDE
            kcvc = pf[:, :2 * kvw].reshape(batch, seq, 2, NSA_GROUPS, HEAD_DIM).transpose(2, 0, 3, 1, 4)
            r = kcvc.reshape(2, batch, NSA_GROUPS, ng, CMP_STRIDE * HEAD_DIM)
            pe = nsa_cmp_pe[j].reshape(2, 2, 1, CMP_STRIDE * HEAD_DIM)
            w1 = nsa_cmp_w1[j].reshape(2, 2, CMP_STRIDE * HEAD_DIM, CMP_HIDDEN).astype(BF)
            w2 = nsa_cmp_w2[j].astype(BF)
            cmp_out = _compress(r, pe, w1, w2)
            cmp_tm = cmp_out.transpose(0, 1, 3, 2, 4).reshape(2, batch, ng, kvw).astype(BF)

            nsel = seq // SLC_BLOCK
            cs = np.arange(ng)[:, None] * CMP_STRIDE
            bs = np.arange(nsel)[None, :] * SLC_BLOCK
            ovl = ((cs < bs + SLC_BLOCK) & (cs + CMP_LEN > bs)).astype(np.float32)
            ovl[ng - 1, :] = 0.0
            ovl_t = jnp.asarray(ovl.T, BF)
            parts = _nsa_attention(pa, pf, cmp_tm[0], cmp_tm[1], ovl_t, batch, seq, 128, 512)
            h = _outproj(h, parts, nsa_w_out[j][perm, :].astype(BF), g[3:4], tm)

        wg, wu, wo = _ffn_weights(ffn_w_in[i, 1], ffn_w_out[i, 1])
        h = _ffn(h, g[4:6], wg, wu, wo, tm)
    return h.reshape(batch, seq, d)
```

```python
import functools
import math

import jax
import jax.numpy as jnp
import numpy as np
from jax import lax
from jax.experimental import pallas as pl
from jax.experimental.pallas import tpu as pltpu

D_MODEL = 1024
DEPTH = 4
HEAD_DIM = 64
ROT_DIM = HEAD_DIM // 4
ROPE_THETA = 500000.0
NORM_EPS = 1e-6
NEG_INF = -1e30
REMOVED = -3e38

DIFF_HEADS = 8
DIFF_SUBLN_EPS = 1e-5
MOBA_BLOCK = 256
MOBA_SHIFT = 8
MOBA_TOPK = 3
NSA_HEADS = 16
NSA_GROUPS = 4
CMP_LEN = 32
CMP_STRIDE = 16
CMP_HIDDEN = 256
SLC_BLOCK = 64
SLC_SHIFT = 6
SLC_TOPK = 16
WINDOW = 512
FORCED_SCORE = 1e9
D_FF = 2816

LANES = 128
FF_CHUNK = 256
SCALE = HEAD_DIM ** -0.5

BF = jnp.bfloat16
F32 = jnp.float32
VMEM_LIMIT = 56 * 1024 * 1024

_NT = (((1,), (1,)), ((), ()))


def _cparams(sem):
    return pltpu.CompilerParams(dimension_semantics=sem, vmem_limit_bytes=VMEM_LIMIT)


def _rms(x, g, eps):
    return x * lax.rsqrt(jnp.mean(x * x, axis=-1, keepdims=True) + eps) * g


def _resident(shape):
    nd = len(shape)
    return pl.BlockSpec(shape, lambda *_: (0,) * nd, pipeline_mode=pl.Buffered(1))


def _ffn_kernel(h_ref, g_ref, wg_ref, wu_ref, wo_ref, o_ref, acc_ref, *, n_chunks):
    h = h_ref[...]
    xn = _rms(h, g_ref[0:1, :], NORM_EPS).astype(BF)
    acc_ref[...] = jnp.zeros_like(acc_ref)

    def body(c, carry):
        gate = jnp.dot(xn, wg_ref[c], preferred_element_type=F32)
        up = jnp.dot(xn, wu_ref[c], preferred_element_type=F32)
        act = (gate * (1.0 / (1.0 + jnp.exp(-gate)))) * up
        acc_ref[...] += jnp.dot(act.astype(BF), wo_ref[c], preferred_element_type=F32)
        return carry

    lax.fori_loop(0, n_chunks, body, 0)
    o_ref[...] = h + 0.5 * _rms(acc_ref[...], g_ref[1:2, :], NORM_EPS)


def _ffn(h, g2, wg, wu, wo, tm):
    n, d = h.shape
    nc = wg.shape[0]
    return pl.pallas_call(
        functools.partial(_ffn_kernel, n_chunks=nc),
        grid=(n // tm,),
        in_specs=[
            pl.BlockSpec((tm, d), lambda i: (i, 0)),
            _resident(g2.shape),
            _resident(wg.shape),
            _resident(wu.shape),
            _resident(wo.shape),
        ],
        out_specs=pl.BlockSpec((tm, d), lambda i: (i, 0)),
        out_shape=jax.ShapeDtypeStruct((n, d), F32),
        scratch_shapes=[pltpu.VMEM((tm, d), F32)],
        compiler_params=_cparams(("parallel",)),
        name="ffn_halfstep",
    )(h, g2, wg, wu, wo)


def _proj_kernel(h_ref, g_ref, w_ref, cos_ref, sa_ref, sb_ref, *out_refs, plan, tm, n_out):
    xn = _rms(h_ref[...], g_ref[...], NORM_EPS).astype(BF)
    cos = cos_ref[...]
    sa = sa_ref[...]
    sb = sb_ref[...]
    for col, width, rope, dest, off, km_off in plan:
        y = jnp.dot(xn, w_ref[:, col:col + width], preferred_element_type=F32)
        for k in range(width // LANES):
            yk = y[:, k * LANES:(k + 1) * LANES]
            if rope:
                yk = yk * cos + pltpu.roll(yk, LANES - ROT_DIM // 2, 1) * sa + pltpu.roll(yk, ROT_DIM // 2, 1) * sb
            o_ref = out_refs[dest]
            o_ref[:, off + k * LANES:off + (k + 1) * LANES] = yk.astype(o_ref.dtype)
            if km_off is not None:
                km_ref = out_refs[n_out]
                for r in range(tm // MOBA_BLOCK):
                    blk = yk[r * MOBA_BLOCK:(r + 1) * MOBA_BLOCK, :]
                    km_ref[0, r:r + 1, km_off + k * LANES:km_off + (k + 1) * LANES] = jnp.mean(
                        blk, axis=0, keepdims=True)


def _proj(h, g, w, tables, plan, outs, tm, seq, kmean_width=None):
    n, d = h.shape
    per_seq = seq // tm
    out_shape = [jax.ShapeDtypeStruct((n, wd), dt) for wd, dt in outs]
    out_specs = [pl.BlockSpec((tm, wd), lambda i: (i, 0)) for wd, _ in outs]
    if kmean_width is not None:
        nb = tm // MOBA_BLOCK
        out_shape.append(jax.ShapeDtypeStruct((n // tm, nb, kmean_width), F32))
        out_specs.append(pl.BlockSpec((1, nb, kmean_width), lambda i: (i, 0, 0)))
    tab_spec = pl.BlockSpec((tm, LANES), lambda i: (i % per_seq, 0))
    return pl.pallas_call(
        functools.partial(_proj_kernel, plan=tuple(plan), tm=tm, n_out=len(outs)),
        grid=(n // tm,),
        in_specs=[
            pl.BlockSpec((tm, d), lambda i: (i, 0)),
            _resident(g.shape),
            _resident(w.shape),
            tab_spec, tab_spec, tab_spec,
        ],
        out_specs=out_specs,
        out_shape=out_shape,
        compiler_params=_cparams(("parallel",)),
        name="norm_proj",
    )(h, g, w, *tables)


def _outproj_kernel(*refs, n_parts):
    h_ref = refs[0]
    parts = refs[1:1 + n_parts]
    w_ref, g_ref, o_ref = refs[1 + n_parts:]
    a = parts[0][...]
    if n_parts > 1:
        a = a.astype(F32)
        for p in parts[1:]:
            a = a + p[...].astype(F32)
        a = a.astype(BF)
    y = jnp.dot(a, w_ref[...], preferred_element_type=F32)
    o_ref[...] = h_ref[...] + _rms(y, g_ref[...], NORM_EPS)


def _outproj(h, parts, w, g, tm):
    n, d = h.shape
    row = pl.BlockSpec((tm, d), lambda i: (i, 0))
    return pl.pallas_call(
        functools.partial(_outproj_kernel, n_parts=len(parts)),
        grid=(n // tm,),
        in_specs=[row] + [row] * len(parts) + [_resident(w.shape), _resident(g.shape)],
        out_specs=row,
        out_shape=jax.ShapeDtypeStruct((n, d), F32),
        compiler_params=_cparams(("parallel",)),
        name="out_proj",
    )(h, *parts, w, g)


def _softmax_step(s, v, m_ref, l_ref, acc_ref):
    m_prev = m_ref[...]
    m_new = jnp.maximum(m_prev, jnp.max(s, axis=1, keepdims=True))
    alpha = jnp.exp(m_prev - m_new)
    ps = [jnp.exp(s[:, c * LANES:(c + 1) * LANES] - m_new) for c in range(s.shape[1] // LANES)]
    lsum = ps[0]
    for p in ps[1:]:
        lsum = lsum + p
    l_ref[...] = alpha * l_ref[...] + lsum
    p = jnp.concatenate([x.astype(BF) for x in ps], axis=1)
    acc_ref[...] = alpha * acc_ref[...] + jnp.dot(p, v, preferred_element_type=F32)
    m_ref[...] = m_new


def _init_stats(m_ref, l_ref, acc_ref):
    m_ref[...] = jnp.full(m_ref.shape, NEG_INF, F32)
    l_ref[...] = jnp.zeros(l_ref.shape, F32)
    acc_ref[...] = jnp.zeros(acc_ref.shape, F32)


def _normalised(l_ref, acc_ref):
    return acc_ref[...] * (1.0 / jnp.sum(l_ref[...], axis=1, keepdims=True))


def _stats_scratch(rows):
    return [pltpu.VMEM((rows, LANES), F32), pltpu.VMEM((rows, LANES), F32), pltpu.VMEM((rows, LANES), F32)]


def _split_halves(q):
    lane = lax.broadcasted_iota(jnp.int32, q.shape, 1)
    zero = jnp.zeros_like(q)
    return jnp.concatenate([jnp.where(lane < HEAD_DIM, q, zero), jnp.where(lane >= HEAD_DIM, q, zero)], axis=0)


def _diff_kernel(q_ref, k_ref, v_ref, lam_ref, sg_ref, o_ref, m_ref, l_ref, acc_ref, *, t, lambda_init):
    qi = pl.program_id(2)
    qs = _split_halves(q_ref[0] * jnp.asarray(SCALE, BF))
    _init_stats(m_ref, l_ref, acc_ref)

    def tile(j, causal):
        start = pl.multiple_of(j * t, t)
        k = k_ref[0, pl.ds(start, t), :]
        v = v_ref[0, pl.ds(start, t), :]
        s = lax.dot_general(qs, k, _NT, preferred_element_type=F32)
        if causal:
            row = lax.broadcasted_iota(jnp.int32, s.shape, 0) & (t - 1)
            col = lax.broadcasted_iota(jnp.int32, s.shape, 1)
            s = jnp.where(col <= row, s, NEG_INF)
        _softmax_step(s, v, m_ref, l_ref, acc_ref)

    tile(qi, True)

    def body(j, carry):
        tile(j, False)
        return carry

    lax.fori_loop(0, qi, body, 0)

    o = _normalised(l_ref, acc_ref)
    lam = lam_ref[...]
    lam_full = (jnp.exp(jnp.sum(lam[0:1] * lam[1:2], axis=1, keepdims=True))
                - jnp.exp(jnp.sum(lam[2:3] * lam[3:4], axis=1, keepdims=True)) + lambda_init)
    od = o[:t] - lam_full * o[t:]
    od = _rms(od, sg_ref[...], DIFF_SUBLN_EPS) * (1.0 - lambda_init)
    o_ref[0] = od.astype(BF)


def _diff_attention(qkv, lam, subln, batch, seq, t, lambda_init):
    nh = DIFF_HEADS
    qkv3 = qkv.reshape(batch, seq, 3 * nh * LANES)
    out = pl.pallas_call(
        functools.partial(_diff_kernel, t=t, lambda_init=lambda_init),
        grid=(batch, nh, seq // t),
        in_specs=[
            pl.BlockSpec((1, t, LANES), lambda b, h, i: (b, i, h)),
            pl.BlockSpec((1, seq, LANES), lambda b, h, i: (b, 0, nh + h)),
            pl.BlockSpec((1, seq, LANES), lambda b, h, i: (b, 0, 2 * nh + h)),
            pl.BlockSpec(lam.shape, lambda b, h, i: (0, 0)),
            pl.BlockSpec(subln.shape, lambda b, h, i: (0, 0)),
        ],
        out_specs=pl.BlockSpec((1, t, LANES), lambda b, h, i: (b, i, h)),
        out_shape=jax.ShapeDtypeStruct((batch, seq, nh * LANES), BF),
        scratch_shapes=_stats_scratch(2 * t),
        compiler_params=_cparams(("parallel", "parallel", "arbitrary")),
        name="diff_attention",
    )(qkv3, qkv3, qkv3, lam, subln)
    return out.reshape(batch * seq, nh * LANES)


def _moba_kernel(q_ref, k_ref, v_ref, km_ref, o_ref, m_ref, l_ref, acc_ref, *, t, nb):
    qi = pl.program_id(2)
    per_tile = t // MOBA_BLOCK
    q2 = _split_halves(q_ref[0])
    qs = q2 * jnp.asarray(SCALE, BF)
    _init_stats(m_ref, l_ref, acc_ref)

    km = jnp.concatenate([km_ref[0], jnp.zeros((LANES - nb, LANES), F32)], axis=0)
    km_hi = km.astype(BF)
    km_lo = (km - km_hi.astype(F32)).astype(BF)
    gate = (lax.dot_general(q2, km_hi, _NT, preferred_element_type=F32)
            + lax.dot_general(q2, km_lo, _NT, preferred_element_type=F32))
    blk = lax.broadcasted_iota(jnp.int32, gate.shape, 1)
    blk_f = blk.astype(F32)
    own = (qi * t + (lax.broadcasted_iota(jnp.int32, gate.shape, 0) & (t - 1))) >> MOBA_SHIFT
    gate = jnp.where(blk < own, gate, NEG_INF)
    sel = jnp.zeros(gate.shape, F32)
    for _ in range(min(MOBA_TOPK, nb)):
        mx = jnp.max(gate, axis=1, keepdims=True)
        idx = jnp.min(jnp.where(gate == mx, blk_f, float(LANES)), axis=1, keepdims=True)
        hit = blk_f == idx
        sel = jnp.where(hit & (mx > 0.5 * NEG_INF), 1.0, sel)
        gate = jnp.where(hit, REMOVED, gate)
    sel = sel.astype(BF)

    def tile(j, diagonal):
        start = pl.multiple_of(j * t, t)
        k = k_ref[0, pl.ds(start, t), :]
        v = v_ref[0, pl.ds(start, t), :]
        s = lax.dot_general(qs, k, _NT, preferred_element_type=F32)
        eb = lax.broadcasted_iota(jnp.int32, (LANES, t), 0) - j * per_tile
        ec = lax.broadcasted_iota(jnp.int32, (LANES, t), 1) >> MOBA_SHIFT
        expand = jnp.where(eb == ec, 1.0, 0.0).astype(BF)
        keep = jnp.dot(sel, expand, preferred_element_type=F32) > 0.5
        if diagonal:
            row = lax.broadcasted_iota(jnp.int32, s.shape, 0) & (t - 1)
            col = lax.broadcasted_iota(jnp.int32, s.shape, 1)
            keep = keep | (((row >> MOBA_SHIFT) == (col >> MOBA_SHIFT)) & (col <= row))
        _softmax_step(jnp.where(keep, s, NEG_INF), v, m_ref, l_ref, acc_ref)

    tile(qi, True)

    def body(j, carry):
        tile(j, False)
        return carry

    lax.fori_loop(0, qi, body, 0)

    o = _normalised(l_ref, acc_ref)
    lane = lax.broadcasted_iota(jnp.int32, (t, LANES), 1)
    o_ref[0] = jnp.where(lane < HEAD_DIM, o[:t], o[t:]).astype(BF)


def _moba_attention(qkv, kmean, batch, seq, t):
    nb = seq // MOBA_BLOCK
    npair = D_MODEL // LANES
    qkv3 = qkv.reshape(batch, seq, 3 * D_MODEL)
    km3 = kmean.reshape(batch, nb, D_MODEL)
    out = pl.pallas_call(
        functools.partial(_moba_kernel, t=t, nb=nb),
        grid=(batch, npair, seq // t),
        in_specs=[
            pl.BlockSpec((1, t, LANES), lambda b, p, i: (b, i, p)),
            pl.BlockSpec((1, seq, LANES), lambda b, p, i: (b, 0, npair + p)),
            pl.BlockSpec((1, seq, LANES), lambda b, p, i: (b, 0, 2 * npair + p)),
            pl.BlockSpec((1, nb, LANES), lambda b, p, i: (b, 0, p)),
        ],
        out_specs=pl.BlockSpec((1, t, LANES), lambda b, p, i: (b, i, p)),
        out_shape=jax.ShapeDtypeStruct((batch, seq, D_MODEL), BF),
        scratch_shapes=_stats_scratch(2 * t),
        compiler_params=_cparams(("parallel", "parallel", "arbitrary")),
        name="moba_attention",
    )(qkv3, qkv3, qkv3, km3)
    return out.reshape(batch * seq, D_MODEL)


def _compress_kernel(r_ref, pe_ref, w1_ref, w2_ref, o_ref):
    r = r_ref[0, 0, 0]
    ng = r.shape[0]
    a0 = (r + pe_ref[0, 0]).astype(BF)
    a1 = (r + pe_ref[0, 1]).astype(BF)
    y0 = jnp.dot(a0, w1_ref[0, 0], preferred_element_type=F32)
    y1 = jnp.dot(a1, w1_ref[0, 1], preferred_element_type=F32)
    pre = y0 + pltpu.roll(y1, ng - 1, 0)
    hid = pre * (1.0 / (1.0 + jnp.exp(-pre)))
    o_ref[0, 0, 0] = jnp.dot(hid.astype(BF), w2_ref[0], preferred_element_type=F32)


def _compress(r, pe, w1, w2):
    two, batch, ngrp, ng, wd = r.shape
    return pl.pallas_call(
        _compress_kernel,
        grid=(two, batch, ngrp),
        in_specs=[
            pl.BlockSpec((1, 1, 1, ng, wd), lambda a, b, g: (a, b, g, 0, 0)),
            pl.BlockSpec((1, 2, 1, wd), lambda a, b, g: (a, 0, 0, 0)),
            pl.BlockSpec((1, 2, wd, CMP_HIDDEN), lambda a, b, g: (a, 0, 0, 0)),
            pl.BlockSpec((1, CMP_HIDDEN, HEAD_DIM), lambda a, b, g: (a, 0, 0)),
        ],
        out_specs=pl.BlockSpec((1, 1, 1, ng, HEAD_DIM), lambda a, b, g: (a, b, g, 0, 0)),
        out_shape=jax.ShapeDtypeStruct((two, batch, ngrp, ng, HEAD_DIM), F32),
        compiler_params=_cparams(("parallel", "parallel", "parallel")),
        name="nsa_compress",
    )(r, pe, w1, w2)


def _nsa_stack_q(q, t):
    lane = lax.broadcasted_iota(jnp.int32, (t, LANES), 1)
    zero = jnp.zeros((t, LANES), q.dtype)
    parts = []
    for half in range(2):
        keep = (lane < HEAD_DIM) if half == 0 else (lane >= HEAD_DIM)
        for r in range(4):
            parts.append(jnp.where(keep, q[:, r * LANES:(r + 1) * LANES], zero))
    return jnp.concatenate(parts, axis=0)


def _nsa_write(o, gl, branch, o_ref, t):
    lane = lax.broadcasted_iota(jnp.int32, (t, LANES), 1)
    for r in range(4):
        c0 = r * 3 + branch
        c1 = 12 + r * 3 + branch
        g0 = 1.0 / (1.0 + jnp.exp(-gl[:, c0:c0 + 1]))
        g1 = 1.0 / (1.0 + jnp.exp(-gl[:, c1:c1 + 1]))
        blk = jnp.where(lane < HEAD_DIM, g0 * o[r * t:(r + 1) * t], g1 * o[(4 + r) * t:(5 + r) * t])
        o_ref[0, :, r * LANES:(r + 1) * LANES] = blk.astype(BF)


def _nsa_cmp_kernel(q_ref, kc_ref, vc_ref, gl_ref, ovl_ref, o_ref, sel_ref, *, t, ncmp):
    qi = pl.program_id(2)
    qs = _nsa_stack_q(q_ref[0] * jnp.asarray(SCALE, BF), t)
    s = lax.dot_general(qs, kc_ref[0], _NT, preferred_element_type=F32)
    n_idx = lax.broadcasted_iota(jnp.int32, s.shape, 1)
    qidx = qi * t + (lax.broadcasted_iota(jnp.int32, s.shape, 0) & (t - 1))
    mask = (n_idx * CMP_STRIDE + (CMP_LEN - 1)) <= qidx
    sm = jnp.where(mask, s, NEG_INF)
    p = jnp.where(mask, jnp.exp(sm - jnp.max(sm, axis=1, keepdims=True)), 0.0)
    l = jnp.sum(p, axis=1, keepdims=True)
    pn = p * (1.0 / jnp.where(l > 0.0, l, 1.0))
    o = jnp.dot(pn.astype(BF), vc_ref[0], preferred_element_type=F32)
    _nsa_write(o, gl_ref[0], 0, o_ref, t)

    nsel = ovl_ref.shape[0]
    jb = lax.broadcasted_iota(jnp.int32, (nsel, t), 0)
    qblk = (qi * t + lax.broadcasted_iota(jnp.int32, (nsel, t), 1)) >> SLC_SHIFT
    forced = (jb == 0) | (jb == qblk) | (jb == qblk - 1)
    ovl = ovl_ref[...]
    for half in range(2):
        ps = pn[(half * 4) * t:(half * 4 + 1) * t]
        for r in range(1, 4):
            ps = ps + pn[(half * 4 + r) * t:(half * 4 + r + 1) * t]
        ps_hi = ps.astype(BF)
        ps_lo = (ps - ps_hi.astype(F32)).astype(BF)
        imp = (lax.dot_general(ovl, ps_hi, _NT, preferred_element_type=F32)
               + lax.dot_general(ovl, ps_lo, _NT, preferred_element_type=F32))
        val = jnp.where(jb <= qblk, jnp.where(forced, FORCED_SCORE, imp), NEG_INF)
        cnt = jnp.zeros((nsel, t), F32)
        for i in range(nsel):
            vi = val[i:i + 1, :]
            ahead = (vi > val) | ((vi == val) & (jb > i))
            cnt = cnt + jnp.where(ahead, 1.0, 0.0)
        chosen = jnp.where((cnt < float(min(SLC_TOPK, nsel))) & (jb <= qblk), 1.0, 0.0)
        pad = jnp.zeros((LANES - nsel, t), F32)
        sel_ref[0, 0, half] = jnp.concatenate([chosen, pad], axis=0).T.astype(BF)


def _nsa_sel_kernel(q_ref, k_ref, v_ref, gl_ref, sel_ref, o_ref, m_ref, l_ref, acc_ref, *, t, tk):
    qi = pl.program_id(2)
    qs = _nsa_stack_q(q_ref[0] * jnp.asarray(SCALE, BF), t)
    sel = jnp.concatenate([sel_ref[0, 0, 0]] * 4 + [sel_ref[0, 0, 1]] * 4, axis=0)
    _init_stats(m_ref, l_ref, acc_ref)
    jd = (qi * t) // tk

    def tile(j, causal):
        start = pl.multiple_of(j * tk, tk)
        k = k_ref[0, pl.ds(start, tk), :]
        v = v_ref[0, pl.ds(start, tk), :]
        s = lax.dot_general(qs, k, _NT, preferred_element_type=F32)
        eb = lax.broadcasted_iota(jnp.int32, (LANES, tk), 0) - j * (tk // SLC_BLOCK)
        ec = lax.broadcasted_iota(jnp.int32, (LANES, tk), 1) >> SLC_SHIFT
        expand = jnp.where(eb == ec, 1.0, 0.0).astype(BF)
        keep = jnp.dot(sel, expand, preferred_element_type=F32) > 0.5
        if causal:
            qidx = qi * t + (lax.broadcasted_iota(jnp.int32, s.shape, 0) & (t - 1))
            kidx = j * tk + lax.broadcasted_iota(jnp.int32, s.shape, 1)
            keep = keep & (kidx <= qidx)
        _softmax_step(jnp.where(keep, s, NEG_INF), v, m_ref, l_ref, acc_ref)

    tile(jd, True)

    def body(j, carry):
        tile(j, False)
        return carry

    lax.fori_loop(0, jd, body, 0)
    _nsa_write(_normalised(l_ref, acc_ref), gl_ref[0], 1, o_ref, t)


def _nsa_win_kernel(q_ref, k_ref, v_ref, gl_ref, o_ref, m_ref, l_ref, acc_ref, *, t, tk):
    qi = pl.program_id(2)
    qs = _nsa_stack_q(q_ref[0] * jnp.asarray(SCALE, BF), t)
    _init_stats(m_ref, l_ref, acc_ref)
    jd = (qi * t) // tk

    def tile(j):
        start = pl.multiple_of(j * tk, tk)
        k = k_ref[0, pl.ds(start, tk), :]
        v = v_ref[0, pl.ds(start, tk), :]
        s = lax.dot_general(qs, k, _NT, preferred_element_type=F32)
        qidx = qi * t + (lax.broadcasted_iota(jnp.int32, s.shape, 0) & (t - 1))
        kidx = j * tk + lax.broadcasted_iota(jnp.int32, s.shape, 1)
        dist = qidx - kidx
        _softmax_step(jnp.where((dist >= 0) & (dist < WINDOW), s, NEG_INF), v, m_ref, l_ref, acc_ref)

    tile(jd)
    lo = jnp.maximum(jd - (WINDOW + tk - 1) // tk, 0)

    def body(j, carry):
        tile(j)
        return carry

    lax.fori_loop(lo, jd, body, 0)
    _nsa_write(_normalised(l_ref, acc_ref), gl_ref[0], 2, o_ref, t)


def _nsa_attention(pa, pf, kcmp, vcmp, ovl_t, batch, seq, t, tk):
    pa3 = pa.reshape(batch, seq, pa.shape[1])
    pf3 = pf.reshape(batch, seq, pf.shape[1])
    nq = seq // t
    grid = (batch, 2, nq)
    ncmp = kcmp.shape[1]
    sem = _cparams(("parallel", "parallel", "arbitrary"))
    q_spec = pl.BlockSpec((1, t, 4 * LANES), lambda b, g, i: (b, i, g))
    gl_spec = pl.BlockSpec((1, t, LANES), lambda b, g, i: (b, i, 4 + g))
    o_spec = pl.BlockSpec((1, t, 4 * LANES), lambda b, g, i: (b, i, g))
    o_shape = jax.ShapeDtypeStruct((batch, seq, D_MODEL), BF)
    stats = _stats_scratch(8 * t)

    def kv_spec(col_block):
        return pl.BlockSpec((1, seq, LANES), lambda b, g, i: (b, 0, col_block + g))

    o_c, sel = pl.pallas_call(
        functools.partial(_nsa_cmp_kernel, t=t, ncmp=ncmp),
        grid=grid,
        in_specs=[
            q_spec,
            pl.BlockSpec((1, ncmp, LANES), lambda b, g, i: (b, 0, g)),
            pl.BlockSpec((1, ncmp, LANES), lambda b, g, i: (b, 0, g)),
            gl_spec,
            pl.BlockSpec(ovl_t.shape, lambda b, g, i: (0, 0)),
        ],
        out_specs=[o_spec, pl.BlockSpec((1, 1, 2, t, LANES), lambda b, g, i: (b, g, 0, i, 0))],
        out_shape=[o_shape, jax.ShapeDtypeStruct((batch, 2, 2, seq, LANES), BF)],
        compiler_params=sem,
        name="nsa_compressed",
    )(pa3, kcmp, vcmp, pf3, ovl_t)

    o_s = pl.pallas_call(
        functools.partial(_nsa_sel_kernel, t=t, tk=tk),
        grid=grid,
        in_specs=[q_spec, kv_spec(8), kv_spec(12), gl_spec,
                  pl.BlockSpec((1, 1, 2, t, LANES), lambda b, g, i: (b, g, 0, i, 0))],
        out_specs=o_spec,
        out_shape=o_shape,
        scratch_shapes=stats,
        compiler_params=sem,
        name="nsa_selected",
    )(pa3, pa3, pa3, pf3, sel)

    o_w = pl.pallas_call(
        functools.partial(_nsa_win_kernel, t=t, tk=tk),
        grid=grid,
        in_specs=[q_spec, kv_spec(10), kv_spec(14), gl_spec],
        out_specs=o_spec,
        out_shape=o_shape,
        scratch_shapes=stats,
        compiler_params=sem,
        name="nsa_window",
    )(pa3, pa3, pa3, pf3)
    n = batch * seq
    return [o_c.reshape(n, D_MODEL), o_s.reshape(n, D_MODEL), o_w.reshape(n, D_MODEL)]


def _rope_tables(seq):
    half = ROT_DIM // 2
    inv_freq = ROPE_THETA ** (-jnp.arange(half, dtype=F32) / half)
    ang = jnp.arange(seq).astype(F32)[:, None] * inv_freq[None, :]
    cos, sin = jnp.cos(ang), jnp.sin(ang)
    d = np.arange(LANES) % HEAD_DIM
    idx = d % half
    cos_t = jnp.where(d[None, :] < ROT_DIM, cos[:, idx], 1.0)
    sa_t = jnp.where(d[None, :] < half, -sin[:, idx], 0.0)
    sb_t = jnp.where((d[None, :] >= half) & (d[None, :] < ROT_DIM), sin[:, idx], 0.0)
    return cos_t.astype(F32), sa_t.astype(F32), sb_t.astype(F32)


def _nsa_q_perm():
    cols = []
    for gp in range(2):
        for r in range(4):
            for half in range(2):
                head = 8 * gp + 4 * half + r
                cols.extend(range(head * HEAD_DIM, (head + 1) * HEAD_DIM))
    return np.asarray(cols, np.int32)


def _nsa_gate_cols():
    src = -np.ones(2 * LANES, np.int32)
    for gp in range(2):
        for half in range(2):
            for r in range(4):
                for br in range(3):
                    src[gp * LANES + half * 12 + r * 3 + br] = (4 * (2 * gp + half) + r) * 3 + br
    return src


def _ffn_weights(w_in, w_out):
    d = w_in.shape[0]
    nc = D_FF // FF_CHUNK
    wg = w_in[:, :D_FF].reshape(d, nc, FF_CHUNK).transpose(1, 0, 2).astype(BF)
    wu = w_in[:, D_FF:].reshape(d, nc, FF_CHUNK).transpose(1, 0, 2).astype(BF)
    wo = w_out.reshape(nc, FF_CHUNK, d).astype(BF)
    return wg, wu, wo


def kernel(x, norm_g, ffn_w_in, ffn_w_out, diff_w_in, diff_w_out, diff_lambda, diff_subln, moba_w_in, moba_w_out,
           nsa_w_in, nsa_w_out, nsa_cmp_pe, nsa_cmp_w1, nsa_cmp_w2):
    batch, seq, d = x.shape
    n = batch * seq
    tm = 512
    h = x.reshape(n, d)
    tables = _rope_tables(seq)

    for i in range(DEPTH):
        g = norm_g[i]
        wg, wu, wo = _ffn_weights(ffn_w_in[i, 0], ffn_w_out[i, 0])
        h = _ffn(h, g[0:2], wg, wu, wo, tm)

        kind, j = i % 3, i // 3
        if kind == 0:
            lambda_init = 0.8 - 0.6 * math.exp(-0.3 * i)
            plan = [(c * 256, 256, c < 8, 0, c * 256, None) for c in range(12)]
            (qkv,) = _proj(h, g[2:3], diff_w_in[j].astype(BF), tables, plan, [(3 * D_MODEL, BF)], tm, seq)
            attn = _diff_attention(qkv, diff_lambda[j], diff_subln[j].reshape(1, LANES), batch, seq, 512,
                                   lambda_init)
            h = _outproj(h, [attn], diff_w_out[j].astype(BF), g[3:4], tm)
        elif kind == 1:
            plan = [(c * 256, 256, c < 8, 0, c * 256, (c - 4) * 256 if 4 <= c < 8 else None) for c in range(12)]
            qkv, kmean = _proj(h, g[2:3], moba_w_in[j].astype(BF), tables, plan, [(3 * D_MODEL, BF)], tm, seq,
                               kmean_width=D_MODEL)
            attn = _moba_attention(qkv, kmean, batch, seq, 512)
            h = _outproj(h, [attn], moba_w_out[j].astype(BF), g[3:4], tm)
        else:
            w = nsa_w_in[j]
            perm = _nsa_q_perm()
            kvw = NSA_GROUPS * HEAD_DIM
            base = NSA_HEADS * HEAD_DIM
            seg = {name: w[:, base + k * kvw: base + (k + 1) * kvw]
                   for k, name in enumerate(["kc", "vc", "ks", "vs", "kw", "vw"])}
            gsrc = _nsa_gate_cols()
            glog = w[:, base + 6 * kvw:]
            gate_w = jnp.where(gsrc[None, :] >= 0, glog[:, np.maximum(gsrc, 0)], 0.0)
            w_all = jnp.concatenate([w[:, perm], seg["ks"], seg["kw"], seg["kc"], seg["vs"], seg["vw"], seg["vc"],
                                     gate_w], axis=1).astype(BF)
            plan = [(c * 256, 256, True, 0, c * 256, None) for c in range(6)]
            plan.append((1536, 256, True, 1, 0, None))
            plan.append((1792, 256, False, 0, 1536, None))
            plan.append((2048, 256, False, 0, 1792, None))
            plan.append((2304, 256, False, 1, 256, None))
            plan.append((2560, 256, False, 1, 512, None))
            pa, pf = _proj(h, g[2:3], w_all, tables, plan, [(2048, BF), (768, F32)], tm, seq)

            ng = seq // CMP_STRIDE
            kcvc = pf[:, :2 * kvw].reshape(batch, seq, 2, NSA_GROUPS, HEAD_DIM).transpose(2, 0, 3, 1, 4)
            r = kcvc.reshape(2, batch, NSA_GROUPS, ng, CMP_STRIDE * HEAD_DIM)
            pe = nsa_cmp_pe[j].reshape(2, 2, 1, CMP_STRIDE * HEAD_DIM)
            w1 = nsa_cmp_w1[j].reshape(2, 2, CMP_STRIDE * HEAD_DIM, CMP_HIDDEN).astype(BF)
            w2 = nsa_cmp_w2[j].astype(BF)
            cmp_out = _compress(r, pe, w1, w2)
            cmp_tm = cmp_out.transpose(0, 1, 3, 2, 4).reshape(2, batch, ng, kvw).astype(BF)

            nsel = seq // SLC_BLOCK
            cs = np.arange(ng)[:, None] * CMP_STRIDE
            bs = np.arange(nsel)[None, :] * SLC_BLOCK
            ovl = ((cs < bs + SLC_BLOCK) & (cs + CMP_LEN > bs)).astype(np.float32)
            ovl[ng - 1, :] = 0.0
            ovl_t = jnp.asarray(ovl.T, BF)
            parts = _nsa_attention(pa, pf, cmp_tm[0], cmp_tm[1], ovl_t, batch, seq, 128, 512)
            h = _outproj(h, parts, nsa_w_out[j][perm, :].astype(BF), g[3:4], tm)

        wg, wu, wo = _ffn_weights(ffn_w_in[i, 1], ffn_w_out[i, 1])
        h = _ffn(h, g[4:6], wg, wu, wo, tm)
    return h.reshape(batch, seq, d)
```

```python
import functools
import math

import jax
import jax.numpy as jnp
import numpy as np
from jax import lax
from jax.experimental import pallas as pl
from jax.experimental.pallas import tpu as pltpu

D_MODEL = 1024
DEPTH = 4
HEAD_DIM = 64
ROT_DIM = HEAD_DIM // 4
ROPE_THETA = 500000.0
NORM_EPS = 1e-6
NEG_INF = -1e30
REMOVED = -3e38

DIFF_HEADS = 8
DIFF_SUBLN_EPS = 1e-5
MOBA_BLOCK = 256
MOBA_SHIFT = 8
MOBA_TOPK = 3
NSA_HEADS = 16
NSA_GROUPS = 4
CMP_LEN = 32
CMP_STRIDE = 16
CMP_HIDDEN = 256
SLC_BLOCK = 64
SLC_SHIFT = 6
SLC_TOPK = 16
WINDOW = 512
FORCED_SCORE = 1e9
D_FF = 2816

LANES = 128
FF_CHUNK = 256
SCALE = HEAD_DIM ** -0.5

BF = jnp.bfloat16
F32 = jnp.float32
VMEM_LIMIT = 56 * 1024 * 1024

_NT = (((1,), (1,)), ((), ()))


def _cparams(sem):
    return pltpu.CompilerParams(dimension_semantics=sem, vmem_limit_bytes=VMEM_LIMIT)


def _rms(x, g, eps):
    return x * lax.rsqrt(jnp.mean(x * x, axis=-1, keepdims=True) + eps) * g


def _resident(shape):
    nd = len(shape)
    return pl.BlockSpec(shape, lambda *_: (0,) * nd, pipeline_mode=pl.Buffered(1))


def _ffn_kernel(h_ref, g_ref, wg_ref, wu_ref, wo_ref, o_ref, acc_ref, act_ref, *, n_chunks):
    h = h_ref[...]
    xn = _rms(h, g_ref[0:1, :], NORM_EPS).astype(BF)

    def act(c):
        gate = jnp.dot(xn, wg_ref[c], preferred_element_type=F32)
        up = jnp.dot(xn, wu_ref[c], preferred_element_type=F32)
        return ((gate * (1.0 / (1.0 + jnp.exp(-gate)))) * up).astype(BF)

    act_ref[...] = act(0)
    acc_ref[...] = jnp.zeros_like(acc_ref)

    def body(c, carry):
        prev = act_ref[...]
        nxt = act(c)
        acc_ref[...] += jnp.dot(prev, wo_ref[c - 1], preferred_element_type=F32)
        act_ref[...] = nxt
        return carry

    lax.fori_loop(1, n_chunks, body, 0)
    y = acc_ref[...] + jnp.dot(act_ref[...], wo_ref[n_chunks - 1], preferred_element_type=F32)
    o_ref[...] = h + 0.5 * _rms(y, g_ref[1:2, :], NORM_EPS)


def _ffn(h, g2, wg, wu, wo, tm):
    n, d = h.shape
    nc = wg.shape[0]
    return pl.pallas_call(
        functools.partial(_ffn_kernel, n_chunks=nc),
        grid=(n // tm,),
        in_specs=[
            pl.BlockSpec((tm, d), lambda i: (i, 0)),
            _resident(g2.shape),
            _resident(wg.shape),
            _resident(wu.shape),
            _resident(wo.shape),
        ],
        out_specs=pl.BlockSpec((tm, d), lambda i: (i, 0)),
        out_shape=jax.ShapeDtypeStruct((n, d), F32),
        scratch_shapes=[pltpu.VMEM((tm, d), F32), pltpu.VMEM((tm, FF_CHUNK), BF)],
        compiler_params=_cparams(("parallel",)),
        name="ffn_halfstep",
    )(h, g2, wg, wu, wo)


def _proj_kernel(h_ref, g_ref, w_ref, cos_ref, sa_ref, sb_ref, *out_refs, plan, tm, n_out):
    xn = _rms(h_ref[...], g_ref[...], NORM_EPS).astype(BF)
    cos = cos_ref[...]
    sa = sa_ref[...]
    sb = sb_ref[...]
    for col, width, rope, dest, off, km_off in plan:
        y = jnp.dot(xn, w_ref[:, col:col + width], preferred_element_type=F32)
        for k in range(width // LANES):
            yk = y[:, k * LANES:(k + 1) * LANES]
            if rope:
                yk = yk * cos + pltpu.roll(yk, LANES - ROT_DIM // 2, 1) * sa + pltpu.roll(yk, ROT_DIM // 2, 1) * sb
            o_ref = out_refs[dest]
            o_ref[:, off + k * LANES:off + (k + 1) * LANES] = yk.astype(o_ref.dtype)
            if km_off is not None:
                km_ref = out_refs[n_out]
                for r in range(tm // MOBA_BLOCK):
                    blk = yk[r * MOBA_BLOCK:(r + 1) * MOBA_BLOCK, :]
                    km_ref[0, r:r + 1, km_off + k * LANES:km_off + (k + 1) * LANES] = jnp.mean(
                        blk, axis=0, keepdims=True)


def _proj(h, g, w, tables, plan, outs, tm, seq, kmean_width=None):
    n, d = h.shape
    per_seq = seq // tm
    out_shape = [jax.ShapeDtypeStruct((n, wd), dt) for wd, dt in outs]
    out_specs = [pl.BlockSpec((tm, wd), lambda i: (i, 0)) for wd, _ in outs]
    if kmean_width is not None:
        nb = tm // MOBA_BLOCK
        out_shape.append(jax.ShapeDtypeStruct((n // tm, nb, kmean_width), F32))
        out_specs.append(pl.BlockSpec((1, nb, kmean_width), lambda i: (i, 0, 0)))
    tab_spec = pl.BlockSpec((tm, LANES), lambda i: (i % per_seq, 0))
    return pl.pallas_call(
        functools.partial(_proj_kernel, plan=tuple(plan), tm=tm, n_out=len(outs)),
        grid=(n // tm,),
        in_specs=[
            pl.BlockSpec((tm, d), lambda i: (i, 0)),
            _resident(g.shape),
            _resident(w.shape),
            tab_spec, tab_spec, tab_spec,
        ],
        out_specs=out_specs,
        out_shape=out_shape,
        compiler_params=_cparams(("parallel",)),
        name="norm_proj",
    )(h, g, w, *tables)


def _outproj_kernel(*refs, n_parts):
    h_ref = refs[0]
    parts = refs[1:1 + n_parts]
    w_ref, g_ref, o_ref = refs[1 + n_parts:]
    a = parts[0][...]
    if n_parts > 1:
        a = a.astype(F32)
        for p in parts[1:]:
            a = a + p[...].astype(F32)
        a = a.astype(BF)
    y = jnp.dot(a, w_ref[...], preferred_element_type=F32)
    o_ref[...] = h_ref[...] + _rms(y, g_ref[...], NORM_EPS)


def _outproj(h, parts, w, g, tm):
    n, d = h.shape
    row = pl.BlockSpec((tm, d), lambda i: (i, 0))
    return pl.pallas_call(
        functools.partial(_outproj_kernel, n_parts=len(parts)),
        grid=(n // tm,),
        in_specs=[row] + [row] * len(parts) + [_resident(w.shape), _resident(g.shape)],
        out_specs=row,
        out_shape=jax.ShapeDtypeStruct((n, d), F32),
        compiler_params=_cparams(("parallel",)),
        name="out_proj",
    )(h, *parts, w, g)


def _softmax_step(s, v, m_ref, l_ref, acc_ref):
    m_prev = m_ref[...]
    m_new = jnp.maximum(m_prev, jnp.max(s, axis=1, keepdims=True))
    alpha = jnp.exp(m_prev - m_new)
    ps = [jnp.exp(s[:, c * LANES:(c + 1) * LANES] - m_new) for c in range(s.shape[1] // LANES)]
    lsum = ps[0]
    for p in ps[1:]:
        lsum = lsum + p
    l_ref[...] = alpha * l_ref[...] + lsum
    p = jnp.concatenate([x.astype(BF) for x in ps], axis=1)
    acc_ref[...] = alpha * acc_ref[...] + jnp.dot(p, v, preferred_element_type=F32)
    m_ref[...] = m_new


def _init_stats(m_ref, l_ref, acc_ref):
    m_ref[...] = jnp.full(m_ref.shape, NEG_INF, F32)
    l_ref[...] = jnp.zeros(l_ref.shape, F32)
    acc_ref[...] = jnp.zeros(acc_ref.shape, F32)


def _normalised(l_ref, acc_ref):
    return acc_ref[...] * (1.0 / jnp.sum(l_ref[...], axis=1, keepdims=True))


def _stats_scratch(rows):
    return [pltpu.VMEM((rows, LANES), F32), pltpu.VMEM((rows, LANES), F32), pltpu.VMEM((rows, LANES), F32)]


def _split_halves(q):
    lane = lax.broadcasted_iota(jnp.int32, q.shape, 1)
    zero = jnp.zeros_like(q)
    return jnp.concatenate([jnp.where(lane < HEAD_DIM, q, zero), jnp.where(lane >= HEAD_DIM, q, zero)], axis=0)


def _diff_kernel(q_ref, k_ref, v_ref, lam_ref, sg_ref, o_ref, m_ref, l_ref, acc_ref, *, t, lambda_init):
    qi = pl.program_id(2)
    qs = _split_halves(q_ref[0] * jnp.asarray(SCALE, BF))
    _init_stats(m_ref, l_ref, acc_ref)

    def tile(j, causal):
        start = pl.multiple_of(j * t, t)
        k = k_ref[0, pl.ds(start, t), :]
        v = v_ref[0, pl.ds(start, t), :]
        s = lax.dot_general(qs, k, _NT, preferred_element_type=F32)
        if causal:
            row = lax.broadcasted_iota(jnp.int32, s.shape, 0) & (t - 1)
            col = lax.broadcasted_iota(jnp.int32, s.shape, 1)
            s = jnp.where(col <= row, s, NEG_INF)
        _softmax_step(s, v, m_ref, l_ref, acc_ref)

    tile(qi, True)

    def body(j, carry):
        tile(j, False)
        return carry

    lax.fori_loop(0, qi, body, 0)

    o = _normalised(l_ref, acc_ref)
    lam = lam_ref[...]
    lam_full = (jnp.exp(jnp.sum(lam[0:1] * lam[1:2], axis=1, keepdims=True))
                - jnp.exp(jnp.sum(lam[2:3] * lam[3:4], axis=1, keepdims=True)) + lambda_init)
    od = o[:t] - lam_full * o[t:]
    od = _rms(od, sg_ref[...], DIFF_SUBLN_EPS) * (1.0 - lambda_init)
    o_ref[0] = od.astype(BF)


def _diff_attention(qkv, lam, subln, batch, seq, t, lambda_init):
    nh = DIFF_HEADS
    qkv3 = qkv.reshape(batch, seq, 3 * nh * LANES)
    out = pl.pallas_call(
        functools.partial(_diff_kernel, t=t, lambda_init=lambda_init),
        grid=(batch, nh, seq // t),
        in_specs=[
            pl.BlockSpec((1, t, LANES), lambda b, h, i: (b, i, h)),
            pl.BlockSpec((1, seq, LANES), lambda b, h, i: (b, 0, nh + h)),
            pl.BlockSpec((1, seq, LANES), lambda b, h, i: (b, 0, 2 * nh + h)),
            pl.BlockSpec(lam.shape, lambda b, h, i: (0, 0)),
            pl.BlockSpec(subln.shape, lambda b, h, i: (0, 0)),
        ],
        out_specs=pl.BlockSpec((1, t, LANES), lambda b, h, i: (b, i, h)),
        out_shape=jax.ShapeDtypeStruct((batch, seq, nh * LANES), BF),
        scratch_shapes=_stats_scratch(2 * t),
        compiler_params=_cparams(("parallel", "parallel", "arbitrary")),
        name="diff_attention",
    )(qkv3, qkv3, qkv3, lam, subln)
    return out.reshape(batch * seq, nh * LANES)


def _moba_kernel(q_ref, k_ref, v_ref, km_ref, ind_ref, o_ref, m_ref, l_ref, acc_ref, *, t, nb):
    qi = pl.program_id(2)
    q2 = _split_halves(q_ref[0])
    qs = q2 * jnp.asarray(SCALE, BF)
    _init_stats(m_ref, l_ref, acc_ref)

    km = jnp.concatenate([km_ref[0], jnp.zeros((LANES - nb, LANES), F32)], axis=0)
    km_hi = km.astype(BF)
    km_lo = (km - km_hi.astype(F32)).astype(BF)
    gate = (lax.dot_general(q2, km_hi, _NT, preferred_element_type=F32)
            + lax.dot_general(q2, km_lo, _NT, preferred_element_type=F32))
    blk = lax.broadcasted_iota(jnp.int32, gate.shape, 1)
    blk_f = blk.astype(F32)
    own = (qi * t + (lax.broadcasted_iota(jnp.int32, gate.shape, 0) & (t - 1))) >> MOBA_SHIFT
    gate = jnp.where(blk < own, gate, NEG_INF)
    sel = jnp.zeros(gate.shape, F32)
    for _ in range(min(MOBA_TOPK, nb)):
        mx = jnp.max(gate, axis=1, keepdims=True)
        idx = jnp.min(jnp.where(gate == mx, blk_f, float(LANES)), axis=1, keepdims=True)
        hit = blk_f == idx
        sel = jnp.where(hit & (mx > 0.5 * NEG_INF), 1.0, sel)
        gate = jnp.where(hit, REMOVED, gate)
    bias = jnp.where((sel > 0.5) | (blk == own), 0.0, NEG_INF).astype(BF)
    q_aug = jnp.concatenate([qs, bias], axis=1)

    def tile(j, diagonal):
        start = pl.multiple_of(j * t, t)
        k_aug = jnp.concatenate([k_ref[0, pl.ds(start, t), :], ind_ref[pl.ds(start, t), :]], axis=1)
        v = v_ref[0, pl.ds(start, t), :]
        s = lax.dot_general(q_aug, k_aug, _NT, preferred_element_type=F32)
        if diagonal:
            row = lax.broadcasted_iota(jnp.int32, s.shape, 0) & (t - 1)
            col = lax.broadcasted_iota(jnp.int32, s.shape, 1)
            s = jnp.where(((row >> MOBA_SHIFT) == (col >> MOBA_SHIFT)) & (col > row), NEG_INF, s)
        _softmax_step(s, v, m_ref, l_ref, acc_ref)

    tile(qi, True)

    def body(j, carry):
        tile(j, False)
        return carry

    lax.fori_loop(0, qi, body, 0)

    o = _normalised(l_ref, acc_ref)
    lane = lax.broadcasted_iota(jnp.int32, (t, LANES), 1)
    o_ref[0] = jnp.where(lane < HEAD_DIM, o[:t], o[t:]).astype(BF)


def _moba_attention(qkv, kmean, batch, seq, t):
    nb = seq // MOBA_BLOCK
    npair = D_MODEL // LANES
    qkv3 = qkv.reshape(batch, seq, 3 * D_MODEL)
    km3 = kmean.reshape(batch, nb, D_MODEL)
    ind = _block_indicator(seq, MOBA_SHIFT)
    out = pl.pallas_call(
        functools.partial(_moba_kernel, t=t, nb=nb),
        grid=(batch, npair, seq // t),
        in_specs=[
            pl.BlockSpec((1, t, LANES), lambda b, p, i: (b, i, p)),
            pl.BlockSpec((1, seq, LANES), lambda b, p, i: (b, 0, npair + p)),
            pl.BlockSpec((1, seq, LANES), lambda b, p, i: (b, 0, 2 * npair + p)),
            pl.BlockSpec((1, nb, LANES), lambda b, p, i: (b, 0, p)),
            pl.BlockSpec(ind.shape, lambda b, p, i: (0, 0)),
        ],
        out_specs=pl.BlockSpec((1, t, LANES), lambda b, p, i: (b, i, p)),
        out_shape=jax.ShapeDtypeStruct((batch, seq, D_MODEL), BF),
        scratch_shapes=_stats_scratch(2 * t),
        compiler_params=_cparams(("parallel", "parallel", "arbitrary")),
        name="moba_attention",
    )(qkv3, qkv3, qkv3, km3, ind)
    return out.reshape(batch * seq, D_MODEL)


def _compress_kernel(r_ref, pe_ref, w1_ref, w2_ref, o_ref):
    r = r_ref[0, 0, 0]
    ng = r.shape[0]
    a0 = (r + pe_ref[0, 0]).astype(BF)
    a1 = (r + pe_ref[0, 1]).astype(BF)
    y0 = jnp.dot(a0, w1_ref[0, 0], preferred_element_type=F32)
    y1 = jnp.dot(a1, w1_ref[0, 1], preferred_element_type=F32)
    pre = y0 + pltpu.roll(y1, ng - 1, 0)
    hid = pre * (1.0 / (1.0 + jnp.exp(-pre)))
    o_ref[0, 0, 0] = jnp.dot(hid.astype(BF), w2_ref[0], preferred_element_type=F32)


def _compress(r, pe, w1, w2):
    two, batch, ngrp, ng, wd = r.shape
    return pl.pallas_call(
        _compress_kernel,
        grid=(two, batch, ngrp),
        in_specs=[
            pl.BlockSpec((1, 1, 1, ng, wd), lambda a, b, g: (a, b, g, 0, 0)),
            pl.BlockSpec((1, 2, 1, wd), lambda a, b, g: (a, 0, 0, 0)),
            pl.BlockSpec((1, 2, wd, CMP_HIDDEN), lambda a, b, g: (a, 0, 0, 0)),
            pl.BlockSpec((1, CMP_HIDDEN, HEAD_DIM), lambda a, b, g: (a, 0, 0)),
        ],
        out_specs=pl.BlockSpec((1, 1, 1, ng, HEAD_DIM), lambda a, b, g: (a, b, g, 0, 0)),
        out_shape=jax.ShapeDtypeStruct((two, batch, ngrp, ng, HEAD_DIM), F32),
        compiler_params=_cparams(("parallel", "parallel", "parallel")),
        name="nsa_compress",
    )(r, pe, w1, w2)


def _nsa_stack_q(q, t):
    lane = lax.broadcasted_iota(jnp.int32, (t, LANES), 1)
    zero = jnp.zeros((t, LANES), q.dtype)
    parts = []
    for half in range(2):
        keep = (lane < HEAD_DIM) if half == 0 else (lane >= HEAD_DIM)
        for r in range(4):
            parts.append(jnp.where(keep, q[:, r * LANES:(r + 1) * LANES], zero))
    return jnp.concatenate(parts, axis=0)


def _nsa_write(o, gl, branch, o_ref, t):
    lane = lax.broadcasted_iota(jnp.int32, (t, LANES), 1)
    for r in range(4):
        c0 = r * 3 + branch
        c1 = 12 + r * 3 + branch
        g0 = 1.0 / (1.0 + jnp.exp(-gl[:, c0:c0 + 1]))
        g1 = 1.0 / (1.0 + jnp.exp(-gl[:, c1:c1 + 1]))
        blk = jnp.where(lane < HEAD_DIM, g0 * o[r * t:(r + 1) * t], g1 * o[(4 + r) * t:(5 + r) * t])
        o_ref[0, :, r * LANES:(r + 1) * LANES] = blk.astype(BF)


def _nsa_cmp_kernel(q_ref, kc_ref, vc_ref, gl_ref, ovl_ref, o_ref, sel_ref, *, t, ncmp):
    qi = pl.program_id(2)
    qs = _nsa_stack_q(q_ref[0] * jnp.asarray(SCALE, BF), t)
    s = lax.dot_general(qs, kc_ref[0], _NT, preferred_element_type=F32)
    n_idx = lax.broadcasted_iota(jnp.int32, s.shape, 1)
    qidx = qi * t + (lax.broadcasted_iota(jnp.int32, s.shape, 0) & (t - 1))
    mask = (n_idx * CMP_STRIDE + (CMP_LEN - 1)) <= qidx
    sm = jnp.where(mask, s, NEG_INF)
    p = jnp.where(mask, jnp.exp(sm - jnp.max(sm, axis=1, keepdims=True)), 0.0)
    l = jnp.sum(p, axis=1, keepdims=True)
    pn = p * (1.0 / jnp.where(l > 0.0, l, 1.0))
    o = jnp.dot(pn.astype(BF), vc_ref[0], preferred_element_type=F32)
    _nsa_write(o, gl_ref[0], 0, o_ref, t)

    nsel = ovl_ref.shape[0]
    jb = lax.broadcasted_iota(jnp.int32, (nsel, t), 0)
    qblk = (qi * t + lax.broadcasted_iota(jnp.int32, (nsel, t), 1)) >> SLC_SHIFT
    forced = (jb == 0) | (jb == qblk) | (jb == qblk - 1)
    ovl = ovl_ref[...]
    for half in range(2):
        ps = pn[(half * 4) * t:(half * 4 + 1) * t]
        for r in range(1, 4):
            ps = ps + pn[(half * 4 + r) * t:(half * 4 + r + 1) * t]
        ps_hi = ps.astype(BF)
        ps_lo = (ps - ps_hi.astype(F32)).astype(BF)
        imp = (lax.dot_general(ovl, ps_hi, _NT, preferred_element_type=F32)
               + lax.dot_general(ovl, ps_lo, _NT, preferred_element_type=F32))
        val = jnp.where(jb <= qblk, jnp.where(forced, FORCED_SCORE, imp), NEG_INF)
        cnt = jnp.zeros((nsel, t), F32)
        for i in range(nsel):
            vi = val[i:i + 1, :]
            ahead = (vi > val) | ((vi == val) & (jb > i))
            cnt = cnt + jnp.where(ahead, 1.0, 0.0)
        chosen = jnp.where((cnt < float(min(SLC_TOPK, nsel))) & (jb <= qblk), 1.0, 0.0)
        pad = jnp.zeros((LANES - nsel, t), F32)
        sel_ref[0, 0, half] = jnp.concatenate([chosen, pad], axis=0).T.astype(BF)


def _nsa_sel_kernel(q_ref, k_ref, v_ref, gl_ref, sel_ref, ind_ref, o_ref, m_ref, l_ref, acc_ref, *, t, tk):
    qi = pl.program_id(2)
    qs = _nsa_stack_q(q_ref[0] * jnp.asarray(SCALE, BF), t)
    bias = [jnp.where(sel_ref[0, 0, half].astype(F32) > 0.5, 0.0, NEG_INF).astype(BF) for half in range(2)]
    q_aug = jnp.concatenate([qs, jnp.concatenate([bias[0]] * 4 + [bias[1]] * 4, axis=0)], axis=1)
    _init_stats(m_ref, l_ref, acc_ref)
    jd = (qi * t) // tk

    def tile(j, causal):
        start = pl.multiple_of(j * tk, tk)
        k_aug = jnp.concatenate([k_ref[0, pl.ds(start, tk), :], ind_ref[pl.ds(start, tk), :]], axis=1)
        v = v_ref[0, pl.ds(start, tk), :]
        s = lax.dot_general(q_aug, k_aug, _NT, preferred_element_type=F32)
        if causal:
            qidx = qi * t + (lax.broadcasted_iota(jnp.int32, s.shape, 0) & (t - 1))
            kidx = j * tk + lax.broadcasted_iota(jnp.int32, s.shape, 1)
            s = jnp.where(kidx <= qidx, s, NEG_INF)
        _softmax_step(s, v, m_ref, l_ref, acc_ref)

    tile(jd, True)

    def body(j, carry):
        tile(j, False)
        return carry

    lax.fori_loop(0, jd, body, 0)
    _nsa_write(_normalised(l_ref, acc_ref), gl_ref[0], 1, o_ref, t)


def _nsa_win_kernel(q_ref, k_ref, v_ref, gl_ref, o_ref, m_ref, l_ref, acc_ref, *, t):
    qi = pl.program_id(2)
    qs = _nsa_stack_q(q_ref[0] * jnp.asarray(SCALE, BF), t)
    _init_stats(m_ref, l_ref, acc_ref)

    def tile(j, kind):
        start = pl.multiple_of(j * t, t)
        k = k_ref[0, pl.ds(start, t), :]
        v = v_ref[0, pl.ds(start, t), :]
        s = lax.dot_general(qs, k, _NT, preferred_element_type=F32)
        if kind != "full":
            row = lax.broadcasted_iota(jnp.int32, s.shape, 0) & (t - 1)
            col = lax.broadcasted_iota(jnp.int32, s.shape, 1)
            s = jnp.where((col <= row) if kind == "causal" else (col > row), s, NEG_INF)
        _softmax_step(s, v, m_ref, l_ref, acc_ref)

    tile(qi, "causal")

    @pl.when(qi >= 1)
    def _():
        tile(qi - 1, "full")

    @pl.when(qi >= 2)
    def _():
        tile(qi - 2, "tail")

    _nsa_write(_normalised(l_ref, acc_ref), gl_ref[0], 2, o_ref, t)


def _nsa_attention(pa, pf, kcmp, vcmp, ovl_t, batch, seq, t, tk):
    pa3 = pa.reshape(batch, seq, pa.shape[1])
    pf3 = pf.reshape(batch, seq, pf.shape[1])
    grid = (batch, 2, seq // t)
    ncmp = kcmp.shape[1]
    sem = _cparams(("parallel", "parallel", "arbitrary"))
    o_shape = jax.ShapeDtypeStruct((batch, seq, D_MODEL), BF)

    def q_specs(rows):
        q_spec = pl.BlockSpec((1, rows, 4 * LANES), lambda b, g, i: (b, i, g))
        gl_spec = pl.BlockSpec((1, rows, LANES), lambda b, g, i: (b, i, 4 + g))
        return q_spec, gl_spec, q_spec

    q_spec, gl_spec, o_spec = q_specs(t)
    ind = _block_indicator(seq, SLC_SHIFT)

    def kv_spec(col_block):
        return pl.BlockSpec((1, seq, LANES), lambda b, g, i: (b, 0, col_block + g))

    o_c, sel = pl.pallas_call(
        functools.partial(_nsa_cmp_kernel, t=t, ncmp=ncmp),
        grid=grid,
        in_specs=[
            q_spec,
            pl.BlockSpec((1, ncmp, LANES), lambda b, g, i: (b, 0, g)),
            pl.BlockSpec((1, ncmp, LANES), lambda b, g, i: (b, 0, g)),
            gl_spec,
            pl.BlockSpec(ovl_t.shape, lambda b, g, i: (0, 0)),
        ],
        out_specs=[o_spec, pl.BlockSpec((1, 1, 2, t, LANES), lambda b, g, i: (b, g, 0, i, 0))],
        out_shape=[o_shape, jax.ShapeDtypeStruct((batch, 2, 2, seq, LANES), BF)],
        compiler_params=sem,
        name="nsa_compressed",
    )(pa3, kcmp, vcmp, pf3, ovl_t)

    o_s = pl.pallas_call(
        functools.partial(_nsa_sel_kernel, t=t, tk=tk),
        grid=grid,
        in_specs=[q_spec, kv_spec(8), kv_spec(12), gl_spec,
                  pl.BlockSpec((1, 1, 2, t, LANES), lambda b, g, i: (b, g, 0, i, 0)),
                  pl.BlockSpec(ind.shape, lambda b, g, i: (0, 0))],
        out_specs=o_spec,
        out_shape=o_shape,
        scratch_shapes=_stats_scratch(8 * t),
        compiler_params=sem,
        name="nsa_selected",
    )(pa3, pa3, pa3, pf3, sel, ind)

    tw = WINDOW // 2
    q_spec, gl_spec, o_spec = q_specs(tw)
    o_w = pl.pallas_call(
        functools.partial(_nsa_win_kernel, t=tw),
        grid=(batch, 2, seq // tw),
        in_specs=[q_spec, kv_spec(10), kv_spec(14), gl_spec],
        out_specs=o_spec,
        out_shape=o_shape,
        scratch_shapes=_stats_scratch(8 * tw),
        compiler_params=sem,
        name="nsa_window",
    )(pa3, pa3, pa3, pf3)
    n = batch * seq
    return [o_c.reshape(n, D_MODEL), o_s.reshape(n, D_MODEL), o_w.reshape(n, D_MODEL)]


def _rope_tables(seq):
    half = ROT_DIM // 2
    inv_freq = ROPE_THETA ** (-jnp.arange(half, dtype=F32) / half)
    ang = jnp.arange(seq).astype(F32)[:, None] * inv_freq[None, :]
    cos, sin = jnp.cos(ang), jnp.sin(ang)
    d = np.arange(LANES) % HEAD_DIM
    idx = d % half
    cos_t = jnp.where(d[None, :] < ROT_DIM, cos[:, idx], 1.0)
    sa_t = jnp.where(d[None, :] < half, -sin[:, idx], 0.0)
    sb_t = jnp.where((d[None, :] >= half) & (d[None, :] < ROT_DIM), sin[:, idx], 0.0)
    return cos_t.astype(F32), sa_t.astype(F32), sb_t.astype(F32)


def _block_indicator(seq, shift):
    blk = np.arange(seq)[:, None] >> shift
    return jnp.asarray((blk == np.arange(LANES)[None, :]).astype(np.float32), BF)


def _nsa_q_perm():
    cols = []
    for gp in range(2):
        for r in range(4):
            for half in range(2):
                head = 8 * gp + 4 * half + r
                cols.extend(range(head * HEAD_DIM, (head + 1) * HEAD_DIM))
    return np.asarray(cols, np.int32)


def _nsa_gate_cols():
    src = -np.ones(2 * LANES, np.int32)
    for gp in range(2):
        for half in range(2):
            for r in range(4):
                for br in range(3):
                    src[gp * LANES + half * 12 + r * 3 + br] = (4 * (2 * gp + half) + r) * 3 + br
    return src


def _ffn_weights(w_in, w_out):
    d = w_in.shape[0]
    nc = D_FF // FF_CHUNK
    wg = w_in[:, :D_FF].reshape(d, nc, FF_CHUNK).transpose(1, 0, 2).astype(BF)
    wu = w_in[:, D_FF:].reshape(d, nc, FF_CHUNK).transpose(1, 0, 2).astype(BF)
    wo = w_out.reshape(nc, FF_CHUNK, d).astype(BF)
    return wg, wu, wo


def kernel(x, norm_g, ffn_w_in, ffn_w_out, diff_w_in, diff_w_out, diff_lambda, diff_subln, moba_w_in, moba_w_out,
           nsa_w_in, nsa_w_out, nsa_cmp_pe, nsa_cmp_w1, nsa_cmp_w2):
    batch, seq, d = x.shape
    n = batch * seq
    tm = 512
    h = x.reshape(n, d)
    tables = _rope_tables(seq)

    for i in range(DEPTH):
        g = norm_g[i]
        wg, wu, wo = _ffn_weights(ffn_w_in[i, 0], ffn_w_out[i, 0])
        h = _ffn(h, g[0:2], wg, wu, wo, tm)

        kind, j = i % 3, i // 3
        if kind == 0:
            lambda_init = 0.8 - 0.6 * math.exp(-0.3 * i)
            plan = [(c * 256, 256, c < 8, 0, c * 256, None) for c in range(12)]
            (qkv,) = _proj(h, g[2:3], diff_w_in[j].astype(BF), tables, plan, [(3 * D_MODEL, BF)], tm, seq)
            attn = _diff_attention(qkv, diff_lambda[j], diff_subln[j].reshape(1, LANES), batch, seq, 512,
                                   lambda_init)
            h = _outproj(h, [attn], diff_w_out[j].astype(BF), g[3:4], tm)
        elif kind == 1:
            plan = [(c * 256, 256, c < 8, 0, c * 256, (c - 4) * 256 if 4 <= c < 8 else None) for c in range(12)]
            qkv, kmean = _proj(h, g[2:3], moba_w_in[j].astype(BF), tables, plan, [(3 * D_MODEL, BF)], tm, seq,
                               kmean_width=D_MODEL)
            attn = _moba_attention(qkv, kmean, batch, seq, 512)
            h = _outproj(h, [attn], moba_w_out[j].astype(BF), g[3:4], tm)
        else:
            w = nsa_w_in[j]
            perm = _nsa_q_perm()
            kvw = NSA_GROUPS * HEAD_DIM
            base = NSA_HEADS * HEAD_DIM
            seg = {name: w[:, base + k * kvw: base + (k + 1) * kvw]
                   for k, name in enumerate(["kc", "vc", "ks", "vs", "kw", "vw"])}
            gsrc = _nsa_gate_cols()
            glog = w[:, base + 6 * kvw:]
            gate_w = jnp.where(gsrc[None, :] >= 0, glog[:, np.maximum(gsrc, 0)], 0.0)
            w_all = jnp.concatenate([w[:, perm], seg["ks"], seg["kw"], seg["kc"], seg["vs"], seg["vw"], seg["vc"],
                                     gate_w], axis=1).astype(BF)
            plan = [(c * 256, 256, True, 0, c * 256, None) for c in range(6)]
            plan.append((1536, 256, True, 1, 0, None))
            plan.append((1792, 256, False, 0, 1536, None))
            plan.append((2048, 256, False, 0, 1792, None))
            plan.append((2304, 256, False, 1, 256, None))
            plan.append((2560, 256, False, 1, 512, None))
            pa, pf = _proj(h, g[2:3], w_all, tables, plan, [(2048, BF), (768, F32)], tm, seq)

            ng = seq // CMP_STRIDE
            kcvc = pf[:, :2 * kvw].reshape(batch, seq, 2, NSA_GROUPS, HEAD_DIM).transpose(2, 0, 3, 1, 4)
            r = kcvc.reshape(2, batch, NSA_GROUPS, ng, CMP_STRIDE * HEAD_DIM)
            pe = nsa_cmp_pe[j].reshape(2, 2, 1, CMP_STRIDE * HEAD_DIM)
            w1 = nsa_cmp_w1[j].reshape(2, 2, CMP_STRIDE * HEAD_DIM, CMP_HIDDEN).astype(BF)
            w2 = nsa_cmp_w2[j].astype(BF)
            cmp_out = _compress(r, pe, w1, w2)
            cmp_tm = cmp_out.transpose(0, 1, 3, 2, 4).reshape(2, batch, ng, kvw).astype(BF)

            nsel = seq // SLC_BLOCK
            cs = np.arange(ng)[:, None] * CMP_STRIDE
            bs = np.arange(nsel)[None, :] * SLC_BLOCK
            ovl = ((cs < bs + SLC_BLOCK) & (cs + CMP_LEN > bs)).astype(np.float32)
            ovl[ng - 1, :] = 0.0
            ovl_t = jnp.asarray(ovl.T, BF)
            parts = _nsa_attention(pa, pf, cmp_tm[0], cmp_tm[1], ovl_t, batch, seq, 128, 512)
            h = _outproj(h, parts, nsa_w_out[j][perm, :].astype(BF), g[3:4], tm)

        wg, wu, wo = _ffn_weights(ffn_w_in[i, 1], ffn_w_out[i, 1])
        h = _ffn(h, g[4:6], wg, wu, wo, tm)
    return h.reshape(batch, seq, d)
```

```python
import functools
import math

import jax
import jax.numpy as jnp
import numpy as np
from jax import lax
from jax.experimental import pallas as pl
from jax.experimental.pallas import tpu as pltpu

D_MODEL = 1024
DEPTH = 4
HEAD_DIM = 64
ROT_DIM = HEAD_DIM // 4
ROPE_THETA = 500000.0
NORM_EPS = 1e-6
NEG_INF = -1e30
REMOVED = -3e38

DIFF_HEADS = 8
DIFF_SUBLN_EPS = 1e-5
MOBA_BLOCK = 256
MOBA_SHIFT = 8
MOBA_TOPK = 3
NSA_HEADS = 16
NSA_GROUPS = 4
CMP_LEN = 32
CMP_STRIDE = 16
CMP_HIDDEN = 256
SLC_BLOCK = 64
SLC_SHIFT = 6
SLC_TOPK = 16
WINDOW = 512
FORCED_SCORE = 1e9
D_FF = 2816

LANES = 128
FF_CHUNK = 256
CHAINS = 2
Q_SCALE = HEAD_DIM ** -0.5 * math.log2(math.e)

BF = jnp.bfloat16
F32 = jnp.float32
VMEM_LIMIT = 56 * 1024 * 1024

_NT = (((1,), (1,)), ((), ()))


def _cparams(sem):
    return pltpu.CompilerParams(dimension_semantics=sem, vmem_limit_bytes=VMEM_LIMIT)


def _rms(x, g, eps):
    return x * lax.rsqrt(jnp.mean(x * x, axis=-1, keepdims=True) + eps) * g


def _resident(shape):
    nd = len(shape)
    return pl.BlockSpec(shape, lambda *_: (0,) * nd, pipeline_mode=pl.Buffered(1))


def _ffn_kernel(h_ref, g_ref, wi_ref, wo_ref, o_ref, acc_ref):
    h = h_ref[...]
    xn = _rms(h, g_ref[0:1, :], NORM_EPS).astype(BF)
    for c in range(D_FF // FF_CHUNK):
        lo = c * FF_CHUNK
        gate = jnp.dot(xn, wi_ref[:, lo:lo + FF_CHUNK], preferred_element_type=F32)
        up = jnp.dot(xn, wi_ref[:, D_FF + lo:D_FF + lo + FF_CHUNK], preferred_element_type=F32)
        act = ((gate * (1.0 / (1.0 + jnp.exp(-gate)))) * up).astype(BF)
        part = jnp.dot(act, wo_ref[lo:lo + FF_CHUNK, :], preferred_element_type=F32)
        if c == 0:
            acc_ref[...] = part
        else:
            acc_ref[...] += part
    o_ref[...] = h + 0.5 * _rms(acc_ref[...], g_ref[1:2, :], NORM_EPS)


def _ffn(h, g2, wi, wo, tm):
    n, d = h.shape
    return pl.pallas_call(
        _ffn_kernel,
        grid=(n // tm,),
        in_specs=[
            pl.BlockSpec((tm, d), lambda i: (i, 0)),
            _resident(g2.shape),
            _resident(wi.shape),
            _resident(wo.shape),
        ],
        out_specs=pl.BlockSpec((tm, d), lambda i: (i, 0)),
        out_shape=jax.ShapeDtypeStruct((n, d), F32),
        scratch_shapes=[pltpu.VMEM((tm, d), F32)],
        compiler_params=_cparams(("parallel",)),
        name="ffn_halfstep",
    )(h, g2, wi, wo)


def _proj_kernel(h_ref, g_ref, w_ref, cos_ref, sa_ref, sb_ref, *out_refs, plan, tm, n_out):
    xn = _rms(h_ref[...], g_ref[...], NORM_EPS).astype(BF)
    cos = cos_ref[...]
    sa = sa_ref[...]
    sb = sb_ref[...]
    for col, width, rope, dest, off, km_off, scale in plan:
        y = jnp.dot(xn, w_ref[:, col:col + width], preferred_element_type=F32)
        for k in range(width // LANES):
            yk = y[:, k * LANES:(k + 1) * LANES]
            if rope:
                yk = yk * cos + pltpu.roll(yk, LANES - ROT_DIM // 2, 1) * sa + pltpu.roll(yk, ROT_DIM // 2, 1) * sb
            if scale is not None:
                yk = yk * scale
            o_ref = out_refs[dest]
            o_ref[:, off + k * LANES:off + (k + 1) * LANES] = yk.astype(o_ref.dtype)
            if km_off is not None:
                km_ref = out_refs[n_out]
                for r in range(tm // MOBA_BLOCK):
                    blk = yk[r * MOBA_BLOCK:(r + 1) * MOBA_BLOCK, :]
                    km_ref[0, r:r + 1, km_off + k * LANES:km_off + (k + 1) * LANES] = jnp.mean(
                        blk, axis=0, keepdims=True)


def _proj(h, g, w, tables, plan, outs, tm, seq, kmean_width=None):
    n, d = h.shape
    per_seq = seq // tm
    out_shape = [jax.ShapeDtypeStruct((n, wd), dt) for wd, dt in outs]
    out_specs = [pl.BlockSpec((tm, wd), lambda i: (i, 0)) for wd, _ in outs]
    if kmean_width is not None:
        nb = tm // MOBA_BLOCK
        out_shape.append(jax.ShapeDtypeStruct((n // tm, nb, kmean_width), F32))
        out_specs.append(pl.BlockSpec((1, nb, kmean_width), lambda i: (i, 0, 0)))
    tab_spec = pl.BlockSpec((tm, LANES), lambda i: (i % per_seq, 0))
    return pl.pallas_call(
        functools.partial(_proj_kernel, plan=tuple(plan), tm=tm, n_out=len(outs)),
        grid=(n // tm,),
        in_specs=[
            pl.BlockSpec((tm, d), lambda i: (i, 0)),
            _resident(g.shape),
            _resident(w.shape),
            tab_spec, tab_spec, tab_spec,
        ],
        out_specs=out_specs,
        out_shape=out_shape,
        compiler_params=_cparams(("parallel",)),
        name="norm_proj",
    )(h, g, w, *tables)


def _outproj_kernel(*refs, n_parts):
    h_ref = refs[0]
    parts = refs[1:1 + n_parts]
    w_ref, g_ref, o_ref = refs[1 + n_parts:]
    a = parts[0][...]
    if n_parts > 1:
        a = a.astype(F32)
        for p in parts[1:]:
            a = a + p[...].astype(F32)
        a = a.astype(BF)
    y = jnp.dot(a, w_ref[...], preferred_element_type=F32)
    o_ref[...] = h_ref[...] + _rms(y, g_ref[...], NORM_EPS)


def _outproj(h, parts, w, g, tm):
    n, d = h.shape
    row = pl.BlockSpec((tm, d), lambda i: (i, 0))
    return pl.pallas_call(
        functools.partial(_outproj_kernel, n_parts=len(parts)),
        grid=(n // tm,),
        in_specs=[row] + [row] * len(parts) + [_resident(w.shape), _resident(g.shape)],
        out_specs=row,
        out_shape=jax.ShapeDtypeStruct((n, d), F32),
        compiler_params=_cparams(("parallel",)),
        name="out_proj",
    )(h, *parts, w, g)


def _softmax_step(s, v, m_ref, l_ref, acc_ref):
    m_prev = m_ref[...]
    m_new = jnp.maximum(m_prev, jnp.max(s, axis=1, keepdims=True))
    alpha = jnp.exp2(m_prev - m_new)
    ps = [jnp.exp2(s[:, c * LANES:(c + 1) * LANES] - m_new) for c in range(s.shape[1] // LANES)]
    lsum = ps[0]
    for p in ps[1:]:
        lsum = lsum + p
    l_ref[...] = alpha * l_ref[...] + lsum
    p = jnp.concatenate([x.astype(BF) for x in ps], axis=1)
    acc_ref[...] = alpha * acc_ref[...] + jnp.dot(p, v, preferred_element_type=F32)
    m_ref[...] = m_new


def _init_stats(m_ref, l_ref, acc_ref):
    m_ref[...] = jnp.full(m_ref.shape, NEG_INF, F32)
    l_ref[...] = jnp.zeros(l_ref.shape, F32)
    acc_ref[...] = jnp.zeros(acc_ref.shape, F32)


def _normalised(l_ref, acc_ref):
    return acc_ref[...] * (1.0 / jnp.sum(l_ref[...], axis=1, keepdims=True))


def _stats_scratch(chains, rows):
    return [pltpu.VMEM((chains, rows, LANES), F32) for _ in range(3)]


def _split_halves(q):
    lane = lax.broadcasted_iota(jnp.int32, q.shape, 1)
    zero = jnp.zeros_like(q)
    return jnp.concatenate([jnp.where(lane < HEAD_DIM, q, zero), jnp.where(lane >= HEAD_DIM, q, zero)], axis=0)


def _lane_block(c):
    return slice(c * LANES, (c + 1) * LANES)


def _diff_kernel(q_ref, k_ref, v_ref, lam_ref, sg_ref, o_ref, m_ref, l_ref, acc_ref, *, t, lambda_init):
    qi = pl.program_id(2)
    qs = [_split_halves(q_ref[0, :, _lane_block(c)]) for c in range(CHAINS)]
    _init_stats(m_ref, l_ref, acc_ref)

    def tile(j, causal):
        start = pl.multiple_of(j * t, t)
        if causal:
            row = lax.broadcasted_iota(jnp.int32, (2 * t, t), 0) & (t - 1)
            col = lax.broadcasted_iota(jnp.int32, (2 * t, t), 1)
        for c in range(CHAINS):
            k = k_ref[0, pl.ds(start, t), _lane_block(c)]
            v = v_ref[0, pl.ds(start, t), _lane_block(c)]
            s = lax.dot_general(qs[c], k, _NT, preferred_element_type=F32)
            if causal:
                s = jnp.where(col <= row, s, NEG_INF)
            _softmax_step(s, v, m_ref.at[c], l_ref.at[c], acc_ref.at[c])

    tile(qi, True)

    def body(j, carry):
        tile(j, False)
        return carry

    lax.fori_loop(0, qi, body, 0)

    lam = lam_ref[...]
    lam_full = (jnp.exp(jnp.sum(lam[0:1] * lam[1:2], axis=1, keepdims=True))
                - jnp.exp(jnp.sum(lam[2:3] * lam[3:4], axis=1, keepdims=True)) + lambda_init)
    for c in range(CHAINS):
        o = _normalised(l_ref.at[c], acc_ref.at[c])
        od = o[:t] - lam_full * o[t:]
        od = _rms(od, sg_ref[...], DIFF_SUBLN_EPS) * (1.0 - lambda_init)
        o_ref[0, :, _lane_block(c)] = od.astype(BF)


def _diff_attention(qkv, lam, subln, batch, seq, t, lambda_init):
    nh = DIFF_HEADS
    ns = nh // CHAINS
    w = CHAINS * LANES
    qkv3 = qkv.reshape(batch, seq, 3 * nh * LANES)
    out = pl.pallas_call(
        functools.partial(_diff_kernel, t=t, lambda_init=lambda_init),
        grid=(batch, ns, seq // t),
        in_specs=[
            pl.BlockSpec((1, t, w), lambda b, h, i: (b, i, h)),
            pl.BlockSpec((1, seq, w), lambda b, h, i: (b, 0, ns + h)),
            pl.BlockSpec((1, seq, w), lambda b, h, i: (b, 0, 2 * ns + h)),
            pl.BlockSpec(lam.shape, lambda b, h, i: (0, 0)),
            pl.BlockSpec(subln.shape, lambda b, h, i: (0, 0)),
        ],
        out_specs=pl.BlockSpec((1, t, w), lambda b, h, i: (b, i, h)),
        out_shape=jax.ShapeDtypeStruct((batch, seq, nh * LANES), BF),
        scratch_shapes=_stats_scratch(CHAINS, 2 * t),
        compiler_params=_cparams(("parallel", "parallel", "arbitrary")),
        name="diff_attention",
    )(qkv3, qkv3, qkv3, lam, subln)
    return out.reshape(batch * seq, nh * LANES)


def _moba_select(q2, km, qi, t, nb):
    km = jnp.concatenate([km, jnp.zeros((LANES - nb, LANES), F32)], axis=0)
    km_hi = km.astype(BF)
    km_lo = (km - km_hi.astype(F32)).astype(BF)
    gate = (lax.dot_general(q2, km_hi, _NT, preferred_element_type=F32)
            + lax.dot_general(q2, km_lo, _NT, preferred_element_type=F32))
    blk = lax.broadcasted_iota(jnp.int32, gate.shape, 1)
    blk_f = blk.astype(F32)
    own = (qi * t + (lax.broadcasted_iota(jnp.int32, gate.shape, 0) & (t - 1))) >> MOBA_SHIFT
    gate = jnp.where(blk < own, gate, NEG_INF)
    sel = jnp.zeros(gate.shape, F32)
    for _ in range(min(MOBA_TOPK, nb)):
        mx = jnp.max(gate, axis=1, keepdims=True)
        idx = jnp.min(jnp.where(gate == mx, blk_f, float(LANES)), axis=1, keepdims=True)
        hit = blk_f == idx
        sel = jnp.where(hit & (mx > 0.5 * NEG_INF), 1.0, sel)
        gate = jnp.where(hit, REMOVED, gate)
    return jnp.where((sel > 0.5) | (blk == own), 0.0, NEG_INF).astype(BF)


def _moba_kernel(q_ref, k_ref, v_ref, km_ref, ind_ref, o_ref, m_ref, l_ref, acc_ref, *, t, nb):
    qi = pl.program_id(2)
    _init_stats(m_ref, l_ref, acc_ref)
    q_aug = []
    for c in range(CHAINS):
        q2 = _split_halves(q_ref[0, :, _lane_block(c)])
        bias = _moba_select(q2, km_ref[0, :, _lane_block(c)], qi, t, nb)
        q_aug.append(jnp.concatenate([q2, bias], axis=1))

    def tile(j, diagonal):
        start = pl.multiple_of(j * t, t)
        ind = ind_ref[pl.ds(start, t), :]
        if diagonal:
            row = lax.broadcasted_iota(jnp.int32, (2 * t, t), 0) & (t - 1)
            col = lax.broadcasted_iota(jnp.int32, (2 * t, t), 1)
            future = ((row >> MOBA_SHIFT) == (col >> MOBA_SHIFT)) & (col > row)
        for c in range(CHAINS):
            k_aug = jnp.concatenate([k_ref[0, pl.ds(start, t), _lane_block(c)], ind], axis=1)
            v = v_ref[0, pl.ds(start, t), _lane_block(c)]
            s = lax.dot_general(q_aug[c], k_aug, _NT, preferred_element_type=F32)
            if diagonal:
                s = jnp.where(future, NEG_INF, s)
            _softmax_step(s, v, m_ref.at[c], l_ref.at[c], acc_ref.at[c])

    tile(qi, True)

    def body(j, carry):
        tile(j, False)
        return carry

    lax.fori_loop(0, qi, body, 0)

    lane = lax.broadcasted_iota(jnp.int32, (t, LANES), 1)
    for c in range(CHAINS):
        o = _normalised(l_ref.at[c], acc_ref.at[c])
        o_ref[0, :, _lane_block(c)] = jnp.where(lane < HEAD_DIM, o[:t], o[t:]).astype(BF)


def _moba_attention(qkv, kmean, batch, seq, t):
    nb = seq // MOBA_BLOCK
    ns = D_MODEL // LANES // CHAINS
    w = CHAINS * LANES
    qkv3 = qkv.reshape(batch, seq, 3 * D_MODEL)
    km3 = kmean.reshape(batch, nb, D_MODEL)
    ind = _block_indicator(seq, MOBA_SHIFT)
    out = pl.pallas_call(
        functools.partial(_moba_kernel, t=t, nb=nb),
        grid=(batch, ns, seq // t),
        in_specs=[
            pl.BlockSpec((1, t, w), lambda b, p, i: (b, i, p)),
            pl.BlockSpec((1, seq, w), lambda b, p, i: (b, 0, ns + p)),
            pl.BlockSpec((1, seq, w), lambda b, p, i: (b, 0, 2 * ns + p)),
            pl.BlockSpec((1, nb, w), lambda b, p, i: (b, 0, p)),
            pl.BlockSpec(ind.shape, lambda b, p, i: (0, 0)),
        ],
        out_specs=pl.BlockSpec((1, t, w), lambda b, p, i: (b, i, p)),
        out_shape=jax.ShapeDtypeStruct((batch, seq, D_MODEL), BF),
        scratch_shapes=_stats_scratch(CHAINS, 2 * t),
        compiler_params=_cparams(("parallel", "parallel", "arbitrary")),
        name="moba_attention",
    )(qkv3, qkv3, qkv3, km3, ind)
    return out.reshape(batch * seq, D_MODEL)


def _compress_kernel(r_ref, pe_ref, w1_ref, w2_ref, o_ref):
    r = r_ref[0, 0, 0]
    ng = r.shape[0]
    a0 = (r + pe_ref[0, 0]).astype(BF)
    a1 = (r + pe_ref[0, 1]).astype(BF)
    y0 = jnp.dot(a0, w1_ref[0, 0], preferred_element_type=F32)
    y1 = jnp.dot(a1, w1_ref[0, 1], preferred_element_type=F32)
    pre = y0 + pltpu.roll(y1, ng - 1, 0)
    hid = pre * (1.0 / (1.0 + jnp.exp(-pre)))
    o_ref[0, 0, 0] = jnp.dot(hid.astype(BF), w2_ref[0], preferred_element_type=F32)


def _compress(r, pe, w1, w2):
    two, batch, ngrp, ng, wd = r.shape
    return pl.pallas_call(
        _compress_kernel,
        grid=(two, batch, ngrp),
        in_specs=[
            pl.BlockSpec((1, 1, 1, ng, wd), lambda a, b, g: (a, b, g, 0, 0)),
            pl.BlockSpec((1, 2, 1, wd), lambda a, b, g: (a, 0, 0, 0)),
            pl.BlockSpec((1, 2, wd, CMP_HIDDEN), lambda a, b, g: (a, 0, 0, 0)),
            pl.BlockSpec((1, CMP_HIDDEN, HEAD_DIM), lambda a, b, g: (a, 0, 0)),
        ],
        out_specs=pl.BlockSpec((1, 1, 1, ng, HEAD_DIM), lambda a, b, g: (a, b, g, 0, 0)),
        out_shape=jax.ShapeDtypeStruct((two, batch, ngrp, ng, HEAD_DIM), F32),
        compiler_params=_cparams(("parallel", "parallel", "parallel")),
        name="nsa_compress",
    )(r, pe, w1, w2)


def _nsa_stack_q(q, t):
    lane = lax.broadcasted_iota(jnp.int32, (t, LANES), 1)
    zero = jnp.zeros((t, LANES), q.dtype)
    parts = []
    for half in range(2):
        keep = (lane < HEAD_DIM) if half == 0 else (lane >= HEAD_DIM)
        for r in range(4):
            parts.append(jnp.where(keep, q[:, r * LANES:(r + 1) * LANES], zero))
    return jnp.concatenate(parts, axis=0)


def _nsa_write(o, gl, branch, o_ref, t, gp=0):
    lane = lax.broadcasted_iota(jnp.int32, (t, LANES), 1)
    for r in range(4):
        c0 = r * 3 + branch
        c1 = 12 + r * 3 + branch
        g0 = 1.0 / (1.0 + jnp.exp(-gl[:, c0:c0 + 1]))
        g1 = 1.0 / (1.0 + jnp.exp(-gl[:, c1:c1 + 1]))
        blk = jnp.where(lane < HEAD_DIM, g0 * o[r * t:(r + 1) * t], g1 * o[(4 + r) * t:(5 + r) * t])
        o_ref[0, :, _lane_block(4 * gp + r)] = blk.astype(BF)


def _nsa_cmp_kernel(q_ref, kc_ref, vc_ref, gl_ref, ovl_ref, o_ref, sel_ref, *, t, ncmp):
    qi = pl.program_id(2)
    qs = _nsa_stack_q(q_ref[0], t)
    s = lax.dot_general(qs, kc_ref[0], _NT, preferred_element_type=F32)
    n_idx = lax.broadcasted_iota(jnp.int32, s.shape, 1)
    qidx = qi * t + (lax.broadcasted_iota(jnp.int32, s.shape, 0) & (t - 1))
    mask = (n_idx * CMP_STRIDE + (CMP_LEN - 1)) <= qidx
    sm = jnp.where(mask, s, NEG_INF)
    p = jnp.where(mask, jnp.exp2(sm - jnp.max(sm, axis=1, keepdims=True)), 0.0)
    l = jnp.sum(p, axis=1, keepdims=True)
    pn = p * (1.0 / jnp.where(l > 0.0, l, 1.0))
    o = jnp.dot(pn.astype(BF), vc_ref[0], preferred_element_type=F32)
    _nsa_write(o, gl_ref[0], 0, o_ref, t)

    nsel = ovl_ref.shape[0]
    jb = lax.broadcasted_iota(jnp.int32, (nsel, t), 0)
    qblk = (qi * t + lax.broadcasted_iota(jnp.int32, (nsel, t), 1)) >> SLC_SHIFT
    forced = (jb == 0) | (jb == qblk) | (jb == qblk - 1)
    ovl = ovl_ref[...]
    for half in range(2):
        ps = pn[(half * 4) * t:(half * 4 + 1) * t]
        for r in range(1, 4):
            ps = ps + pn[(half * 4 + r) * t:(half * 4 + r + 1) * t]
        ps_hi = ps.astype(BF)
        ps_lo = (ps - ps_hi.astype(F32)).astype(BF)
        imp = (lax.dot_general(ovl, ps_hi, _NT, preferred_element_type=F32)
               + lax.dot_general(ovl, ps_lo, _NT, preferred_element_type=F32))
        val = jnp.where(jb <= qblk, jnp.where(forced, FORCED_SCORE, imp), NEG_INF)
        cnt = jnp.zeros((nsel, t), F32)
        for i in range(nsel):
            vi = val[i:i + 1, :]
            ahead = (vi > val) | ((vi == val) & (jb > i))
            cnt = cnt + jnp.where(ahead, 1.0, 0.0)
        chosen = jnp.where((cnt < float(min(SLC_TOPK, nsel))) & (jb <= qblk), 1.0, 0.0)
        pad = jnp.zeros((LANES - nsel, t), F32)
        sel_ref[0, 0, half] = jnp.concatenate([chosen, pad], axis=0).T.astype(BF)


def _nsa_sel_kernel(q_ref, k_ref, v_ref, gl_ref, sel_ref, ind_ref, o_ref, m_ref, l_ref, acc_ref, *, t, tk):
    qi = pl.program_id(1)
    q_aug = []
    for gp in range(2):
        qs = _nsa_stack_q(q_ref[0, :, gp * 4 * LANES:(gp + 1) * 4 * LANES], t)
        bias = [jnp.where(sel_ref[0, gp, half].astype(F32) > 0.5, 0.0, NEG_INF).astype(BF) for half in range(2)]
        q_aug.append(jnp.concatenate([qs, jnp.concatenate([bias[0]] * 4 + [bias[1]] * 4, axis=0)], axis=1))
    _init_stats(m_ref, l_ref, acc_ref)
    jd = (qi * t) // tk

    def tile(j, causal):
        start = pl.multiple_of(j * tk, tk)
        ind = ind_ref[pl.ds(start, tk), :]
        if causal:
            qidx = qi * t + (lax.broadcasted_iota(jnp.int32, (8 * t, tk), 0) & (t - 1))
            kidx = j * tk + lax.broadcasted_iota(jnp.int32, (8 * t, tk), 1)
        for gp in range(2):
            k_aug = jnp.concatenate([k_ref[0, pl.ds(start, tk), _lane_block(gp)], ind], axis=1)
            v = v_ref[0, pl.ds(start, tk), _lane_block(gp)]
            s = lax.dot_general(q_aug[gp], k_aug, _NT, preferred_element_type=F32)
            if causal:
                s = jnp.where(kidx <= qidx, s, NEG_INF)
            _softmax_step(s, v, m_ref.at[gp], l_ref.at[gp], acc_ref.at[gp])

    tile(jd, True)

    def body(j, carry):
        tile(j, False)
        return carry

    lax.fori_loop(0, jd, body, 0)
    for gp in range(2):
        _nsa_write(_normalised(l_ref.at[gp], acc_ref.at[gp]), gl_ref[0, :, _lane_block(gp)], 1, o_ref, t, gp)


def _nsa_win_kernel(q_ref, k_ref, v_ref, gl_ref, o_ref, m_ref, l_ref, acc_ref, *, t):
    qi = pl.program_id(1)
    qs = [_nsa_stack_q(q_ref[0, :, gp * 4 * LANES:(gp + 1) * 4 * LANES], t) for gp in range(2)]
    _init_stats(m_ref, l_ref, acc_ref)

    def tile(j, kind):
        start = pl.multiple_of(j * t, t)
        if kind != "full":
            row = lax.broadcasted_iota(jnp.int32, (8 * t, t), 0) & (t - 1)
            col = lax.broadcasted_iota(jnp.int32, (8 * t, t), 1)
            keep = (col <= row) if kind == "causal" else (col > row)
        for gp in range(2):
            k = k_ref[0, pl.ds(start, t), _lane_block(gp)]
            v = v_ref[0, pl.ds(start, t), _lane_block(gp)]
            s = lax.dot_general(qs[gp], k, _NT, preferred_element_type=F32)
            if kind != "full":
                s = jnp.where(keep, s, NEG_INF)
            _softmax_step(s, v, m_ref.at[gp], l_ref.at[gp], acc_ref.at[gp])

    tile(qi, "causal")

    @pl.when(qi >= 1)
    def _():
        tile(qi - 1, "full")

    @pl.when(qi >= 2)
    def _():
        tile(qi - 2, "tail")

    for gp in range(2):
        _nsa_write(_normalised(l_ref.at[gp], acc_ref.at[gp]), gl_ref[0, :, _lane_block(gp)], 2, o_ref, t, gp)


def _nsa_attention(pa, pf, kcmp, vcmp, ovl_t, batch, seq, t, tk):
    pa3 = pa.reshape(batch, seq, pa.shape[1])
    pf3 = pf.reshape(batch, seq, pf.shape[1])
    grid = (batch, 2, seq // t)
    ncmp = kcmp.shape[1]
    sem = _cparams(("parallel", "parallel", "arbitrary"))
    o_shape = jax.ShapeDtypeStruct((batch, seq, D_MODEL), BF)

    def q_specs(rows):
        q_spec = pl.BlockSpec((1, rows, 4 * LANES), lambda b, g, i: (b, i, g))
        gl_spec = pl.BlockSpec((1, rows, LANES), lambda b, g, i: (b, i, 4 + g))
        return q_spec, gl_spec, q_spec

    q_spec, gl_spec, o_spec = q_specs(t)
    ind = _block_indicator(seq, SLC_SHIFT)

    def kv_spec(col_block):
        return pl.BlockSpec((1, seq, LANES), lambda b, g, i: (b, 0, col_block + g))

    o_c, sel = pl.pallas_call(
        functools.partial(_nsa_cmp_kernel, t=t, ncmp=ncmp),
        grid=grid,
        in_specs=[
            q_spec,
            pl.BlockSpec((1, ncmp, LANES), lambda b, g, i: (b, 0, g)),
            pl.BlockSpec((1, ncmp, LANES), lambda b, g, i: (b, 0, g)),
            gl_spec,
            pl.BlockSpec(ovl_t.shape, lambda b, g, i: (0, 0)),
        ],
        out_specs=[o_spec, pl.BlockSpec((1, 1, 2, t, LANES), lambda b, g, i: (b, g, 0, i, 0))],
        out_shape=[o_shape, jax.ShapeDtypeStruct((batch, 2, 2, seq, LANES), BF)],
        compiler_params=sem,
        name="nsa_compressed",
    )(pa3, kcmp, vcmp, pf3, ovl_t)

    sem2 = _cparams(("parallel", "arbitrary"))
    kvw = 2 * LANES

    def row_spec(rows):
        return pl.BlockSpec((1, rows, D_MODEL), lambda b, i: (b, i, 0))

    def kv2_spec(col_block):
        return pl.BlockSpec((1, seq, kvw), lambda b, i: (b, 0, col_block))

    def gl2_spec(rows):
        return pl.BlockSpec((1, rows, kvw), lambda b, i: (b, i, 2))

    o_s = pl.pallas_call(
        functools.partial(_nsa_sel_kernel, t=t, tk=tk),
        grid=(batch, seq // t),
        in_specs=[row_spec(t), kv2_spec(4), kv2_spec(6), gl2_spec(t),
                  pl.BlockSpec((1, 2, 2, t, LANES), lambda b, i: (b, 0, 0, i, 0)),
                  pl.BlockSpec(ind.shape, lambda b, i: (0, 0))],
        out_specs=row_spec(t),
        out_shape=o_shape,
        scratch_shapes=_stats_scratch(2, 8 * t),
        compiler_params=sem2,
        name="nsa_selected",
    )(pa3, pa3, pa3, pf3, sel, ind)

    tw = WINDOW // 2
    o_w = pl.pallas_call(
        functools.partial(_nsa_win_kernel, t=tw),
        grid=(batch, seq // tw),
        in_specs=[row_spec(tw), kv2_spec(5), kv2_spec(7), gl2_spec(tw)],
        out_specs=row_spec(tw),
        out_shape=o_shape,
        scratch_shapes=_stats_scratch(2, 8 * tw),
        compiler_params=sem2,
        name="nsa_window",
    )(pa3, pa3, pa3, pf3)
    n = batch * seq
    return [o_c.reshape(n, D_MODEL), o_s.reshape(n, D_MODEL), o_w.reshape(n, D_MODEL)]


def _rope_tables(seq):
    half = ROT_DIM // 2
    inv_freq = ROPE_THETA ** (-jnp.arange(half, dtype=F32) / half)
    ang = jnp.arange(seq).astype(F32)[:, None] * inv_freq[None, :]
    cos, sin = jnp.cos(ang), jnp.sin(ang)
    d = np.arange(LANES) % HEAD_DIM
    idx = d % half
    cos_t = jnp.where(d[None, :] < ROT_DIM, cos[:, idx], 1.0)
    sa_t = jnp.where(d[None, :] < half, -sin[:, idx], 0.0)
    sb_t = jnp.where((d[None, :] >= half) & (d[None, :] < ROT_DIM), sin[:, idx], 0.0)
    return cos_t.astype(F32), sa_t.astype(F32), sb_t.astype(F32)


def _block_indicator(seq, shift):
    blk = np.arange(seq)[:, None] >> shift
    return jnp.asarray((blk == np.arange(LANES)[None, :]).astype(np.float32), BF)


def _nsa_q_perm():
    cols = []
    for gp in range(2):
        for r in range(4):
            for half in range(2):
                head = 8 * gp + 4 * half + r
                cols.extend(range(head * HEAD_DIM, (head + 1) * HEAD_DIM))
    return np.asarray(cols, np.int32)


def _nsa_gate_cols():
    src = -np.ones(2 * LANES, np.int32)
    for gp in range(2):
        for half in range(2):
            for r in range(4):
                for br in range(3):
                    src[gp * LANES + half * 12 + r * 3 + br] = (4 * (2 * gp + half) + r) * 3 + br
    return src


def kernel(x, norm_g, ffn_w_in, ffn_w_out, diff_w_in, diff_w_out, diff_lambda, diff_subln, moba_w_in, moba_w_out,
           nsa_w_in, nsa_w_out, nsa_cmp_pe, nsa_cmp_w1, nsa_cmp_w2):
    batch, seq, d = x.shape
    n = batch * seq
    tm = 512
    h = x.reshape(n, d)
    tables = _rope_tables(seq)

    for i in range(DEPTH):
        g = norm_g[i]
        h = _ffn(h, g[0:2], ffn_w_in[i, 0].astype(BF), ffn_w_out[i, 0].astype(BF), tm)

        kind, j = i % 3, i // 3
        if kind == 0:
            lambda_init = 0.8 - 0.6 * math.exp(-0.3 * i)
            plan = [(c * 256, 256, c < 8, 0, c * 256, None, Q_SCALE if c < 4 else None) for c in range(12)]
            (qkv,) = _proj(h, g[2:3], diff_w_in[j].astype(BF), tables, plan, [(3 * D_MODEL, BF)], tm, seq)
            attn = _diff_attention(qkv, diff_lambda[j], diff_subln[j].reshape(1, LANES), batch, seq, 512,
                                   lambda_init)
            h = _outproj(h, [attn], diff_w_out[j].astype(BF), g[3:4], tm)
        elif kind == 1:
            plan = [(c * 256, 256, c < 8, 0, c * 256, (c - 4) * 256 if 4 <= c < 8 else None,
                     Q_SCALE if c < 4 else None) for c in range(12)]
            qkv, kmean = _proj(h, g[2:3], moba_w_in[j].astype(BF), tables, plan, [(3 * D_MODEL, BF)], tm, seq,
                               kmean_width=D_MODEL)
            attn = _moba_attention(qkv, kmean, batch, seq, 512)
            h = _outproj(h, [attn], moba_w_out[j].astype(BF), g[3:4], tm)
        else:
            w = nsa_w_in[j]
            perm = _nsa_q_perm()
            kvw = NSA_GROUPS * HEAD_DIM
            base = NSA_HEADS * HEAD_DIM
            seg = {name: w[:, base + k * kvw: base + (k + 1) * kvw]
                   for k, name in enumerate(["kc", "vc", "ks", "vs", "kw", "vw"])}
            gsrc = _nsa_gate_cols()
            glog = w[:, base + 6 * kvw:]
            gate_w = jnp.where(gsrc[None, :] >= 0, glog[:, np.maximum(gsrc, 0)], 0.0)
            w_all = jnp.concatenate([w[:, perm], seg["ks"], seg["kw"], seg["kc"], seg["vs"], seg["vw"], seg["vc"],
                                     gate_w], axis=1).astype(BF)
            plan = [(c * 256, 256, True, 0, c * 256, None, Q_SCALE if c < 4 else None)
                    for c in range(6)]
            plan.append((1536, 256, True, 1, 0, None, None))
            plan.append((1792, 256, False, 0, 1536, None, None))
            plan.append((2048, 256, False, 0, 1792, None, None))
            plan.append((2304, 256, False, 1, 256, None, None))
            plan.append((2560, 256, False, 1, 512, None, None))
            pa, pf = _proj(h, g[2:3], w_all, tables, plan, [(2048, BF), (768, F32)], tm, seq)

            ng = seq // CMP_STRIDE
            kcvc = pf[:, :2 * kvw].reshape(batch, seq, 2, NSA_GROUPS, HEAD_DIM).transpose(2, 0, 3, 1, 4)
            r = kcvc.reshape(2, batch, NSA_GROUPS, ng, CMP_STRIDE * HEAD_DIM)
            pe = nsa_cmp_pe[j].reshape(2, 2, 1, CMP_STRIDE * HEAD_DIM)
            w1 = nsa_cmp_w1[j].reshape(2, 2, CMP_STRIDE * HEAD_DIM, CMP_HIDDEN).astype(BF)
            w2 = nsa_cmp_w2[j].astype(BF)
            cmp_out = _compress(r, pe, w1, w2)
            cmp_tm = cmp_out.transpose(0, 1, 3, 2, 4).reshape(2, batch, ng, kvw).astype(BF)

            nsel = seq // SLC_BLOCK
            cs = np.arange(ng)[:, None] * CMP_STRIDE
            bs = np.arange(nsel)[None, :] * SLC_BLOCK
            ovl = ((cs < bs + SLC_BLOCK) & (cs + CMP_LEN > bs)).astype(np.float32)
            ovl[ng - 1, :] = 0.0
            ovl_t = jnp.asarray(ovl.T, BF)
            parts = _nsa_attention(pa, pf, cmp_tm[0], cmp_tm[1], ovl_t, batch, seq, 128, 512)
            h = _outproj(h, parts, nsa_w_out[j][perm, :].astype(BF), g[3:4], tm)

        h = _ffn(h, g[4:6], ffn_w_in[i, 1].astype(BF), ffn_w_out[i, 1].astype(BF), tm)
    return h.reshape(batch, seq, d)
```

```python
import functools
import math

import jax
import jax.numpy as jnp
import numpy as np
from jax import lax
from jax.experimental import pallas as pl
from jax.experimental.pallas import tpu as pltpu

D_MODEL = 1024
DEPTH = 4
HEAD_DIM = 64
ROT_DIM = HEAD_DIM // 4
ROPE_THETA = 500000.0
NORM_EPS = 1e-6
NEG_INF = -1e30
REMOVED = -3e38

DIFF_HEADS = 8
DIFF_SUBLN_EPS = 1e-5
MOBA_BLOCK = 256
MOBA_SHIFT = 8
MOBA_TOPK = 3
NSA_HEADS = 16
NSA_GROUPS = 4
CMP_LEN = 32
CMP_STRIDE = 16
CMP_HIDDEN = 256
SLC_BLOCK = 64
SLC_SHIFT = 6
SLC_TOPK = 16
WINDOW = 512
FORCED_SCORE = 1e9
D_FF = 2816

LANES = 128
FF_CHUNK = 256
CHAINS = 2
Q_SCALE = HEAD_DIM ** -0.5 * math.log2(math.e)

BF = jnp.bfloat16
F32 = jnp.float32
VMEM_LIMIT = 56 * 1024 * 1024

_NT = (((1,), (1,)), ((), ()))


def _cparams(sem):
    return pltpu.CompilerParams(dimension_semantics=sem, vmem_limit_bytes=VMEM_LIMIT)


def _rms(x, g, eps):
    return x * lax.rsqrt(jnp.mean(x * x, axis=-1, keepdims=True) + eps) * g


def _resident(shape):
    nd = len(shape)
    return pl.BlockSpec(shape, lambda *_: (0,) * nd, pipeline_mode=pl.Buffered(1))


def _ffn_kernel(*refs, n_parts):
    h_ref = refs[0]
    parts = refs[1:1 + n_parts]
    if n_parts:
        wp_ref, gp_ref = refs[1 + n_parts:3 + n_parts]
        refs = refs[3 + n_parts:]
    else:
        refs = refs[1:]
    g_ref, wi_ref, wo_ref, o_ref, acc_ref = refs
    h = h_ref[...]
    if n_parts:
        a = parts[0][...]
        if n_parts > 1:
            a = a.astype(F32)
            for p in parts[1:]:
                a = a + p[...].astype(F32)
            a = a.astype(BF)
        h = h + _rms(jnp.dot(a, wp_ref[...], preferred_element_type=F32), gp_ref[...], NORM_EPS)
    xn = _rms(h, g_ref[0:1, :], NORM_EPS).astype(BF)
    for c in range(D_FF // FF_CHUNK):
        lo = c * FF_CHUNK
        gate = jnp.dot(xn, wi_ref[:, lo:lo + FF_CHUNK], preferred_element_type=F32)
        up = jnp.dot(xn, wi_ref[:, D_FF + lo:D_FF + lo + FF_CHUNK], preferred_element_type=F32)
        act = ((gate * (1.0 / (1.0 + jnp.exp(-gate)))) * up).astype(BF)
        part = jnp.dot(act, wo_ref[lo:lo + FF_CHUNK, :], preferred_element_type=F32)
        if c == 0:
            acc_ref[...] = part
        else:
            acc_ref[...] += part
    o_ref[...] = h + 0.5 * _rms(acc_ref[...], g_ref[1:2, :], NORM_EPS)


def _ffn(h, g2, wi, wo, tm, parts=(), wp=None, gp=None):
    n, d = h.shape
    row = pl.BlockSpec((tm, d), lambda i: (i, 0))
    proj_args = [wp, gp] if parts else []
    return pl.pallas_call(
        functools.partial(_ffn_kernel, n_parts=len(parts)),
        grid=(n // tm,),
        in_specs=([row] * (1 + len(parts)) + [_resident(a.shape) for a in proj_args]
                  + [_resident(g2.shape), _resident(wi.shape), _resident(wo.shape)]),
        out_specs=row,
        out_shape=jax.ShapeDtypeStruct((n, d), F32),
        scratch_shapes=[pltpu.VMEM((tm, d), F32)],
        compiler_params=_cparams(("parallel",)),
        name="ffn_halfstep",
    )(h, *parts, *proj_args, g2, wi, wo)


def _proj_kernel(h_ref, g_ref, w_ref, cos_ref, sa_ref, sb_ref, *out_refs, plan, tm, n_out):
    xn = _rms(h_ref[...], g_ref[...], NORM_EPS).astype(BF)
    cos = cos_ref[...]
    sa = sa_ref[...]
    sb = sb_ref[...]
    for col, width, rope, dest, off, km_off, scale in plan:
        y = jnp.dot(xn, w_ref[:, col:col + width], preferred_element_type=F32)
        for k in range(width // LANES):
            yk = y[:, k * LANES:(k + 1) * LANES]
            if rope:
                yk = yk * cos + pltpu.roll(yk, LANES - ROT_DIM // 2, 1) * sa + pltpu.roll(yk, ROT_DIM // 2, 1) * sb
            if scale is not None:
                yk = yk * scale
            o_ref = out_refs[dest]
            o_ref[:, off + k * LANES:off + (k + 1) * LANES] = yk.astype(o_ref.dtype)
            if km_off is not None:
                km_ref = out_refs[n_out]
                for r in range(tm // MOBA_BLOCK):
                    blk = yk[r * MOBA_BLOCK:(r + 1) * MOBA_BLOCK, :]
                    km_ref[0, r:r + 1, km_off + k * LANES:km_off + (k + 1) * LANES] = jnp.mean(
                        blk, axis=0, keepdims=True)


def _proj(h, g, w, tables, plan, outs, tm, seq, kmean_width=None):
    n, d = h.shape
    per_seq = seq // tm
    out_shape = [jax.ShapeDtypeStruct((n, wd), dt) for wd, dt in outs]
    out_specs = [pl.BlockSpec((tm, wd), lambda i: (i, 0)) for wd, _ in outs]
    if kmean_width is not None:
        nb = tm // MOBA_BLOCK
        out_shape.append(jax.ShapeDtypeStruct((n // tm, nb, kmean_width), F32))
        out_specs.append(pl.BlockSpec((1, nb, kmean_width), lambda i: (i, 0, 0)))
    tab_spec = pl.BlockSpec((tm, LANES), lambda i: (i % per_seq, 0))
    return pl.pallas_call(
        functools.partial(_proj_kernel, plan=tuple(plan), tm=tm, n_out=len(outs)),
        grid=(n // tm,),
        in_specs=[
            pl.BlockSpec((tm, d), lambda i: (i, 0)),
            _resident(g.shape),
            _resident(w.shape),
            tab_spec, tab_spec, tab_spec,
        ],
        out_specs=out_specs,
        out_shape=out_shape,
        compiler_params=_cparams(("parallel",)),
        name="norm_proj",
    )(h, g, w, *tables)


def _softmax_step(s, v, m_ref, l_ref, acc_ref):
    m_prev = m_ref[...]
    m_new = jnp.maximum(m_prev, jnp.max(s, axis=1, keepdims=True))
    alpha = jnp.exp2(m_prev - m_new)
    ps = [jnp.exp2(s[:, c * LANES:(c + 1) * LANES] - m_new) for c in range(s.shape[1] // LANES)]
    lsum = ps[0]
    for p in ps[1:]:
        lsum = lsum + p
    l_ref[...] = alpha * l_ref[...] + lsum
    p = jnp.concatenate([x.astype(BF) for x in ps], axis=1)
    acc_ref[...] = alpha * acc_ref[...] + jnp.dot(p, v, preferred_element_type=F32)
    m_ref[...] = m_new


def _init_stats(m_ref, l_ref, acc_ref):
    m_ref[...] = jnp.full(m_ref.shape, NEG_INF, F32)
    l_ref[...] = jnp.zeros(l_ref.shape, F32)
    acc_ref[...] = jnp.zeros(acc_ref.shape, F32)


def _normalised(l_ref, acc_ref):
    return acc_ref[...] * (1.0 / jnp.sum(l_ref[...], axis=1, keepdims=True))


def _stats_scratch(chains, rows):
    return [pltpu.VMEM((chains, rows, LANES), F32) for _ in range(3)]


def _split_halves(q):
    lane = lax.broadcasted_iota(jnp.int32, q.shape, 1)
    zero = jnp.zeros_like(q)
    return jnp.concatenate([jnp.where(lane < HEAD_DIM, q, zero), jnp.where(lane >= HEAD_DIM, q, zero)], axis=0)


def _lane_block(c):
    return slice(c * LANES, (c + 1) * LANES)


def _diff_kernel(q_ref, k_ref, v_ref, lam_ref, sg_ref, o_ref, m_ref, l_ref, acc_ref, *, t, lambda_init):
    qi = pl.program_id(2)
    qs = [_split_halves(q_ref[0, :, _lane_block(c)]) for c in range(CHAINS)]
    _init_stats(m_ref, l_ref, acc_ref)

    def tile(j, causal):
        start = pl.multiple_of(j * t, t)
        ss = [lax.dot_general(qs[c], k_ref[0, pl.ds(start, t), _lane_block(c)], _NT, preferred_element_type=F32)
              for c in range(CHAINS)]
        if causal:
            row = lax.broadcasted_iota(jnp.int32, (2 * t, t), 0) & (t - 1)
            col = lax.broadcasted_iota(jnp.int32, (2 * t, t), 1)
            ss = [jnp.where(col <= row, s, NEG_INF) for s in ss]
        for c in range(CHAINS):
            v = v_ref[0, pl.ds(start, t), _lane_block(c)]
            _softmax_step(ss[c], v, m_ref.at[c], l_ref.at[c], acc_ref.at[c])

    tile(qi, True)

    def body(j, carry):
        tile(j, False)
        return carry

    lax.fori_loop(0, qi, body, 0)

    lam = lam_ref[...]
    lam_full = (jnp.exp(jnp.sum(lam[0:1] * lam[1:2], axis=1, keepdims=True))
                - jnp.exp(jnp.sum(lam[2:3] * lam[3:4], axis=1, keepdims=True)) + lambda_init)
    for c in range(CHAINS):
        o = _normalised(l_ref.at[c], acc_ref.at[c])
        od = o[:t] - lam_full * o[t:]
        od = _rms(od, sg_ref[...], DIFF_SUBLN_EPS) * (1.0 - lambda_init)
        o_ref[0, :, _lane_block(c)] = od.astype(BF)


def _diff_attention(qkv, lam, subln, batch, seq, t, lambda_init):
    nh = DIFF_HEADS
    ns = nh // CHAINS
    w = CHAINS * LANES
    qkv3 = qkv.reshape(batch, seq, 3 * nh * LANES)
    out = pl.pallas_call(
        functools.partial(_diff_kernel, t=t, lambda_init=lambda_init),
        grid=(batch, ns, seq // t),
        in_specs=[
            pl.BlockSpec((1, t, w), lambda b, h, i: (b, i, h)),
            pl.BlockSpec((1, seq, w), lambda b, h, i: (b, 0, ns + h)),
            pl.BlockSpec((1, seq, w), lambda b, h, i: (b, 0, 2 * ns + h)),
            pl.BlockSpec(lam.shape, lambda b, h, i: (0, 0)),
            pl.BlockSpec(subln.shape, lambda b, h, i: (0, 0)),
        ],
        out_specs=pl.BlockSpec((1, t, w), lambda b, h, i: (b, i, h)),
        out_shape=jax.ShapeDtypeStruct((batch, seq, nh * LANES), BF),
        scratch_shapes=_stats_scratch(CHAINS, 2 * t),
        compiler_params=_cparams(("parallel", "parallel", "arbitrary")),
        name="diff_attention",
    )(qkv3, qkv3, qkv3, lam, subln)
    return out.reshape(batch * seq, nh * LANES)


def _moba_select(q2, km, qi, t, nb):
    km = jnp.concatenate([km, jnp.zeros((LANES - nb, LANES), F32)], axis=0)
    km_hi = km.astype(BF)
    km_lo = (km - km_hi.astype(F32)).astype(BF)
    gate = (lax.dot_general(q2, km_hi, _NT, preferred_element_type=F32)
            + lax.dot_general(q2, km_lo, _NT, preferred_element_type=F32))
    blk = lax.broadcasted_iota(jnp.int32, gate.shape, 1)
    blk_f = blk.astype(F32)
    own = (qi * t + (lax.broadcasted_iota(jnp.int32, gate.shape, 0) & (t - 1))) >> MOBA_SHIFT
    gate = jnp.where(blk < own, gate, NEG_INF)
    sel = jnp.zeros(gate.shape, F32)
    for _ in range(min(MOBA_TOPK, nb)):
        mx = jnp.max(gate, axis=1, keepdims=True)
        idx = jnp.min(jnp.where(gate == mx, blk_f, float(LANES)), axis=1, keepdims=True)
        hit = blk_f == idx
        sel = jnp.where(hit & (mx > 0.5 * NEG_INF), 1.0, sel)
        gate = jnp.where(hit, REMOVED, gate)
    return jnp.where((sel > 0.5) | (blk == own), 0.0, NEG_INF).astype(BF)


def _moba_kernel(q_ref, k_ref, v_ref, km_ref, ind_ref, o_ref, m_ref, l_ref, acc_ref, *, t, nb):
    qi = pl.program_id(2)
    _init_stats(m_ref, l_ref, acc_ref)
    q_aug = []
    for c in range(CHAINS):
        q2 = _split_halves(q_ref[0, :, _lane_block(c)])
        bias = _moba_select(q2, km_ref[0, :, _lane_block(c)], qi, t, nb)
        q_aug.append(jnp.concatenate([q2, bias], axis=1))

    def tile(j, diagonal):
        start = pl.multiple_of(j * t, t)
        ind = ind_ref[pl.ds(start, t), :]
        if diagonal:
            row = lax.broadcasted_iota(jnp.int32, (2 * t, t), 0) & (t - 1)
            col = lax.broadcasted_iota(jnp.int32, (2 * t, t), 1)
            future = ((row >> MOBA_SHIFT) == (col >> MOBA_SHIFT)) & (col > row)
        for c in range(CHAINS):
            k_aug = jnp.concatenate([k_ref[0, pl.ds(start, t), _lane_block(c)], ind], axis=1)
            v = v_ref[0, pl.ds(start, t), _lane_block(c)]
            s = lax.dot_general(q_aug[c], k_aug, _NT, preferred_element_type=F32)
            if diagonal:
                s = jnp.where(future, NEG_INF, s)
            _softmax_step(s, v, m_ref.at[c], l_ref.at[c], acc_ref.at[c])

    tile(qi, True)

    def body(j, carry):
        tile(j, False)
        return carry

    lax.fori_loop(0, qi, body, 0)

    lane = lax.broadcasted_iota(jnp.int32, (t, LANES), 1)
    for c in range(CHAINS):
        o = _normalised(l_ref.at[c], acc_ref.at[c])
        o_ref[0, :, _lane_block(c)] = jnp.where(lane < HEAD_DIM, o[:t], o[t:]).astype(BF)


def _moba_attention(qkv, kmean, batch, seq, t):
    nb = seq // MOBA_BLOCK
    ns = D_MODEL // LANES // CHAINS
    w = CHAINS * LANES
    qkv3 = qkv.reshape(batch, seq, 3 * D_MODEL)
    km3 = kmean.reshape(batch, nb, D_MODEL)
    ind = _block_indicator(seq, MOBA_SHIFT)
    out = pl.pallas_call(
        functools.partial(_moba_kernel, t=t, nb=nb),
        grid=(batch, ns, seq // t),
        in_specs=[
            pl.BlockSpec((1, t, w), lambda b, p, i: (b, i, p)),
            pl.BlockSpec((1, seq, w), lambda b, p, i: (b, 0, ns + p)),
            pl.BlockSpec((1, seq, w), lambda b, p, i: (b, 0, 2 * ns + p)),
            pl.BlockSpec((1, nb, w), lambda b, p, i: (b, 0, p)),
            pl.BlockSpec(ind.shape, lambda b, p, i: (0, 0)),
        ],
        out_specs=pl.BlockSpec((1, t, w), lambda b, p, i: (b, i, p)),
        out_shape=jax.ShapeDtypeStruct((batch, seq, D_MODEL), BF),
        scratch_shapes=_stats_scratch(CHAINS, 2 * t),
        compiler_params=_cparams(("parallel", "parallel", "arbitrary")),
        name="moba_attention",
    )(qkv3, qkv3, qkv3, km3, ind)
    return out.reshape(batch * seq, D_MODEL)


def _compress_kernel(r_ref, pe_ref, w1_ref, w2_ref, o_ref):
    r = r_ref[0, 0, 0]
    ng = r.shape[0]
    a0 = (r + pe_ref[0, 0]).astype(BF)
    a1 = (r + pe_ref[0, 1]).astype(BF)
    y0 = jnp.dot(a0, w1_ref[0, 0], preferred_element_type=F32)
    y1 = jnp.dot(a1, w1_ref[0, 1], preferred_element_type=F32)
    pre = y0 + pltpu.roll(y1, ng - 1, 0)
    hid = pre * (1.0 / (1.0 + jnp.exp(-pre)))
    o_ref[0, 0, 0] = jnp.dot(hid.astype(BF), w2_ref[0], preferred_element_type=F32)


def _compress(r, pe, w1, w2):
    two, batch, ngrp, ng, wd = r.shape
    return pl.pallas_call(
        _compress_kernel,
        grid=(two, batch, ngrp),
        in_specs=[
            pl.BlockSpec((1, 1, 1, ng, wd), lambda a, b, g: (a, b, g, 0, 0)),
            pl.BlockSpec((1, 2, 1, wd), lambda a, b, g: (a, 0, 0, 0)),
            pl.BlockSpec((1, 2, wd, CMP_HIDDEN), lambda a, b, g: (a, 0, 0, 0)),
            pl.BlockSpec((1, CMP_HIDDEN, HEAD_DIM), lambda a, b, g: (a, 0, 0)),
        ],
        out_specs=pl.BlockSpec((1, 1, 1, ng, HEAD_DIM), lambda a, b, g: (a, b, g, 0, 0)),
        out_shape=jax.ShapeDtypeStruct((two, batch, ngrp, ng, HEAD_DIM), F32),
        compiler_params=_cparams(("parallel", "parallel", "parallel")),
        name="nsa_compress",
    )(r, pe, w1, w2)


def _nsa_stack_q(q, t):
    lane = lax.broadcasted_iota(jnp.int32, (t, LANES), 1)
    zero = jnp.zeros((t, LANES), q.dtype)
    parts = []
    for half in range(2):
        keep = (lane < HEAD_DIM) if half == 0 else (lane >= HEAD_DIM)
        for r in range(4):
            parts.append(jnp.where(keep, q[:, r * LANES:(r + 1) * LANES], zero))
    return jnp.concatenate(parts, axis=0)


def _nsa_write(o, gl, branch, o_ref, t, gp=0):
    lane = lax.broadcasted_iota(jnp.int32, (t, LANES), 1)
    for r in range(4):
        c0 = r * 3 + branch
        c1 = 12 + r * 3 + branch
        g0 = 1.0 / (1.0 + jnp.exp(-gl[:, c0:c0 + 1]))
        g1 = 1.0 / (1.0 + jnp.exp(-gl[:, c1:c1 + 1]))
        blk = jnp.where(lane < HEAD_DIM, g0 * o[r * t:(r + 1) * t], g1 * o[(4 + r) * t:(5 + r) * t])
        o_ref[0, :, _lane_block(4 * gp + r)] = blk.astype(BF)


def _nsa_cmp_kernel(q_ref, kc_ref, vc_ref, gl_ref, ovl_ref, o_ref, sel_ref, *, t, ncmp):
    qi = pl.program_id(2)
    qs = _nsa_stack_q(q_ref[0], t)
    s = lax.dot_general(qs, kc_ref[0], _NT, preferred_element_type=F32)
    n_idx = lax.broadcasted_iota(jnp.int32, s.shape, 1)
    qidx = qi * t + (lax.broadcasted_iota(jnp.int32, s.shape, 0) & (t - 1))
    mask = (n_idx * CMP_STRIDE + (CMP_LEN - 1)) <= qidx
    sm = jnp.where(mask, s, NEG_INF)
    p = jnp.where(mask, jnp.exp2(sm - jnp.max(sm, axis=1, keepdims=True)), 0.0)
    l = jnp.sum(p, axis=1, keepdims=True)
    pn = p * (1.0 / jnp.where(l > 0.0, l, 1.0))
    o = jnp.dot(pn.astype(BF), vc_ref[0], preferred_element_type=F32)
    _nsa_write(o, gl_ref[0], 0, o_ref, t)

    nsel = ovl_ref.shape[0]
    jb = lax.broadcasted_iota(jnp.int32, (nsel, t), 0)
    qblk =(qi * t + lax.broadcasted_iota(jnp.int32, (nsel, t), 1)) >> SLC_SHIFT
    forced = (jb == 0) | (jb == qblk) | (jb == qblk - 1)
    ovl = ovl_ref[...]
    for half in range(2):
        ps = pn[(half * 4) * t:(half * 4 + 1) * t]
        for r in range(1, 4):
            ps = ps + pn[(half * 4 + r) * t:(half * 4 + r + 1) * t]
        ps_hi = ps.astype(BF)
        ps_lo = (ps - ps_hi.astype(F32)).astype(BF)
        imp = (lax.dot_general(ovl, ps_hi, _NT, preferred_element_type=F32)
               + lax.dot_general(ovl, ps_lo, _NT, preferred_element_type=F32))
        val = jnp.where(jb <= qblk, jnp.where(forced, FORCED_SCORE, imp), NEG_INF)
        slabs = [val[g * 8:(g + 1) * 8] for g in range(nsel // 8)]
        row8 = lax.broadcasted_iota(jnp.int32, (8, t), 0)
        cnt = [jnp.zeros((8, t), F32) for _ in slabs]
        for i in range(nsel):
            vi = val[i:i + 1, :]
            for g, sl in enumerate(slabs):
                if g > i // 8:
                    ahead = vi >= sl
                elif g < i // 8:
                    ahead = vi > sl
                else:
                    ahead = (vi > sl) | ((vi == sl) & (row8 > (i % 8)))
                cnt[g] = jnp.where(ahead, cnt[g] + 1.0, cnt[g])
        rank = jnp.concatenate(cnt, axis=0)
        chosen = jnp.where((rank < float(min(SLC_TOPK, nsel))) & (jb <= qblk), 1.0, 0.0)
        pad = jnp.zeros((LANES - nsel, t), F32)
        sel_ref[0, 0, half] = jnp.concatenate([chosen, pad], axis=0).T.astype(BF)


def _nsa_sel_kernel(q_ref, k_ref, v_ref, gl_ref, sel_ref, ind_ref, o_ref, m_ref, l_ref, acc_ref, *, t, tk):
    qi = pl.program_id(1)
    q_aug = []
    for gp in range(2):
        qs = _nsa_stack_q(q_ref[0, :, gp * 4 * LANES:(gp + 1) * 4 * LANES], t)
        bias = [jnp.where(sel_ref[0, gp, half].astype(F32) > 0.5, 0.0, NEG_INF).astype(BF) for half in range(2)]
        q_aug.append(jnp.concatenate([qs, jnp.concatenate([bias[0]] * 4 + [bias[1]] * 4, axis=0)], axis=1))
    _init_stats(m_ref, l_ref, acc_ref)
    jd = (qi * t) // tk

    def tile(j, causal):
        start = pl.multiple_of(j * tk, tk)
        ind = ind_ref[pl.ds(start, tk), :]
        if causal:
            qidx = qi * t + (lax.broadcasted_iota(jnp.int32, (8 * t, tk), 0) & (t - 1))
            kidx = j * tk + lax.broadcasted_iota(jnp.int32, (8 * t, tk), 1)
        ss = []
        for gp in range(2):
            k_aug = jnp.concatenate([k_ref[0, pl.ds(start, tk), _lane_block(gp)], ind], axis=1)
            s = lax.dot_general(q_aug[gp], k_aug, _NT, preferred_element_type=F32)
            ss.append(jnp.where(kidx <= qidx, s, NEG_INF) if causal else s)
        for gp in range(2):
            v = v_ref[0, pl.ds(start, tk), _lane_block(gp)]
            _softmax_step(ss[gp], v, m_ref.at[gp], l_ref.at[gp], acc_ref.at[gp])

    tile(jd, True)

    def body(j, carry):
        tile(j, False)
        return carry

    lax.fori_loop(0, jd, body, 0)
    for gp in range(2):
        _nsa_write(_normalised(l_ref.at[gp], acc_ref.at[gp]), gl_ref[0, :, _lane_block(gp)], 1, o_ref, t, gp)


def _nsa_win_kernel(q_ref, k_ref, v_ref, gl_ref, o_ref, m_ref, l_ref, acc_ref, *, t):
    qi = pl.program_id(1)
    qs = [_nsa_stack_q(q_ref[0, :, gp * 4 * LANES:(gp + 1) * 4 * LANES], t) for gp in range(2)]
    _init_stats(m_ref, l_ref, acc_ref)

    def tile(j, kind):
        start = pl.multiple_of(j * t, t)
        if kind != "full":
            row = lax.broadcasted_iota(jnp.int32, (8 * t, t), 0) & (t - 1)
            col = lax.broadcasted_iota(jnp.int32, (8 * t, t), 1)
            keep = (col <= row) if kind == "causal" else (col > row)
        for gp in range(2):
            k = k_ref[0, pl.ds(start, t), _lane_block(gp)]
            v = v_ref[0, pl.ds(start, t), _lane_block(gp)]
            s = lax.dot_general(qs[gp], k, _NT, preferred_element_type=F32)
            if kind != "full":
                s = jnp.where(keep, s, NEG_INF)
            _softmax_step(s, v, m_ref.at[gp], l_ref.at[gp], acc_ref.at[gp])

    tile(qi, "causal")

    @pl.when(qi >= 1)
    def _():
        tile(qi - 1, "full")

    @pl.when(qi >= 2)
    def _():
        tile(qi - 2, "tail")

    for gp in range(2):
        _nsa_write(_normalised(l_ref.at[gp], acc_ref.at[gp]), gl_ref[0, :, _lane_block(gp)], 2, o_ref, t, gp)


def _nsa_attention(pa, pf, kcmp, vcmp, ovl_t, batch, seq, t, tk):
    pa3 = pa.reshape(batch, seq, pa.shape[1])
    pf3 = pf.reshape(batch, seq, pf.shape[1])
    grid = (batch, 2, seq // t)
    ncmp = kcmp.shape[1]
    sem = _cparams(("parallel", "parallel", "arbitrary"))
    o_shape = jax.ShapeDtypeStruct((batch, seq, D_MODEL), BF)

    def q_specs(rows):
        q_spec = pl.BlockSpec((1, rows, 4 * LANES), lambda b, g, i: (b, i, g))
        gl_spec = pl.BlockSpec((1, rows, LANES), lambda b, g, i: (b, i, 4 + g))
        return q_spec, gl_spec, q_spec

    q_spec, gl_spec, o_spec = q_specs(t)
    ind = _block_indicator(seq, SLC_SHIFT)

    def kv_spec(col_block):
        return pl.BlockSpec((1, seq, LANES), lambda b, g, i: (b, 0, col_block + g))

    o_c, sel = pl.pallas_call(
        functools.partial(_nsa_cmp_kernel, t=t, ncmp=ncmp),
        grid=grid,
        in_specs=[
            q_spec,
            pl.BlockSpec((1, ncmp, LANES), lambda b, g, i: (b, 0, g)),
            pl.BlockSpec((1, ncmp, LANES), lambda b, g, i: (b, 0, g)),
            gl_spec,
            pl.BlockSpec(ovl_t.shape, lambda b, g, i: (0, 0)),
        ],
        out_specs=[o_spec, pl.BlockSpec((1, 1, 2, t, LANES), lambda b, g, i: (b, g, 0, i, 0))],
        out_shape=[o_shape, jax.ShapeDtypeStruct((batch, 2, 2, seq, LANES), BF)],
        compiler_params=sem,
        name="nsa_compressed",
    )(pa3, kcmp, vcmp, pf3, ovl_t)

    sem2 = _cparams(("parallel", "arbitrary"))
    kvw = 2 * LANES

    def row_spec(rows):
        return pl.BlockSpec((1, rows, D_MODEL), lambda b, i: (b, i, 0))

    def kv2_spec(col_block):
        return pl.BlockSpec((1, seq, kvw), lambda b, i: (b, 0, col_block))

    def gl2_spec(rows):
        return pl.BlockSpec((1, rows, kvw), lambda b, i: (b, i, 2))

    o_s = pl.pallas_call(
        functools.partial(_nsa_sel_kernel, t=t, tk=tk),
        grid=(batch, seq // t),
        in_specs=[row_spec(t), kv2_spec(4), kv2_spec(6), gl2_spec(t),
                  pl.BlockSpec((1, 2, 2, t, LANES), lambda b, i: (b, 0, 0, i, 0)),
                  pl.BlockSpec(ind.shape, lambda b, i: (0, 0))],
        out_specs=row_spec(t),
        out_shape=o_shape,
        scratch_shapes=_stats_scratch(2, 8 * t),
        compiler_params=sem2,
        name="nsa_selected",
    )(pa3, pa3, pa3, pf3, sel, ind)

    tw = WINDOW // 2
    o_w = pl.pallas_call(
        functools.partial(_nsa_win_kernel, t=tw),
        grid=(batch, seq // tw),
        in_specs=[row_spec(tw), kv2_spec(5), kv2_spec(7), gl2_spec(tw)],
        out_specs=row_spec(tw),
        out_shape=o_shape,
        scratch_shapes=_stats_scratch(2, 8 * tw),
        compiler_params=sem2,
        name="nsa_window",
    )(pa3, pa3, pa3, pf3)
    n = batch * seq
    return [o_c.reshape(n, D_MODEL), o_s.reshape(n, D_MODEL), o_w.reshape(n, D_MODEL)]


def _rope_tables(seq):
    half = ROT_DIM // 2
    inv_freq = ROPE_THETA ** (-jnp.arange(half, dtype=F32) / half)
    ang = jnp.arange(seq).astype(F32)[:, None] * inv_freq[None, :]
    cos, sin = jnp.cos(ang), jnp.sin(ang)
    d = np.arange(LANES) % HEAD_DIM
    idx = d % half
    cos_t = jnp.where(d[None, :] < ROT_DIM, cos[:, idx], 1.0)
    sa_t = jnp.where(d[None, :] < half, -sin[:, idx], 0.0)
    sb_t = jnp.where((d[None, :] >= half) & (d[None, :] < ROT_DIM), sin[:, idx], 0.0)
    return cos_t.astype(F32), sa_t.astype(F32), sb_t.astype(F32)


def _block_indicator(seq, shift):
    blk = np.arange(seq)[:, None] >> shift
    return jnp.asarray((blk == np.arange(LANES)[None, :]).astype(np.float32), BF)


def _nsa_q_perm():
    cols = []
    for gp in range(2):
        for r in range(4):
            for half in range(2):
                head = 8 * gp + 4 * half + r
                cols.extend(range(head * HEAD_DIM, (head + 1) * HEAD_DIM))
    return np.asarray(cols, np.int32)


def _nsa_gate_cols():
    src = -np.ones(2 * LANES, np.int32)
    for gp in range(2):
        for half in range(2):
            for r in range(4):
                for br in range(3):
                    src[gp * LANES + half * 12 + r * 3 + br] = (4 * (2 * gp + half) + r) * 3 + br
    return src


def kernel(x, norm_g, ffn_w_in, ffn_w_out, diff_w_in, diff_w_out, diff_lambda, diff_subln, moba_w_in, moba_w_out,
           nsa_w_in, nsa_w_out, nsa_cmp_pe, nsa_cmp_w1, nsa_cmp_w2):
    batch, seq, d = x.shape
    n = batch * seq
    tm = 512
    h = x.reshape(n, d)
    tables = _rope_tables(seq)

    for i in range(DEPTH):
        g = norm_g[i]
        h = _ffn(h, g[0:2], ffn_w_in[i, 0].astype(BF), ffn_w_out[i, 0].astype(BF), tm)

        kind, j = i % 3, i // 3
        if kind == 0:
            lambda_init = 0.8 - 0.6 * math.exp(-0.3 * i)
            plan = [(c * 256, 256, c < 8, 0, c * 256, None, Q_SCALE if c < 4 else None) for c in range(12)]
            (qkv,) = _proj(h, g[2:3], diff_w_in[j].astype(BF), tables, plan, [(3 * D_MODEL, BF)], tm, seq)
            attn = _diff_attention(qkv, diff_lambda[j], diff_subln[j].reshape(1, LANES), batch, seq, 512,
                                   lambda_init)
            parts, w_out = [attn], diff_w_out[j]
        elif kind == 1:
            plan = [(c * 256, 256, c < 8, 0, c * 256, (c - 4) * 256 if 4 <= c < 8 else None,
                     Q_SCALE if c < 4 else None) for c in range(12)]
            qkv, kmean = _proj(h, g[2:3], moba_w_in[j].astype(BF), tables, plan, [(3 * D_MODEL, BF)], tm, seq,
                               kmean_width=D_MODEL)
            attn = _moba_attention(qkv, kmean, batch, seq, 512)
            parts, w_out = [attn], moba_w_out[j]
        else:
            w = nsa_w_in[j]
            perm = _nsa_q_perm()
            kvw = NSA_GROUPS * HEAD_DIM
            base = NSA_HEADS * HEAD_DIM
            seg = {name: w[:, base + k * kvw: base + (k + 1) * kvw]
                   for k, name in enumerate(["kc", "vc", "ks", "vs", "kw", "vw"])}
            gsrc = _nsa_gate_cols()
            glog = w[:, base + 6 * kvw:]
            gate_w = jnp.where(gsrc[None, :] >= 0, glog[:, np.maximum(gsrc, 0)], 0.0)
            w_all = jnp.concatenate([w[:, perm], seg["ks"], seg["kw"], seg["kc"], seg["vs"], seg["vw"], seg["vc"],
                                     gate_w], axis=1).astype(BF)
            plan = [(c * 256, 256, True, 0, c * 256, None, Q_SCALE if c < 4 else None)
                    for c in range(6)]
            plan.append((1536, 256, True, 1, 0, None, None))
            plan.append((1792, 256, False, 0, 1536, None, None))
            plan.append((2048, 256, False, 0, 1792, None, None))
            plan.append((2304, 256, False, 1, 256, None, None))
            plan.append((2560, 256, False, 1, 512, None, None))
            pa, pf = _proj(h, g[2:3], w_all, tables, plan, [(2048, BF), (768, F32)], tm, seq)

            ng = seq // CMP_STRIDE
            kcvc = pf[:, :2 * kvw].reshape(batch, seq, 2, NSA_GROUPS, HEAD_DIM).transpose(2, 0, 3, 1, 4)
            r = kcvc.reshape(2, batch, NSA_GROUPS, ng, CMP_STRIDE * HEAD_DIM)
            pe = nsa_cmp_pe[j].reshape(2, 2, 1, CMP_STRIDE * HEAD_DIM)
            w1 = nsa_cmp_w1[j].reshape(2, 2, CMP_STRIDE * HEAD_DIM, CMP_HIDDEN).astype(BF)
            w2 = nsa_cmp_w2[j].astype(BF)
            cmp_out = _compress(r, pe, w1, w2)
            cmp_tm = cmp_out.transpose(0, 1, 3, 2, 4).reshape(2, batch, ng, kvw).astype(BF)

            nsel = seq // SLC_BLOCK
            cs = np.arange(ng)[:, None] * CMP_STRIDE
            bs = np.arange(nsel)[None, :] * SLC_BLOCK
            ovl = ((cs < bs + SLC_BLOCK) & (cs + CMP_LEN > bs)).astype(np.float32)
            ovl[ng - 1, :] = 0.0
            ovl_t = jnp.asarray(ovl.T, BF)
            parts = _nsa_attention(pa, pf, cmp_tm[0], cmp_tm[1], ovl_t, batch, seq, 128, 512)
            w_out = nsa_w_out[j][perm, :]

        h = _ffn(h, g[4:6], ffn_w_in[i, 1].astype(BF), ffn_w_out[i, 1].astype(BF), tm,
                 parts=parts, wp=w_out.astype(BF), gp=g[3:4])
    return h.reshape(batch, seq, d)
```

```python
import functools
import math

import jax
import jax.numpy as jnp
import numpy as np
from jax import lax
from jax.experimental import pallas as pl
from jax.experimental.pallas import tpu as pltpu

D_MODEL = 1024
DEPTH = 4
HEAD_DIM = 64
ROT_DIM = HEAD_DIM // 4
ROPE_THETA = 500000.0
NORM_EPS = 1e-6
NEG_INF = -1e30
REMOVED = -3e38

DIFF_HEADS = 8
DIFF_SUBLN_EPS = 1e-5
MOBA_BLOCK = 256
MOBA_SHIFT = 8
MOBA_TOPK = 3
NSA_HEADS = 16
NSA_GROUPS = 4
CMP_LEN = 32
CMP_STRIDE = 16
CMP_HIDDEN = 256
SLC_BLOCK = 64
SLC_SHIFT = 6
SLC_TOPK = 16
WINDOW = 512
FORCED_SCORE = 1e9
D_FF = 2816

LANES = 128
FF_CHUNK = 256
CHAINS = 4
Q_SCALE = HEAD_DIM ** -0.5 * math.log2(math.e)

BF = jnp.bfloat16
F32 = jnp.float32
VMEM_LIMIT = 56 * 1024 * 1024

_NT = (((1,), (1,)), ((), ()))


def _cparams(sem):
    return pltpu.CompilerParams(dimension_semantics=sem, vmem_limit_bytes=VMEM_LIMIT)


def _rms(x, g, eps):
    return x * lax.rsqrt(jnp.mean(x * x, axis=-1, keepdims=True) + eps) * g


def _resident(shape):
    nd = len(shape)
    return pl.BlockSpec(shape, lambda *_: (0,) * nd, pipeline_mode=pl.Buffered(1))


def _ffn_kernel(*refs, n_parts):
    h_ref = refs[0]
    parts = refs[1:1 + n_parts]
    if n_parts:
        wp_ref, gp_ref = refs[1 + n_parts:3 + n_parts]
        refs = refs[3 + n_parts:]
    else:
        refs = refs[1:]
    g_ref, wi_ref, wo_ref, o_ref, acc_ref = refs
    h = h_ref[...]
    if n_parts:
        a = parts[0][...]
        if n_parts > 1:
            a = a.astype(F32)
            for p in parts[1:]:
                a = a + p[...].astype(F32)
            a = a.astype(BF)
        h = h + _rms(jnp.dot(a, wp_ref[...], preferred_element_type=F32), gp_ref[...], NORM_EPS)
    xn = _rms(h, g_ref[0:1, :], NORM_EPS).astype(BF)
    for c in range(D_FF // FF_CHUNK):
        lo = c * FF_CHUNK
        gate = jnp.dot(xn, wi_ref[:, lo:lo + FF_CHUNK], preferred_element_type=F32)
        up = jnp.dot(xn, wi_ref[:, D_FF + lo:D_FF + lo + FF_CHUNK], preferred_element_type=F32)
        act = ((gate * (1.0 / (1.0 + jnp.exp(-gate)))) * up).astype(BF)
        part = jnp.dot(act, wo_ref[lo:lo + FF_CHUNK, :], preferred_element_type=F32)
        if c == 0:
            acc_ref[...] = part
        else:
            acc_ref[...] += part
    o_ref[...] = h + 0.5 * _rms(acc_ref[...], g_ref[1:2, :], NORM_EPS)


def _ffn(h, g2, wi, wo, tm, parts=(), wp=None, gp=None):
    n, d = h.shape
    row = pl.BlockSpec((tm, d), lambda i: (i, 0))
    proj_args = [wp, gp] if parts else []
    return pl.pallas_call(
        functools.partial(_ffn_kernel, n_parts=len(parts)),
        grid=(n // tm,),
        in_specs=([row] * (1 + len(parts)) + [_resident(a.shape) for a in proj_args]
                  + [_resident(g2.shape), _resident(wi.shape), _resident(wo.shape)]),
        out_specs=row,
        out_shape=jax.ShapeDtypeStruct((n, d), F32),
        scratch_shapes=[pltpu.VMEM((tm, d), F32)],
        compiler_params=_cparams(("parallel",)),
        name="ffn_halfstep",
    )(h, *parts, *proj_args, g2, wi, wo)


def _proj_kernel(h_ref, g_ref, w_ref, cos_ref, sa_ref, sb_ref, *out_refs, plan, tm, n_out):
    xn = _rms(h_ref[...], g_ref[...], NORM_EPS).astype(BF)
    cos = cos_ref[...]
    sa = sa_ref[...]
    sb = sb_ref[...]
    for col, width, rope, dest, off, km_off, scale in plan:
        y = jnp.dot(xn, w_ref[:, col:col + width], preferred_element_type=F32)
        for k in range(width // LANES):
            yk = y[:, k * LANES:(k + 1) * LANES]
            if rope:
                yk = yk * cos + pltpu.roll(yk, LANES - ROT_DIM // 2, 1) * sa + pltpu.roll(yk, ROT_DIM // 2, 1) * sb
            if scale is not None:
                yk = yk * scale
            o_ref = out_refs[dest]
            o_ref[:, off + k * LANES:off + (k + 1) * LANES] = yk.astype(o_ref.dtype)
            if km_off is not None:
                km_ref = out_refs[n_out]
                for r in range(tm // MOBA_BLOCK):
                    blk = yk[r * MOBA_BLOCK:(r + 1) * MOBA_BLOCK, :]
                    km_ref[0, r:r + 1, km_off + k * LANES:km_off + (k + 1) * LANES] = jnp.mean(
                        blk, axis=0, keepdims=True)


def _proj(h, g, w, tables, plan, outs, tm, seq, kmean_width=None):
    n, d = h.shape
    per_seq = seq // tm
    out_shape = [jax.ShapeDtypeStruct((n, wd), dt) for wd, dt in outs]
    out_specs = [pl.BlockSpec((tm, wd), lambda i: (i, 0)) for wd, _ in outs]
    if kmean_width is not None:
        nb = tm // MOBA_BLOCK
        out_shape.append(jax.ShapeDtypeStruct((n // tm, nb, kmean_width), F32))
        out_specs.append(pl.BlockSpec((1, nb, kmean_width), lambda i: (i, 0, 0)))
    tab_spec = pl.BlockSpec((tm, LANES), lambda i: (i % per_seq, 0))
    return pl.pallas_call(
        functools.partial(_proj_kernel, plan=tuple(plan), tm=tm, n_out=len(outs)),
        grid=(n // tm,),
        in_specs=[
            pl.BlockSpec((tm, d), lambda i: (i, 0)),
            _resident(g.shape),
            _resident(w.shape),
            tab_spec, tab_spec, tab_spec,
        ],
        out_specs=out_specs,
        out_shape=out_shape,
        compiler_params=_cparams(("parallel",)),
        name="norm_proj",
    )(h, g, w, *tables)


def _softmax_step(s, v, m_ref, l_ref, acc_ref):
    m_prev = m_ref[...]
    m_new = jnp.maximum(m_prev, jnp.max(s, axis=1, keepdims=True))
    alpha = jnp.exp2(m_prev - m_new)
    ps = [jnp.exp2(s[:, c * LANES:(c + 1) * LANES] - m_new) for c in range(s.shape[1] // LANES)]
    lsum = ps[0]
    for p in ps[1:]:
        lsum = lsum + p
    l_ref[...] = alpha * l_ref[...] + lsum
    p = jnp.concatenate([x.astype(BF) for x in ps], axis=1)
    acc_ref[...] = alpha * acc_ref[...] + jnp.dot(p, v, preferred_element_type=F32)
    m_ref[...] = m_new


def _init_stats(m_ref, l_ref, acc_ref):
    m_ref[...] = jnp.full(m_ref.shape, NEG_INF, F32)
    l_ref[...] = jnp.zeros(l_ref.shape, F32)
    acc_ref[...] = jnp.zeros(acc_ref.shape, F32)


def _normalised(l_ref, acc_ref):
    return acc_ref[...] * (1.0 / jnp.sum(l_ref[...], axis=1, keepdims=True))


def _stats_scratch(chains, rows):
    return [pltpu.VMEM((chains, rows, LANES), F32) for _ in range(3)]


def _split_halves(q):
    lane = lax.broadcasted_iota(jnp.int32, q.shape, 1)
    zero = jnp.zeros_like(q)
    return jnp.concatenate([jnp.where(lane < HEAD_DIM, q, zero), jnp.where(lane >= HEAD_DIM, q, zero)], axis=0)


def _lane_block(c):
    return slice(c * LANES, (c + 1) * LANES)


def _diff_kernel(q_ref, k_ref, v_ref, lam_ref, sg_ref, o_ref, m_ref, l_ref, acc_ref, *, t, lambda_init):
    qi = pl.program_id(2)
    qs = [_split_halves(q_ref[0, :, _lane_block(c)]) for c in range(CHAINS)]
    _init_stats(m_ref, l_ref, acc_ref)

    def tile(j, causal):
        start = pl.multiple_of(j * t, t)
        ss = [lax.dot_general(qs[c], k_ref[0, pl.ds(start, t), _lane_block(c)], _NT, preferred_element_type=F32)
              for c in range(CHAINS)]
        if causal:
            row = lax.broadcasted_iota(jnp.int32, (2 * t, t), 0) & (t - 1)
            col = lax.broadcasted_iota(jnp.int32, (2 * t, t), 1)
            ss = [jnp.where(col <= row, s, NEG_INF) for s in ss]
        for c in range(CHAINS):
            v = v_ref[0, pl.ds(start, t), _lane_block(c)]
            _softmax_step(ss[c], v, m_ref.at[c], l_ref.at[c], acc_ref.at[c])

    tile(qi, True)

    def body(j, carry):
        tile(j, False)
        return carry

    lax.fori_loop(0, qi, body, 0)

    lam = lam_ref[...]
    lam_full = (jnp.exp(jnp.sum(lam[0:1] * lam[1:2], axis=1, keepdims=True))
                - jnp.exp(jnp.sum(lam[2:3] * lam[3:4], axis=1, keepdims=True)) + lambda_init)
    for c in range(CHAINS):
        o = _normalised(l_ref.at[c], acc_ref.at[c])
        od = o[:t] - lam_full * o[t:]
        od = _rms(od, sg_ref[...], DIFF_SUBLN_EPS) * (1.0 - lambda_init)
        o_ref[0, :, _lane_block(c)] = od.astype(BF)


def _diff_attention(qkv, lam, subln, batch, seq, t, lambda_init):
    nh = DIFF_HEADS
    ns = nh // CHAINS
    w = CHAINS * LANES
    qkv3 = qkv.reshape(batch, seq, 3 * nh * LANES)
    out = pl.pallas_call(
        functools.partial(_diff_kernel, t=t, lambda_init=lambda_init),
        grid=(batch, ns, seq // t),
        in_specs=[
            pl.BlockSpec((1, t, w), lambda b, h, i: (b, i, h)),
            pl.BlockSpec((1, seq, w), lambda b, h, i: (b, 0, ns + h)),
            pl.BlockSpec((1, seq, w), lambda b, h, i: (b, 0, 2 * ns + h)),
            pl.BlockSpec(lam.shape, lambda b, h, i: (0, 0)),
            pl.BlockSpec(subln.shape, lambda b, h, i: (0, 0)),
        ],
        out_specs=pl.BlockSpec((1, t, w), lambda b, h, i: (b, i, h)),
        out_shape=jax.ShapeDtypeStruct((batch, seq, nh * LANES), BF),
        scratch_shapes=_stats_scratch(CHAINS, 2 * t),
        compiler_params=_cparams(("parallel", "parallel", "arbitrary")),
        name="diff_attention",
    )(qkv3, qkv3, qkv3, lam, subln)
    return out.reshape(batch * seq, nh * LANES)


def _moba_select(q2, km, qi, t, nb):
    km = jnp.concatenate([km, jnp.zeros((LANES - nb, LANES), F32)], axis=0)
    km_hi = km.astype(BF)
    km_lo = (km - km_hi.astype(F32)).astype(BF)
    gate = (lax.dot_general(q2, km_hi, _NT, preferred_element_type=F32)
            + lax.dot_general(q2, km_lo, _NT, preferred_element_type=F32))
    blk = lax.broadcasted_iota(jnp.int32, gate.shape, 1)
    blk_f = blk.astype(F32)
    own = (qi * t + (lax.broadcasted_iota(jnp.int32, gate.shape, 0) & (t - 1))) >> MOBA_SHIFT
    gate = jnp.where(blk < own, gate, NEG_INF)
    sel = jnp.zeros(gate.shape, F32)
    for _ in range(min(MOBA_TOPK, nb)):
        mx = jnp.max(gate, axis=1, keepdims=True)
        idx = jnp.min(jnp.where(gate == mx, blk_f, float(LANES)), axis=1, keepdims=True)
        hit = blk_f == idx
        sel = jnp.where(hit & (mx > 0.5 * NEG_INF), 1.0, sel)
        gate = jnp.where(hit, REMOVED, gate)
    return jnp.where((sel > 0.5) | (blk == own), 0.0, NEG_INF).astype(BF)


def _moba_kernel(q_ref, k_ref, v_ref, km_ref, ind_ref, o_ref, m_ref, l_ref, acc_ref, *, t, nb):
    qi = pl.program_id(2)
    _init_stats(m_ref, l_ref, acc_ref)
    q_aug = []
    for c in range(CHAINS):
        q2 = _split_halves(q_ref[0, :, _lane_block(c)])
        bias = _moba_select(q2, km_ref[0, :, _lane_block(c)], qi, t, nb)
        q_aug.append(jnp.concatenate([q2, bias], axis=1))

    def tile(j, diagonal):
        start = pl.multiple_of(j * t, t)
        ind = ind_ref[pl.ds(start, t), :]
        if diagonal:
            row = lax.broadcasted_iota(jnp.int32, (2 * t, t), 0) & (t - 1)
            col = lax.broadcasted_iota(jnp.int32, (2 * t, t), 1)
            future = ((row >> MOBA_SHIFT) == (col >> MOBA_SHIFT)) & (col > row)
        for c in range(CHAINS):
            k_aug = jnp.concatenate([k_ref[0, pl.ds(start, t), _lane_block(c)], ind], axis=1)
            v = v_ref[0, pl.ds(start, t), _lane_block(c)]
            s = lax.dot_general(q_aug[c], k_aug, _NT, preferred_element_type=F32)
            if diagonal:
                s = jnp.where(future, NEG_INF, s)
            _softmax_step(s, v, m_ref.at[c], l_ref.at[c], acc_ref.at[c])

    tile(qi, True)

    def body(j, carry):
        tile(j, False)
        return carry

    lax.fori_loop(0, qi, body, 0)

    lane = lax.broadcasted_iota(jnp.int32, (t, LANES), 1)
    for c in range(CHAINS):
        o = _normalised(l_ref.at[c], acc_ref.at[c])
        o_ref[0, :, _lane_block(c)] = jnp.where(lane < HEAD_DIM, o[:t], o[t:]).astype(BF)


def _moba_attention(qkv, kmean, batch, seq, t):
    nb = seq // MOBA_BLOCK
    ns = D_MODEL // LANES // CHAINS
    w = CHAINS * LANES
    qkv3 = qkv.reshape(batch, seq, 3 * D_MODEL)
    km3 = kmean.reshape(batch, nb, D_MODEL)
    ind = _block_indicator(seq, MOBA_SHIFT)
    out = pl.pallas_call(
        functools.partial(_moba_kernel, t=t, nb=nb),
        grid=(batch, ns, seq // t),
        in_specs=[
            pl.BlockSpec((1, t, w), lambda b, p, i: (b, i, p)),
            pl.BlockSpec((1, seq, w), lambda b, p, i: (b, 0, ns + p)),
            pl.BlockSpec((1, seq, w), lambda b, p, i: (b, 0, 2 * ns + p)),
            pl.BlockSpec((1, nb, w), lambda b, p, i: (b, 0, p)),
            pl.BlockSpec(ind.shape, lambda b, p, i: (0, 0)),
        ],
        out_specs=pl.BlockSpec((1, t, w), lambda b, p, i: (b, i, p)),
        out_shape=jax.ShapeDtypeStruct((batch, seq, D_MODEL), BF),
        scratch_shapes=_stats_scratch(CHAINS, 2 * t),
        compiler_params=_cparams(("parallel", "parallel", "arbitrary")),
        name="moba_attention",
    )(qkv3, qkv3, qkv3, km3, ind)
    return out.reshape(batch * seq, D_MODEL)


def _compress_kernel(r_ref, pe_ref, w1_ref, w2_ref, o_ref):
    r = r_ref[0, 0, 0]
    ng = r.shape[0]
    a0 = (r + pe_ref[0, 0]).astype(BF)
    a1 = (r + pe_ref[0, 1]).astype(BF)
    y0 = jnp.dot(a0, w1_ref[0, 0], preferred_element_type=F32)
    y1 = jnp.dot(a1, w1_ref[0, 1], preferred_element_type=F32)
    pre = y0 + pltpu.roll(y1, ng - 1, 0)
    hid = pre * (1.0 / (1.0 + jnp.exp(-pre)))
    o_ref[0, 0, 0] = jnp.dot(hid.astype(BF), w2_ref[0], preferred_element_type=F32)


def _compress(r, pe, w1, w2):
    two, batch, ngrp, ng, wd = r.shape
    return pl.pallas_call(
        _compress_kernel,
        grid=(two, batch, ngrp),
        in_specs=[
            pl.BlockSpec((1, 1, 1, ng, wd), lambda a, b, g: (a, b, g, 0, 0)),
            pl.BlockSpec((1, 2, 1, wd), lambda a, b, g: (a, 0, 0, 0)),
            pl.BlockSpec((1, 2, wd, CMP_HIDDEN), lambda a, b, g: (a, 0, 0, 0)),
            pl.BlockSpec((1, CMP_HIDDEN, HEAD_DIM), lambda a, b, g: (a, 0, 0)),
        ],
        out_specs=pl.BlockSpec((1, 1, 1, ng, HEAD_DIM), lambda a, b, g: (a, b, g, 0, 0)),
        out_shape=jax.ShapeDtypeStruct((two, batch, ngrp, ng, HEAD_DIM), F32),
        compiler_params=_cparams(("parallel", "parallel", "parallel")),
        name="nsa_compress",
    )(r, pe, w1, w2)


def _nsa_stack_q(q, t):
    lane = lax.broadcasted_iota(jnp.int32, (t, LANES), 1)
    zero = jnp.zeros((t, LANES), q.dtype)
    parts = []
    for half in range(2):
        keep = (lane < HEAD_DIM) if half == 0 else (lane >= HEAD_DIM)
        for r in range(4):
            parts.append(jnp.where(keep, q[:, r * LANES:(r + 1) * LANES], zero))
    return jnp.concatenate(parts, axis=0)


def _nsa_write(o, gl, branch, o_ref, t, gp=0):
    lane = lax.broadcasted_iota(jnp.int32, (t, LANES), 1)
    for r in range(4):
        c0 = r * 3 + branch
        c1 = 12 + r * 3 + branch
        g0 = 1.0 / (1.0 + jnp.exp(-gl[:, c0:c0 + 1]))
        g1 = 1.0 / (1.0 + jnp.exp(-gl[:, c1:c1 + 1]))
        blk = jnp.where(lane < HEAD_DIM, g0 * o[r * t:(r + 1) * t], g1 * o[(4 + r) * t:(5 + r) * t])
        o_ref[0, :, _lane_block(4 * gp + r)] = blk.astype(BF)


def _nsa_cmp_kernel(q_ref, kc_ref, vc_ref, gl_ref, ovl_ref, o_ref, sel_ref, *, t, ncmp):
    qi = pl.program_id(2)
    qs = _nsa_stack_q(q_ref[0], t)
    s = lax.dot_general(qs, kc_ref[0], _NT, preferred_element_type=F32)
    n_idx = lax.broadcasted_iota(jnp.int32, s.shape, 1)
    qidx = qi * t + (lax.broadcasted_iota(jnp.int32, s.shape, 0) & (t - 1))
    mask = (n_idx * CMP_STRIDE + (CMP_LEN - 1)) <= qidx
    sm = jnp.where(mask, s, NEG_INF)
    p = jnp.where(mask, jnp.exp2(sm - jnp.max(sm, axis=1, keepdims=True)), 0.0)
    l = jnp.sum(p, axis=1, keepdims=True)
    pn = p * (1.0 / jnp.where(l > 0.0, l, 1.0))
    o = jnp.dot(pn.astype(BF), vc_ref[0], preferred_element_type=F32)
    _nsa_write(o, gl_ref[0], 0, o_ref, t)

    nsel = ovl_ref.shape[0]
    jb = lax.broadcasted_iota(jnp.int32, (nsel, t), 0)
    qblk =(qi * t + lax.broadcasted_iota(jnp.int32, (nsel, t), 1)) >> SLC_SHIFT
    forced = (jb == 0) | (jb == qblk) | (jb == qblk - 1)
    ovl = ovl_ref[...]
    for half in range(2):
        ps = pn[(half * 4) * t:(half * 4 + 1) * t]
        for r in range(1, 4):
            ps = ps + pn[(half * 4 + r) * t:(half * 4 + r + 1) * t]
        ps_hi = ps.astype(BF)
        ps_lo = (ps - ps_hi.astype(F32)).astype(BF)
        imp = (lax.dot_general(ovl, ps_hi, _NT, preferred_element_type=F32)
               + lax.dot_general(ovl, ps_lo, _NT, preferred_element_type=F32))
        val = jnp.where(jb <= qblk, jnp.where(forced, FORCED_SCORE, imp), NEG_INF)
        slabs = [val[g * 8:(g + 1) * 8] for g in range(nsel // 8)]
        row8 = lax.broadcasted_iota(jnp.int32, (8, t), 0)
        cnt = [jnp.zeros((8, t), F32) for _ in slabs]
        for i in range(nsel):
            vi = val[i:i + 1, :]
            for g, sl in enumerate(slabs):
                if g > i // 8:
                    ahead = vi >= sl
                elif g < i // 8:
                    ahead = vi > sl
                else:
                    ahead = (vi > sl) | ((vi == sl) & (row8 > (i % 8)))
                cnt[g] = jnp.where(ahead, cnt[g] + 1.0, cnt[g])
        rank = jnp.concatenate(cnt, axis=0)
        chosen = jnp.where((rank < float(min(SLC_TOPK, nsel))) & (jb <= qblk), 1.0, 0.0)
        pad = jnp.zeros((LANES - nsel, t), F32)
        sel_ref[0, 0, half] = jnp.concatenate([chosen, pad], axis=0).T.astype(BF)


def _nsa_sel_kernel(q_ref, k_ref, v_ref, gl_ref, sel_ref, ind_ref, o_ref, m_ref, l_ref, acc_ref, *, t, tk):
    qi = pl.program_id(1)
    q_aug = []
    for gp in range(2):
        qs = _nsa_stack_q(q_ref[0, :, gp * 4 * LANES:(gp + 1) * 4 * LANES], t)
        for half in range(2):
            bias = jnp.where(sel_ref[0, gp, half].astype(F32) > 0.5, 0.0, NEG_INF).astype(BF)
            q_aug.append(jnp.concatenate([qs[half * 4 * t:(half + 1) * 4 * t],
                                          jnp.concatenate([bias] * 4, axis=0)], axis=1))
    _init_stats(m_ref, l_ref, acc_ref)
    jd = (qi * t) // tk

    def tile(j, causal):
        start = pl.multiple_of(j * tk, tk)
        ind = ind_ref[pl.ds(start, tk), :]
        if causal:
            qidx = qi * t + (lax.broadcasted_iota(jnp.int32, (4 * t, tk), 0) & (t - 1))
            kidx = j * tk + lax.broadcasted_iota(jnp.int32, (4 * t, tk), 1)
        ss = []
        for gp in range(2):
            k_aug = jnp.concatenate([k_ref[0, pl.ds(start, tk), _lane_block(gp)], ind], axis=1)
            for half in range(2):
                s = lax.dot_general(q_aug[2 * gp + half], k_aug, _NT, preferred_element_type=F32)
                ss.append(jnp.where(kidx <= qidx, s, NEG_INF) if causal else s)
        for c in range(4):
            v = v_ref[0, pl.ds(start, tk), _lane_block(c // 2)]
            _softmax_step(ss[c], v, m_ref.at[c], l_ref.at[c], acc_ref.at[c])

    tile(jd, True)

    def body(j, carry):
        tile(j, False)
        return carry

    lax.fori_loop(0, jd, body, 0)
    for gp in range(2):
        o = jnp.concatenate([_normalised(l_ref.at[2 * gp + half], acc_ref.at[2 * gp + half]) for half in range(2)],
                            axis=0)
        _nsa_write(o, gl_ref[0, :, _lane_block(gp)], 1, o_ref, t, gp)


def _nsa_win_kernel(q_ref, k_ref, v_ref, gl_ref, o_ref, m_ref, l_ref, acc_ref, *, t):
    qi = pl.program_id(1)
    qs = [_nsa_stack_q(q_ref[0, :, gp * 4 * LANES:(gp + 1) * 4 * LANES], t) for gp in range(2)]
    _init_stats(m_ref, l_ref, acc_ref)

    def tile(j, kind):
        start = pl.multiple_of(j * t, t)
        if kind != "full":
            row = lax.broadcasted_iota(jnp.int32, (8 * t, t), 0) & (t - 1)
            col = lax.broadcasted_iota(jnp.int32, (8 * t, t), 1)
            keep = (col <= row) if kind == "causal" else (col > row)
        for gp in range(2):
            k = k_ref[0, pl.ds(start, t), _lane_block(gp)]
            v = v_ref[0, pl.ds(start, t), _lane_block(gp)]
            s = lax.dot_general(qs[gp], k, _NT, preferred_element_type=F32)
            if kind != "full":
                s = jnp.where(keep, s, NEG_INF)
            _softmax_step(s, v, m_ref.at[gp], l_ref.at[gp], acc_ref.at[gp])

    tile(qi, "causal")

    @pl.when(qi >= 1)
    def _():
        tile(qi - 1, "full")

    @pl.when(qi >= 2)
    def _():
        tile(qi - 2, "tail")

    for gp in range(2):
        _nsa_write(_normalised(l_ref.at[gp], acc_ref.at[gp]), gl_ref[0, :, _lane_block(gp)], 2, o_ref, t, gp)


def _nsa_attention(pa, pf, kcmp, vcmp, ovl_t, batch, seq, t, tk):
    pa3 = pa.reshape(batch, seq, pa.shape[1])
    pf3 = pf.reshape(batch, seq, pf.shape[1])
    grid = (batch, 2, seq // t)
    ncmp = kcmp.shape[1]
    sem = _cparams(("parallel", "parallel", "arbitrary"))
    o_shape = jax.ShapeDtypeStruct((batch, seq, D_MODEL), BF)

    def q_specs(rows):
        q_spec = pl.BlockSpec((1, rows, 4 * LANES), lambda b, g, i: (b, i, g))
        gl_spec = pl.BlockSpec((1, rows, LANES), lambda b, g, i: (b, i, 4 + g))
        return q_spec, gl_spec, q_spec

    q_spec, gl_spec, o_spec = q_specs(t)
    ind = _block_indicator(seq, SLC_SHIFT)

    def kv_spec(col_block):
        return pl.BlockSpec((1, seq, LANES), lambda b, g, i: (b, 0, col_block + g))

    o_c, sel = pl.pallas_call(
        functools.partial(_nsa_cmp_kernel, t=t, ncmp=ncmp),
        grid=grid,
        in_specs=[
            q_spec,
            pl.BlockSpec((1, ncmp, LANES), lambda b, g, i: (b, 0, g)),
            pl.BlockSpec((1, ncmp, LANES), lambda b, g, i: (b, 0, g)),
            gl_spec,
            pl.BlockSpec(ovl_t.shape, lambda b, g, i: (0, 0)),
        ],
        out_specs=[o_spec, pl.BlockSpec((1, 1, 2, t, LANES), lambda b, g, i: (b, g, 0, i, 0))],
        out_shape=[o_shape, jax.ShapeDtypeStruct((batch, 2, 2, seq, LANES), BF)],
        compiler_params=sem,
        name="nsa_compressed",
    )(pa3, kcmp, vcmp, pf3, ovl_t)

    sem2 = _cparams(("parallel", "arbitrary"))
    kvw = 2 * LANES

    def row_spec(rows):
        return pl.BlockSpec((1, rows, D_MODEL), lambda b, i: (b, i, 0))

    def kv2_spec(col_block):
        return pl.BlockSpec((1, seq, kvw), lambda b, i: (b, 0, col_block))

    def gl2_spec(rows):
        return pl.BlockSpec((1, rows, kvw), lambda b, i: (b, i, 2))

    ts = 2 * t
    o_s = pl.pallas_call(
        functools.partial(_nsa_sel_kernel, t=ts, tk=tk),
        grid=(batch, seq // ts),
        in_specs=[row_spec(ts), kv2_spec(4), kv2_spec(6), gl2_spec(ts),
                  pl.BlockSpec((1, 2, 2, ts, LANES), lambda b, i: (b, 0, 0, i, 0)),
                  pl.BlockSpec(ind.shape, lambda b, i: (0, 0))],
        out_specs=row_spec(ts),
        out_shape=o_shape,
        scratch_shapes=_stats_scratch(4, 4 * ts),
        compiler_params=sem2,
        name="nsa_selected",
    )(pa3, pa3, pa3, pf3, sel, ind)

    tw = WINDOW // 2
    o_w = pl.pallas_call(
        functools.partial(_nsa_win_kernel, t=tw),
        grid=(batch, seq // tw),
        in_specs=[row_spec(tw), kv2_spec(5), kv2_spec(7), gl2_spec(tw)],
        out_specs=row_spec(tw),
        out_shape=o_shape,
        scratch_shapes=_stats_scratch(2, 8 * tw),
        compiler_params=sem2,
        name="nsa_window",
    )(pa3, pa3, pa3, pf3)
    n = batch * seq
    return [o_c.reshape(n, D_MODEL), o_s.reshape(n, D_MODEL), o_w.reshape(n, D_MODEL)]


def _rope_tables(seq):
    half = ROT_DIM // 2
    inv_freq = ROPE_THETA ** (-jnp.arange(half, dtype=F32) / half)
    ang = jnp.arange(seq).astype(F32)[:, None] * inv_freq[None, :]
    cos, sin = jnp.cos(ang), jnp.sin(ang)
    d = np.arange(LANES) % HEAD_DIM
    idx = d % half
    cos_t = jnp.where(d[None, :] < ROT_DIM, cos[:, idx], 1.0)
    sa_t = jnp.where(d[None, :] < half, -sin[:, idx], 0.0)
    sb_t = jnp.where((d[None, :] >= half) & (d[None, :] < ROT_DIM), sin[:, idx], 0.0)
    return cos_t.astype(F32), sa_t.astype(F32), sb_t.astype(F32)


def _block_indicator(seq, shift):
    blk = np.arange(seq)[:, None] >> shift
    return jnp.asarray((blk == np.arange(LANES)[None, :]).astype(np.float32), BF)


def _nsa_q_perm():
    cols = []
    for gp in range(2):
        for r in range(4):
            for half in range(2):
                head = 8 * gp + 4 * half + r
                cols.extend(range(head * HEAD_DIM, (head + 1) * HEAD_DIM))
    return np.asarray(cols, np.int32)


def _nsa_gate_cols():
    src = -np.ones(2 * LANES, np.int32)
    for gp in range(2):
        for half in range(2):
            for r in range(4):
                for br in range(3):
                    src[gp * LANES + half * 12 + r * 3 + br] = (4 * (2 * gp + half) + r) * 3 + br
    return src


def kernel(x, norm_g, ffn_w_in, ffn_w_out, diff_w_in, diff_w_out, diff_lambda, diff_subln, moba_w_in, moba_w_out,
           nsa_w_in, nsa_w_out, nsa_cmp_pe, nsa_cmp_w1, nsa_cmp_w2):
    batch, seq, d = x.shape
    n = batch * seq
    tm = 512
    h = x.reshape(n, d)
    tables = _rope_tables(seq)

    for i in range(DEPTH):
        g = norm_g[i]
        h = _ffn(h, g[0:2], ffn_w_in[i, 0].astype(BF), ffn_w_out[i, 0].astype(BF), tm)

        kind, j = i % 3, i // 3
        if kind == 0:
            lambda_init = 0.8 - 0.6 * math.exp(-0.3 * i)
            plan = [(c * 256, 256, c < 8, 0, c * 256, None, Q_SCALE if c < 4 else None) for c in range(12)]
            (qkv,) = _proj(h, g[2:3], diff_w_in[j].astype(BF), tables, plan, [(3 * D_MODEL, BF)], tm, seq)
            attn = _diff_attention(qkv, diff_lambda[j], diff_subln[j].reshape(1, LANES), batch, seq, 512,
                                   lambda_init)
            parts, w_out = [attn], diff_w_out[j]
        elif kind == 1:
            plan = [(c * 256, 256, c < 8, 0, c * 256, (c - 4) * 256 if 4 <= c < 8 else None,
                     Q_SCALE if c < 4 else None) for c in range(12)]
            qkv, kmean = _proj(h, g[2:3], moba_w_in[j].astype(BF), tables, plan, [(3 * D_MODEL, BF)], tm, seq,
                               kmean_width=D_MODEL)
            attn = _moba_attention(qkv, kmean, batch, seq, 512)
            parts, w_out = [attn], moba_w_out[j]
        else:
            w = nsa_w_in[j]
            perm = _nsa_q_perm()
            kvw = NSA_GROUPS * HEAD_DIM
            base = NSA_HEADS * HEAD_DIM
            seg = {name: w[:, base + k * kvw: base + (k + 1) * kvw]
                   for k, name in enumerate(["kc", "vc", "ks", "vs", "kw", "vw"])}
            gsrc = _nsa_gate_cols()
            glog = w[:, base + 6 * kvw:]
            gate_w = jnp.where(gsrc[None, :] >= 0, glog[:, np.maximum(gsrc, 0)], 0.0)
            w_all = jnp.concatenate([w[:, perm], seg["ks"], seg["kw"], seg["kc"], seg["vs"], seg["vw"], seg["vc"],
                                     gate_w], axis=1).astype(BF)
            plan = [(c * 256, 256, True, 0, c * 256, None, Q_SCALE if c < 4 else None)
                    for c in range(6)]
            plan.append((1536, 256, True, 1, 0, None, None))
            plan.append((1792, 256, False, 0, 1536, None, None))
            plan.append((2048, 256, False, 0, 1792, None, None))
            plan.append((2304, 256, False, 1, 256, None, None))
            plan.append((2560, 256, False, 1, 512, None, None))
            pa, pf = _proj(h, g[2:3], w_all, tables, plan, [(2048, BF), (768, F32)], tm, seq)

            ng = seq // CMP_STRIDE
            kcvc = pf[:, :2 * kvw].reshape(batch, seq, 2, NSA_GROUPS, HEAD_DIM).transpose(2, 0, 3, 1, 4)
            r = kcvc.reshape(2, batch, NSA_GROUPS, ng, CMP_STRIDE * HEAD_DIM)
            pe = nsa_cmp_pe[j].reshape(2, 2, 1, CMP_STRIDE * HEAD_DIM)
            w1 = nsa_cmp_w1[j].reshape(2, 2, CMP_STRIDE * HEAD_DIM, CMP_HIDDEN).astype(BF)
            w2 = nsa_cmp_w2[j].astype(BF)
            cmp_out = _compress(r, pe, w1, w2)
            cmp_tm = cmp_out.transpose(0, 1, 3, 2, 4).reshape(2, batch, ng, kvw).astype(BF)

            nsel = seq // SLC_BLOCK
            cs = np.arange(ng)[:, None] * CMP_STRIDE
            bs = np.arange(nsel)[None, :] * SLC_BLOCK
            ovl = ((cs < bs + SLC_BLOCK) & (cs + CMP_LEN > bs)).astype(np.float32)
            ovl[ng - 1, :] = 0.0
            ovl_t = jnp.asarray(ovl.T, BF)
            parts = _nsa_attention(pa, pf, cmp_tm[0], cmp_tm[1], ovl_t, batch, seq, 128, 512)
            w_out = nsa_w_out[j][perm, :]

        h = _ffn(h, g[4:6], ffn_w_in[i, 1].astype(BF), ffn_w_out[i, 1].astype(BF), tm,
                 parts=parts, wp=w_out.astype(BF), gp=g[3:4])
    return h.reshape(batch, seq, d)
```

```python
import functools
import math

import jax
import jax.numpy as jnp
import numpy as np
from jax import lax
from jax.experimental import pallas as pl
from jax.experimental.pallas import tpu as pltpu

D_MODEL = 1024
DEPTH = 4
HEAD_DIM = 64
ROT_DIM = HEAD_DIM // 4
ROPE_THETA = 500000.0
NORM_EPS = 1e-6
NEG_INF = -1e30
REMOVED = -3e38

DIFF_HEADS = 8
DIFF_SUBLN_EPS = 1e-5
MOBA_BLOCK = 256
MOBA_SHIFT = 8
MOBA_TOPK = 3
NSA_HEADS = 16
NSA_GROUPS = 4
CMP_LEN = 32
CMP_STRIDE = 16
CMP_HIDDEN = 256
SLC_BLOCK = 64
SLC_SHIFT = 6
SLC_TOPK = 16
WINDOW = 512
FORCED_SCORE = 1e9
D_FF = 2816

LANES = 128
FF_CHUNK = 256
CHAINS = 4
Q_SCALE = HEAD_DIM ** -0.5 * math.log2(math.e)

BF = jnp.bfloat16
F32 = jnp.float32
VMEM_LIMIT = 56 * 1024 * 1024

_NT = (((1,), (1,)), ((), ()))


def _cparams(sem):
    return pltpu.CompilerParams(dimension_semantics=sem, vmem_limit_bytes=VMEM_LIMIT)


def _rms(x, g, eps):
    return x * lax.rsqrt(jnp.mean(x * x, axis=-1, keepdims=True) + eps) * g


def _resident(shape):
    nd = len(shape)
    return pl.BlockSpec(shape, lambda *_: (0,) * nd, pipeline_mode=pl.Buffered(1))


def _ffn_kernel(*refs, n_parts):
    h_ref = refs[0]
    parts = refs[1:1 + n_parts]
    if n_parts:
        wp_ref, gp_ref = refs[1 + n_parts:3 + n_parts]
        refs = refs[3 + n_parts:]
    else:
        refs = refs[1:]
    g_ref, wi_ref, wo_ref, o_ref, acc_ref = refs
    h = h_ref[...]
    if n_parts:
        a = parts[0][...]
        if n_parts > 1:
            a = a.astype(F32)
            for p in parts[1:]:
                a = a + p[...].astype(F32)
            a = a.astype(BF)
        h = h + _rms(jnp.dot(a, wp_ref[...], preferred_element_type=F32), gp_ref[...], NORM_EPS)
    xn = _rms(h, g_ref[0:1, :], NORM_EPS).astype(BF)
    for c in range(D_FF // FF_CHUNK):
        lo = c * FF_CHUNK
        gate = jnp.dot(xn, wi_ref[:, lo:lo + FF_CHUNK], preferred_element_type=F32)
        up = jnp.dot(xn, wi_ref[:, D_FF + lo:D_FF + lo + FF_CHUNK], preferred_element_type=F32)
        act = ((gate * (1.0 / (1.0 + jnp.exp(-gate)))) * up).astype(BF)
        part = jnp.dot(act, wo_ref[lo:lo + FF_CHUNK, :], preferred_element_type=F32)
        if c == 0:
            acc_ref[...] = part
        else:
            acc_ref[...] += part
    o_ref[...] = h + 0.5 * _rms(acc_ref[...], g_ref[1:2, :], NORM_EPS)


def _ffn(h, g2, wi, wo, tm, parts=(), wp=None, gp=None):
    n, d = h.shape
    row = pl.BlockSpec((tm, d), lambda i: (i, 0))
    proj_args = [wp, gp] if parts else []
    return pl.pallas_call(
        functools.partial(_ffn_kernel, n_parts=len(parts)),
        grid=(n // tm,),
        in_specs=([row] * (1 + len(parts)) + [_resident(a.shape) for a in proj_args]
                  + [_resident(g2.shape), _resident(wi.shape), _resident(wo.shape)]),
        out_specs=row,
        out_shape=jax.ShapeDtypeStruct((n, d), F32),
        scratch_shapes=[pltpu.VMEM((tm, d), F32)],
        compiler_params=_cparams(("parallel",)),
        name="ffn_halfstep",
    )(h, *parts, *proj_args, g2, wi, wo)


def _proj_kernel(h_ref, g_ref, w_ref, cos_ref, sa_ref, sb_ref, *out_refs, plan, tm, n_out):
    xn = _rms(h_ref[...], g_ref[...], NORM_EPS).astype(BF)
    cos = cos_ref[...]
    sa = sa_ref[...]
    sb = sb_ref[...]
    for col, width, rope, dest, off, km_off, scale in plan:
        y = jnp.dot(xn, w_ref[:, col:col + width], preferred_element_type=F32)
        for k in range(width // LANES):
            yk = y[:, k * LANES:(k + 1) * LANES]
            if rope:
                yk = yk * cos + pltpu.roll(yk, LANES - ROT_DIM // 2, 1) * sa + pltpu.roll(yk, ROT_DIM // 2, 1) * sb
            if scale is not None:
                yk = yk * scale
            o_ref = out_refs[dest]
            o_ref[:, off + k * LANES:off + (k + 1) * LANES] = yk.astype(o_ref.dtype)
            if km_off is not None:
                km_ref = out_refs[n_out]
                for r in range(tm // MOBA_BLOCK):
                    blk = yk[r * MOBA_BLOCK:(r + 1) * MOBA_BLOCK, :]
                    km_ref[0, r:r + 1, km_off + k * LANES:km_off + (k + 1) * LANES] = jnp.mean(
                        blk, axis=0, keepdims=True)


def _proj(h, g, w, tables, plan, outs, tm, seq, kmean_width=None):
    n, d = h.shape
    per_seq = seq // tm
    out_shape = [jax.ShapeDtypeStruct((n, wd), dt) for wd, dt in outs]
    out_specs = [pl.BlockSpec((tm, wd), lambda i: (i, 0)) for wd, _ in outs]
    if kmean_width is not None:
        nb = tm // MOBA_BLOCK
        out_shape.append(jax.ShapeDtypeStruct((n // tm, nb, kmean_width), F32))
        out_specs.append(pl.BlockSpec((1, nb, kmean_width), lambda i: (i, 0, 0)))
    tab_spec = pl.BlockSpec((tm, LANES), lambda i: (i % per_seq, 0))
    return pl.pallas_call(
        functools.partial(_proj_kernel, plan=tuple(plan), tm=tm, n_out=len(outs)),
        grid=(n // tm,),
        in_specs=[
            pl.BlockSpec((tm, d), lambda i: (i, 0)),
            _resident(g.shape),
            _resident(w.shape),
            tab_spec, tab_spec, tab_spec,
        ],
        out_specs=out_specs,
        out_shape=out_shape,
        compiler_params=_cparams(("parallel",)),
        name="norm_proj",
    )(h, g, w, *tables)


def _softmax_step(s, v, m_ref, l_ref, acc_ref):
    m_prev = m_ref[...]
    m_new = jnp.maximum(m_prev, jnp.max(s, axis=1, keepdims=True))
    alpha = jnp.exp2(m_prev - m_new)
    ps = [jnp.exp2(s[:, c * LANES:(c + 1) * LANES] - m_new) for c in range(s.shape[1] // LANES)]
    lsum = ps[0]
    for p in ps[1:]:
        lsum = lsum + p
    l_ref[...] = alpha * l_ref[...] + lsum
    p = jnp.concatenate([x.astype(BF) for x in ps], axis=1)
    acc_ref[...] = alpha * acc_ref[...] + jnp.dot(p, v, preferred_element_type=F32)
    m_ref[...] = m_new


def _init_stats(m_ref, l_ref, acc_ref):
    m_ref[...] = jnp.full(m_ref.shape, NEG_INF, F32)
    l_ref[...] = jnp.zeros(l_ref.shape, F32)
    acc_ref[...] = jnp.zeros(acc_ref.shape, F32)


def _normalised(l_ref, acc_ref):
    return acc_ref[...] * (1.0 / jnp.sum(l_ref[...], axis=1, keepdims=True))


def _stats_scratch(chains, rows):
    return [pltpu.VMEM((chains, rows, LANES), F32) for _ in range(3)]


def _split_halves(q):
    lane = lax.broadcasted_iota(jnp.int32, q.shape, 1)
    zero = jnp.zeros_like(q)
    return jnp.concatenate([jnp.where(lane < HEAD_DIM, q, zero), jnp.where(lane >= HEAD_DIM, q, zero)], axis=0)


def _lane_block(c):
    return slice(c * LANES, (c + 1) * LANES)


def _diff_kernel(q_ref, k_ref, v_ref, lam_ref, sg_ref, o_ref, m_ref, l_ref, acc_ref, *, t, lambda_init):
    qi = pl.program_id(2)
    qs = [_split_halves(q_ref[0, :, _lane_block(c)]) for c in range(CHAINS)]
    _init_stats(m_ref, l_ref, acc_ref)

    def tile(j, causal):
        start = pl.multiple_of(j * t, t)
        ss = [lax.dot_general(qs[c], k_ref[0, pl.ds(start, t), _lane_block(c)], _NT, preferred_element_type=F32)
              for c in range(CHAINS)]
        if causal:
            row = lax.broadcasted_iota(jnp.int32, (2 * t, t), 0) & (t - 1)
            col = lax.broadcasted_iota(jnp.int32, (2 * t, t), 1)
            ss = [jnp.where(col <= row, s, NEG_INF) for s in ss]
        for c in range(CHAINS):
            v = v_ref[0, pl.ds(start, t), _lane_block(c)]
            _softmax_step(ss[c], v, m_ref.at[c], l_ref.at[c], acc_ref.at[c])

    tile(qi, True)

    def body(j, carry):
        tile(j, False)
        return carry

    lax.fori_loop(0, qi, body, 0)

    lam = lam_ref[...]
    lam_full = (jnp.exp(jnp.sum(lam[0:1] * lam[1:2], axis=1, keepdims=True))
                - jnp.exp(jnp.sum(lam[2:3] * lam[3:4], axis=1, keepdims=True)) + lambda_init)
    for c in range(CHAINS):
        o = _normalised(l_ref.at[c], acc_ref.at[c])
        od = o[:t] - lam_full * o[t:]
        od = _rms(od, sg_ref[...], DIFF_SUBLN_EPS) * (1.0 - lambda_init)
        o_ref[0, :, _lane_block(c)] = od.astype(BF)


def _diff_attention(qkv, lam, subln, batch, seq, t, lambda_init):
    nh = DIFF_HEADS
    ns = nh // CHAINS
    w = CHAINS * LANES
    qkv3 = qkv.reshape(batch, seq, 3 * nh * LANES)
    out = pl.pallas_call(
        functools.partial(_diff_kernel, t=t, lambda_init=lambda_init),
        grid=(batch, ns, seq // t),
        in_specs=[
            pl.BlockSpec((1, t, w), lambda b, h, i: (b, i, h)),
            pl.BlockSpec((1, seq, w), lambda b, h, i: (b, 0, ns + h)),
            pl.BlockSpec((1, seq, w), lambda b, h, i: (b, 0, 2 * ns + h)),
            pl.BlockSpec(lam.shape, lambda b, h, i: (0, 0)),
            pl.BlockSpec(subln.shape, lambda b, h, i: (0, 0)),
        ],
        out_specs=pl.BlockSpec((1, t, w), lambda b, h, i: (b, i, h)),
        out_shape=jax.ShapeDtypeStruct((batch, seq, nh * LANES), BF),
        scratch_shapes=_stats_scratch(CHAINS, 2 * t),
        compiler_params=_cparams(("parallel", "parallel", "arbitrary")),
        name="diff_attention",
    )(qkv3, qkv3, qkv3, lam, subln)
    return out.reshape(batch * seq, nh * LANES)


def _moba_select(q2, km, qi, t, nb):
    km_hi = km.astype(BF)
    km_lo = (km - km_hi.astype(F32)).astype(BF)
    gate = (lax.dot_general(km_hi, q2, _NT, preferred_element_type=F32)
            + lax.dot_general(km_lo, q2, _NT, preferred_element_type=F32))
    blk = lax.broadcasted_iota(jnp.int32, gate.shape, 0)
    blk_f = blk.astype(F32)
    own = (qi * t + (lax.broadcasted_iota(jnp.int32, gate.shape, 1) & (t - 1))) >> MOBA_SHIFT
    gate = jnp.where(blk < own, gate, NEG_INF)
    sel = jnp.zeros(gate.shape, F32)
    for _ in range(min(MOBA_TOPK, nb)):
        mx = jnp.max(gate, axis=0, keepdims=True)
        idx = jnp.min(jnp.where(gate == mx, blk_f, float(nb)), axis=0, keepdims=True)
        hit = blk_f == idx
        sel = jnp.where(hit & (mx > 0.5 * NEG_INF), 1.0, sel)
        gate = jnp.where(hit, REMOVED, gate)
    bias_t = jnp.where((sel > 0.5) | (blk == own), 0.0, NEG_INF)
    bias_t = jnp.concatenate([bias_t, jnp.zeros((LANES - nb, 2 * t), F32)], axis=0)
    return jnp.concatenate([bias_t[:, _lane_block(n)].T for n in range(2 * t // LANES)], axis=0).astype(BF)


def _moba_kernel(q_ref, k_ref, v_ref, km_ref, ind_ref, o_ref, m_ref, l_ref, acc_ref, *, t, nb):
    qi = pl.program_id(2)
    _init_stats(m_ref, l_ref, acc_ref)
    q_aug = []
    for c in range(CHAINS):
        q2 = _split_halves(q_ref[0, :, _lane_block(c)])
        bias = _moba_select(q2, km_ref[0, :, _lane_block(c)], qi, t, nb)
        q_aug.append(jnp.concatenate([q2, bias], axis=1))

    def tile(j, diagonal):
        start = pl.multiple_of(j * t, t)
        ind = ind_ref[pl.ds(start, t), :]
        if diagonal:
            row = lax.broadcasted_iota(jnp.int32, (2 * t, t), 0) & (t - 1)
            col = lax.broadcasted_iota(jnp.int32, (2 * t, t), 1)
            future = ((row >> MOBA_SHIFT) == (col >> MOBA_SHIFT)) & (col > row)
        for c in range(CHAINS):
            k_aug = jnp.concatenate([k_ref[0, pl.ds(start, t), _lane_block(c)], ind], axis=1)
            v = v_ref[0, pl.ds(start, t), _lane_block(c)]
            s = lax.dot_general(q_aug[c], k_aug, _NT, preferred_element_type=F32)
            if diagonal:
                s = jnp.where(future, NEG_INF, s)
            _softmax_step(s, v, m_ref.at[c], l_ref.at[c], acc_ref.at[c])

    tile(qi, True)

    def body(j, carry):
        tile(j, False)
        return carry

    lax.fori_loop(0, qi, body, 0)

    lane = lax.broadcasted_iota(jnp.int32, (t, LANES), 1)
    for c in range(CHAINS):
        o = _normalised(l_ref.at[c], acc_ref.at[c])
        o_ref[0, :, _lane_block(c)] = jnp.where(lane < HEAD_DIM, o[:t], o[t:]).astype(BF)


def _moba_attention(qkv, kmean, batch, seq, t):
    nb = seq // MOBA_BLOCK
    ns = D_MODEL // LANES // CHAINS
    w = CHAINS * LANES
    qkv3 = qkv.reshape(batch, seq, 3 * D_MODEL)
    km3 = kmean.reshape(batch, nb, D_MODEL)
    ind = _block_indicator(seq, MOBA_SHIFT)
    out = pl.pallas_call(
        functools.partial(_moba_kernel, t=t, nb=nb),
        grid=(batch, ns, seq // t),
        in_specs=[
            pl.BlockSpec((1, t, w), lambda b, p, i: (b, i, p)),
            pl.BlockSpec((1, seq, w), lambda b, p, i: (b, 0, ns + p)),
            pl.BlockSpec((1, seq, w), lambda b, p, i: (b, 0, 2 * ns + p)),
            pl.BlockSpec((1, nb, w), lambda b, p, i: (b, 0, p)),
            pl.BlockSpec(ind.shape, lambda b, p, i: (0, 0)),
        ],
        out_specs=pl.BlockSpec((1, t, w), lambda b, p, i: (b, i, p)),
        out_shape=jax.ShapeDtypeStruct((batch, seq, D_MODEL), BF),
        scratch_shapes=_stats_scratch(CHAINS, 2 * t),
        compiler_params=_cparams(("parallel", "parallel", "arbitrary")),
        name="moba_attention",
    )(qkv3, qkv3, qkv3, km3, ind)
    return out.reshape(batch * seq, D_MODEL)


def _compress_kernel(r_ref, pe_ref, w1_ref, w2_ref, o_ref):
    r = r_ref[0]
    ng = r.shape[0]
    a0 = (r + pe_ref[0]).astype(BF)
    a1 = (r + pe_ref[1]).astype(BF)
    y0 = jnp.dot(a0, w1_ref[0], preferred_element_type=F32)
    y1 = jnp.dot(a1, w1_ref[1], preferred_element_type=F32)
    pre = y0 + pltpu.roll(y1, ng - 1, 0)
    hid = pre * (1.0 / (1.0 + jnp.exp(-pre)))
    o_ref[0] = jnp.dot(hid.astype(BF), w2_ref[...], preferred_element_type=F32).astype(o_ref.dtype)


def _compress(r, pe, w1, w2):
    batch, ng, wd = r.shape
    wout = w2.shape[1]
    return pl.pallas_call(
        _compress_kernel,
        grid=(batch,),
        in_specs=[
            pl.BlockSpec((1, ng, wd), lambda b: (b, 0, 0)),
            _resident(pe.shape),
            _resident(w1.shape),
            _resident(w2.shape),
        ],
        out_specs=pl.BlockSpec((1, ng, wout), lambda b: (b, 0, 0)),
        out_shape=jax.ShapeDtypeStruct((batch, ng, wout), BF),
        compiler_params=_cparams(("parallel",)),
        name="nsa_compress",
    )(r, pe, w1, w2)


def _compress_weights(pe, w1, w2):
    g = NSA_GROUPS
    eye = jnp.eye(g, dtype=F32)
    w1r = w1.reshape(2, CMP_STRIDE, HEAD_DIM, CMP_HIDDEN)
    w1b = jnp.einsum('tldh,gk->tlgdkh', w1r, eye).reshape(2, CMP_STRIDE * g * HEAD_DIM, g * CMP_HIDDEN)
    w2b = jnp.einsum('hd,gk->ghkd', w2, eye).reshape(g * CMP_HIDDEN, g * HEAD_DIM)
    peb = jnp.broadcast_to(pe.reshape(2, CMP_STRIDE, 1, HEAD_DIM), (2, CMP_STRIDE, g, HEAD_DIM))
    return peb.reshape(2, 1, CMP_STRIDE * g * HEAD_DIM), w1b.astype(BF), w2b.astype(BF)


def _nsa_stack_q(q, t):
    lane = lax.broadcasted_iota(jnp.int32, (t, LANES), 1)
    zero = jnp.zeros((t, LANES), q.dtype)
    parts = []
    for half in range(2):
        keep = (lane < HEAD_DIM) if half == 0 else (lane >= HEAD_DIM)
        for r in range(4):
            parts.append(jnp.where(keep, q[:, r * LANES:(r + 1) * LANES], zero))
    return jnp.concatenate(parts, axis=0)


def _nsa_write(o, gl, branch, o_ref, t, gp=0):
    lane = lax.broadcasted_iota(jnp.int32, (t, LANES), 1)
    for r in range(4):
        c0 = r * 3 + branch
        c1 = 12 + r * 3 + branch
        g0 = 1.0 / (1.0 + jnp.exp(-gl[:, c0:c0 + 1]))
        g1 = 1.0 / (1.0 + jnp.exp(-gl[:, c1:c1 + 1]))
        blk = jnp.where(lane < HEAD_DIM, g0 * o[r * t:(r + 1) * t], g1 * o[(4 + r) * t:(5 + r) * t])
        o_ref[0, :, _lane_block(4 * gp + r)] = blk.astype(BF)


def _nsa_cmp_kernel(q_ref, kc_ref, vc_ref, gl_ref, ovl_ref, o_ref, sel_ref, *, t, ncmp):
    qi = pl.program_id(2)
    qs = _nsa_stack_q(q_ref[0], t)
    s = lax.dot_general(qs, kc_ref[0], _NT, preferred_element_type=F32)
    n_idx = lax.broadcasted_iota(jnp.int32, s.shape, 1)
    qidx = qi * t + (lax.broadcasted_iota(jnp.int32, s.shape, 0) & (t - 1))
    mask = (n_idx * CMP_STRIDE + (CMP_LEN - 1)) <= qidx
    sm = jnp.where(mask, s, NEG_INF)
    p = jnp.where(mask, jnp.exp2(sm - jnp.max(sm, axis=1, keepdims=True)), 0.0)
    l = jnp.sum(p, axis=1, keepdims=True)
    pn = p * (1.0 / jnp.where(l > 0.0, l, 1.0))
    o = jnp.dot(pn.astype(BF), vc_ref[0], preferred_element_type=F32)
    _nsa_write(o, gl_ref[0], 0, o_ref, t)

    nsel = ovl_ref.shape[0]
    jb = lax.broadcasted_iota(jnp.int32, (nsel, t), 0)
    qblk =(qi * t + lax.broadcasted_iota(jnp.int32, (nsel, t), 1)) >> SLC_SHIFT
    forced = (jb == 0) | (jb == qblk) | (jb == qblk - 1)
    ovl = ovl_ref[...]
    for half in range(2):
        ps = pn[(half * 4) * t:(half * 4 + 1) * t]
        for r in range(1, 4):
            ps = ps + pn[(half * 4 + r) * t:(half * 4 + r + 1) * t]
        ps_hi = ps.astype(BF)
        ps_lo = (ps - ps_hi.astype(F32)).astype(BF)
        imp = (lax.dot_general(ovl, ps_hi, _NT, preferred_element_type=F32)
               + lax.dot_general(ovl, ps_lo, _NT, preferred_element_type=F32))
        val = jnp.where(jb <= qblk, jnp.where(forced, FORCED_SCORE, imp), NEG_INF)
        slabs = [val[g * 8:(g + 1) * 8] for g in range(nsel // 8)]
        row8 = lax.broadcasted_iota(jnp.int32, (8, t), 0)
        cnt = [jnp.zeros((8, t), F32) for _ in slabs]
        for i in range(nsel):
            vi = val[i:i + 1, :]
            for g, sl in enumerate(slabs):
                if g > i // 8:
                    ahead = vi >= sl
                elif g < i // 8:
                    ahead = vi > sl
                else:
                    ahead = (vi > sl) | ((vi == sl) & (row8 > (i % 8)))
                cnt[g] = jnp.where(ahead, cnt[g] + 1.0, cnt[g])
        rank = jnp.concatenate(cnt, axis=0)
        chosen = jnp.where((rank < float(min(SLC_TOPK, nsel))) & (jb <= qblk), 1.0, 0.0)
        pad = jnp.zeros((LANES - nsel, t), F32)
        sel_ref[0, 0, half] = jnp.concatenate([chosen, pad], axis=0).T.astype(BF)


def _nsa_sel_kernel(q_ref, k_ref, v_ref, gl_ref, sel_ref, ind_ref, o_ref, m_ref, l_ref, acc_ref, *, t, tk):
    qi = pl.program_id(1)
    q_aug = []
    for gp in range(2):
        qs = _nsa_stack_q(q_ref[0, :, gp * 4 * LANES:(gp + 1) * 4 * LANES], t)
        for half in range(2):
            bias = jnp.where(sel_ref[0, gp, half].astype(F32) > 0.5, 0.0, NEG_INF).astype(BF)
            q_aug.append(jnp.concatenate([qs[half * 4 * t:(half + 1) * 4 * t],
                                          jnp.concatenate([bias] * 4, axis=0)], axis=1))
    _init_stats(m_ref, l_ref, acc_ref)
    jd = (qi * t) // tk

    def tile(j, causal):
        start = pl.multiple_of(j * tk, tk)
        ind = ind_ref[pl.ds(start, tk), :]
        if causal:
            qidx = qi * t + (lax.broadcasted_iota(jnp.int32, (4 * t, tk), 0) & (t - 1))
            kidx = j * tk + lax.broadcasted_iota(jnp.int32, (4 * t, tk), 1)
        ss = []
        for gp in range(2):
            k_aug = jnp.concatenate([k_ref[0, pl.ds(start, tk), _lane_block(gp)], ind], axis=1)
            for half in range(2):
                s = lax.dot_general(q_aug[2 * gp + half], k_aug, _NT, preferred_element_type=F32)
                ss.append(jnp.where(kidx <= qidx, s, NEG_INF) if causal else s)
        for c in range(4):
            v = v_ref[0, pl.ds(start, tk), _lane_block(c // 2)]
            _softmax_step(ss[c], v, m_ref.at[c], l_ref.at[c], acc_ref.at[c])

    tile(jd, True)

    def body(j, carry):
        tile(j, False)
        return carry

    lax.fori_loop(0, jd, body, 0)
    for gp in range(2):
        o = jnp.concatenate([_normalised(l_ref.at[2 * gp + half], acc_ref.at[2 * gp + half]) for half in range(2)],
                            axis=0)
        _nsa_write(o, gl_ref[0, :, _lane_block(gp)], 1, o_ref, t, gp)


def _nsa_win_kernel(q_ref, k_ref, v_ref, gl_ref, o_ref, m_ref, l_ref, acc_ref, *, t):
    qi = pl.program_id(1)
    qs = [_nsa_stack_q(q_ref[0, :, gp * 4 * LANES:(gp + 1) * 4 * LANES], t) for gp in range(2)]
    _init_stats(m_ref, l_ref, acc_ref)

    def tile(j, kind):
        start = pl.multiple_of(j * t, t)
        if kind != "full":
            row = lax.broadcasted_iota(jnp.int32, (8 * t, t), 0) & (t - 1)
            col = lax.broadcasted_iota(jnp.int32, (8 * t, t), 1)
            keep = (col <= row) if kind == "causal" else (col > row)
        for gp in range(2):
            k = k_ref[0, pl.ds(start, t), _lane_block(gp)]
            v = v_ref[0, pl.ds(start, t), _lane_block(gp)]
            s = lax.dot_general(qs[gp], k, _NT, preferred_element_type=F32)
            if kind != "full":
                s = jnp.where(keep, s, NEG_INF)
            _softmax_step(s, v, m_ref.at[gp], l_ref.at[gp], acc_ref.at[gp])

    tile(qi, "causal")

    @pl.when(qi >= 1)
    def _():
        tile(qi - 1, "full")

    @pl.when(qi >= 2)
    def _():
        tile(qi - 2, "tail")

    for gp in range(2):
        _nsa_write(_normalised(l_ref.at[gp], acc_ref.at[gp]), gl_ref[0, :, _lane_block(gp)], 2, o_ref, t, gp)


def _nsa_attention(pa, pf, kcmp, vcmp, ovl_t, batch, seq, t, tk):
    pa3 = pa.reshape(batch, seq, pa.shape[1])
    pf3 = pf.reshape(batch, seq, pf.shape[1])
    grid = (batch, 2, seq // t)
    ncmp = kcmp.shape[1]
    sem = _cparams(("parallel", "parallel", "arbitrary"))
    o_shape = jax.ShapeDtypeStruct((batch, seq, D_MODEL), BF)

    def q_specs(rows):
        q_spec = pl.BlockSpec((1, rows, 4 * LANES), lambda b, g, i: (b, i, g))
        gl_spec = pl.BlockSpec((1, rows, LANES), lambda b, g, i: (b, i, g))
        return q_spec, gl_spec, q_spec

    q_spec, gl_spec, o_spec = q_specs(t)
    ind = _block_indicator(seq, SLC_SHIFT)

    def kv_spec(col_block):
        return pl.BlockSpec((1, seq, LANES), lambda b, g, i: (b, 0, col_block + g))

    o_c, sel = pl.pallas_call(
        functools.partial(_nsa_cmp_kernel, t=t, ncmp=ncmp),
        grid=grid,
        in_specs=[
            q_spec,
            pl.BlockSpec((1, ncmp, LANES), lambda b, g, i: (b, 0, g)),
            pl.BlockSpec((1, ncmp, LANES), lambda b, g, i: (b, 0, g)),
            gl_spec,
            pl.BlockSpec(ovl_t.shape, lambda b, g, i: (0, 0)),
        ],
        out_specs=[o_spec, pl.BlockSpec((1, 1, 2, t, LANES), lambda b, g, i: (b, g, 0, i, 0))],
        out_shape=[o_shape, jax.ShapeDtypeStruct((batch, 2, 2, seq, LANES), BF)],
        compiler_params=sem,
        name="nsa_compressed",
    )(pa3, kcmp, vcmp, pf3, ovl_t)

    sem2 = _cparams(("parallel", "arbitrary"))
    kvw = 2 * LANES

    def row_spec(rows):
        return pl.BlockSpec((1, rows, D_MODEL), lambda b, i: (b, i, 0))

    def kv2_spec(col_block):
        return pl.BlockSpec((1, seq, kvw), lambda b, i: (b, 0, col_block))

    def gl2_spec(rows):
        return pl.BlockSpec((1, rows, kvw), lambda b, i: (b, i, 0))

    ts = 2 * t
    o_s = pl.pallas_call(
        functools.partial(_nsa_sel_kernel, t=ts, tk=tk),
        grid=(batch, seq // ts),
        in_specs=[row_spec(ts), kv2_spec(4), kv2_spec(6), gl2_spec(ts),
                  pl.BlockSpec((1, 2, 2, ts, LANES), lambda b, i: (b, 0, 0, i, 0)),
                  pl.BlockSpec(ind.shape, lambda b, i: (0, 0))],
        out_specs=row_spec(ts),
        out_shape=o_shape,
        scratch_shapes=_stats_scratch(4, 4 * ts),
        compiler_params=sem2,
        name="nsa_selected",
    )(pa3, pa3, pa3, pf3, sel, ind)

    tw = WINDOW // 2
    o_w = pl.pallas_call(
        functools.partial(_nsa_win_kernel, t=tw),
        grid=(batch, seq // tw),
        in_specs=[row_spec(tw), kv2_spec(5), kv2_spec(7), gl2_spec(tw)],
        out_specs=row_spec(tw),
        out_shape=o_shape,
        scratch_shapes=_stats_scratch(2, 8 * tw),
        compiler_params=sem2,
        name="nsa_window",
    )(pa3, pa3, pa3, pf3)
    n = batch * seq
    return [o_c.reshape(n, D_MODEL), o_s.reshape(n, D_MODEL), o_w.reshape(n, D_MODEL)]


def _rope_tables(seq):
    half = ROT_DIM // 2
    inv_freq = ROPE_THETA ** (-jnp.arange(half, dtype=F32) / half)
    ang = jnp.arange(seq).astype(F32)[:, None] * inv_freq[None, :]
    cos, sin = jnp.cos(ang), jnp.sin(ang)
    d = np.arange(LANES) % HEAD_DIM
    idx = d % half
    cos_t = jnp.where(d[None, :] < ROT_DIM, cos[:, idx], 1.0)
    sa_t = jnp.where(d[None, :] < half, -sin[:, idx], 0.0)
    sb_t = jnp.where((d[None, :] >= half) & (d[None, :] < ROT_DIM), sin[:, idx], 0.0)
    return cos_t.astype(F32), sa_t.astype(F32), sb_t.astype(F32)


def _block_indicator(seq, shift):
    blk = np.arange(seq)[:, None] >> shift
    return jnp.asarray((blk == np.arange(LANES)[None, :]).astype(np.float32), BF)


def _nsa_q_perm():
    cols = []
    for gp in range(2):
        for r in range(4):
            for half in range(2):
                head = 8 * gp + 4 * half + r
                cols.extend(range(head * HEAD_DIM, (head + 1) * HEAD_DIM))
    return np.asarray(cols, np.int32)


def _nsa_gate_cols():
    src = -np.ones(2 * LANES, np.int32)
    for gp in range(2):
        for half in range(2):
            for r in range(4):
                for br in range(3):
                    src[gp * LANES + half * 12 + r * 3 + br] = (4 * (2 * gp + half) + r) * 3 + br
    return src


def kernel(x, norm_g, ffn_w_in, ffn_w_out, diff_w_in, diff_w_out, diff_lambda, diff_subln, moba_w_in, moba_w_out,
           nsa_w_in, nsa_w_out, nsa_cmp_pe, nsa_cmp_w1, nsa_cmp_w2):
    batch, seq, d = x.shape
    n = batch * seq
    tm = 512
    h = x.reshape(n, d)
    tables = _rope_tables(seq)

    for i in range(DEPTH):
        g = norm_g[i]
        h = _ffn(h, g[0:2], ffn_w_in[i, 0].astype(BF), ffn_w_out[i, 0].astype(BF), tm)

        kind, j = i % 3, i // 3
        if kind == 0:
            lambda_init = 0.8 - 0.6 * math.exp(-0.3 * i)
            plan = [(c * 256, 256, c < 8, 0, c * 256, None, Q_SCALE if c < 4 else None) for c in range(12)]
            (qkv,) = _proj(h, g[2:3], diff_w_in[j].astype(BF), tables, plan, [(3 * D_MODEL, BF)], tm, seq)
            attn = _diff_attention(qkv, diff_lambda[j], diff_subln[j].reshape(1, LANES), batch, seq, 512,
                                   lambda_init)
            parts, w_out = [attn], diff_w_out[j]
        elif kind == 1:
            plan = [(c * 256, 256, c < 8, 0, c * 256, (c - 4) * 256 if 4 <= c < 8 else None,
                     Q_SCALE if c < 4 else None) for c in range(12)]
            qkv, kmean = _proj(h, g[2:3], moba_w_in[j].astype(BF), tables, plan, [(3 * D_MODEL, BF)], tm, seq,
                               kmean_width=D_MODEL)
            attn = _moba_attention(qkv, kmean, batch, seq, 512)
            parts, w_out = [attn], moba_w_out[j]
        else:
            w = nsa_w_in[j]
            perm = _nsa_q_perm()
            kvw = NSA_GROUPS * HEAD_DIM
            base = NSA_HEADS * HEAD_DIM
            seg = {name: w[:, base + k * kvw: base + (k + 1) * kvw]
                   for k, name in enumerate(["kc", "vc", "ks", "vs", "kw", "vw"])}
            gsrc = _nsa_gate_cols()
            glog = w[:, base + 6 * kvw:]
            gate_w = jnp.where(gsrc[None, :] >= 0, glog[:, np.maximum(gsrc, 0)], 0.0)
            w_all = jnp.concatenate([w[:, perm], seg["ks"], seg["kw"], seg["kc"], seg["vs"], seg["vw"], seg["vc"],
                                     gate_w], axis=1).astype(BF)
            plan = [(c * 256, 256, True, 0, c * 256, None, Q_SCALE if c < 4 else None)
                    for c in range(6)]
            plan.append((1536, 256, True, 1, 0, None, None))
            plan.append((1792, 256, False, 0, 1536, None, None))
            plan.append((2048, 256, False, 0, 1792, None, None))
            plan.append((2304, 256, False, 2, 0, None, None))
            plan.append((2560, 256, False, 3, 0, None, None))
            pa, kc, vc, pf = _proj(h, g[2:3], w_all, tables, plan,
                                   [(2048, BF), (kvw, F32), (kvw, F32), (2 * LANES, F32)], tm, seq)

            ng = seq // CMP_STRIDE
            cmp_tm = []
            for a, src in enumerate((kc, vc)):
                pe, w1, w2 = _compress_weights(nsa_cmp_pe[j][a], nsa_cmp_w1[j][a], nsa_cmp_w2[j][a])
                cmp_tm.append(_compress(src.reshape(batch, ng, CMP_STRIDE * kvw), pe, w1, w2))

            nsel = seq // SLC_BLOCK
            cs = np.arange(ng)[:, None] * CMP_STRIDE
            bs = np.arange(nsel)[None, :] * SLC_BLOCK
            ovl = ((cs < bs + SLC_BLOCK) & (cs + CMP_LEN > bs)).astype(np.float32)
            ovl[ng - 1, :] = 0.0
            ovl_t = jnp.asarray(ovl.T, BF)
            parts = _nsa_attention(pa, pf, cmp_tm[0], cmp_tm[1], ovl_t, batch, seq, 128, 512)
            w_out = nsa_w_out[j][perm, :]

        h = _ffn(h, g[4:6], ffn_w_in[i, 1].astype(BF), ffn_w_out[i, 1].astype(BF), tm,
                 parts=parts, wp=w_out.astype(BF), gp=g[3:4])
    return h.reshape(batch, seq, d)
```

```python
import functools
import math

import jax
import jax.numpy as jnp
import numpy as np
from jax import lax
from jax.experimental import pallas as pl
from jax.experimental.pallas import tpu as pltpu

D_MODEL = 1024
DEPTH = 4
HEAD_DIM = 64
ROT_DIM = HEAD_DIM // 4
ROPE_THETA = 500000.0
NORM_EPS = 1e-6
NEG_INF = -1e30
REMOVED = -3e38

DIFF_HEADS = 8
DIFF_SUBLN_EPS = 1e-5
MOBA_BLOCK = 256
MOBA_SHIFT = 8
MOBA_TOPK = 3
NSA_HEADS = 16
NSA_GROUPS = 4
CMP_LEN = 32
CMP_STRIDE = 16
CMP_HIDDEN = 256
SLC_BLOCK = 64
SLC_SHIFT = 6
SLC_TOPK = 16
WINDOW = 512
FORCED_SCORE = 1e9
D_FF = 2816

LANES = 128
FF_CHUNK = 256
FFN_ROWS = 512
CHAINS = 4
Q_SCALE = HEAD_DIM ** -0.5 * math.log2(math.e)

BF = jnp.bfloat16
F32 = jnp.float32
VMEM_LIMIT = 56 * 1024 * 1024

_NT = (((1,), (1,)), ((), ()))


def _cparams(sem):
    return pltpu.CompilerParams(dimension_semantics=sem, vmem_limit_bytes=VMEM_LIMIT)


def _rms(x, g, eps):
    return x * lax.rsqrt(jnp.mean(x * x, axis=-1, keepdims=True) + eps) * g


def _resident(shape):
    nd = len(shape)
    return pl.BlockSpec(shape, lambda *_: (0,) * nd, pipeline_mode=pl.Buffered(1))


def _ffn_kernel(*refs, n_parts):
    h_ref = refs[0]
    parts = refs[1:1 + n_parts]
    if n_parts:
        wp_ref, gp_ref = refs[1 + n_parts:3 + n_parts]
        refs = refs[3 + n_parts:]
    else:
        refs = refs[1:]
    g_ref, wi_ref, wo_ref, o_ref, acc_ref = refs
    for sub in range(h_ref.shape[0] // FFN_ROWS):
        rows = slice(sub * FFN_ROWS, (sub + 1) * FFN_ROWS)
        h = h_ref[rows, :]
        if n_parts:
            a = parts[0][rows, :]
            if n_parts > 1:
                a = a.astype(F32)
                for p in parts[1:]:
                    a = a + p[rows, :].astype(F32)
                a = a.astype(BF)
            h = h + _rms(jnp.dot(a, wp_ref[...], preferred_element_type=F32), gp_ref[...], NORM_EPS)
        xn = _rms(h, g_ref[0:1, :], NORM_EPS).astype(BF)
        for c in range(D_FF // FF_CHUNK):
            lo = c * FF_CHUNK
            gate = jnp.dot(xn, wi_ref[:, lo:lo + FF_CHUNK], preferred_element_type=F32)
            up = jnp.dot(xn, wi_ref[:, D_FF + lo:D_FF + lo + FF_CHUNK], preferred_element_type=F32)
            act = ((gate * (1.0 / (1.0 + jnp.exp(-gate)))) * up).astype(BF)
            part = jnp.dot(act, wo_ref[lo:lo + FF_CHUNK, :], preferred_element_type=F32)
            if c == 0:
                acc_ref[rows, :] = part
            else:
                acc_ref[rows, :] += part
        o_ref[rows, :] = h + 0.5 * _rms(acc_ref[rows, :], g_ref[1:2, :], NORM_EPS)


def _ffn(h, g2, wi, wo, tm, parts=(), wp=None, gp=None):
    n, d = h.shape
    row = pl.BlockSpec((tm, d), lambda i: (i, 0))
    proj_args = [wp, gp] if parts else []
    return pl.pallas_call(
        functools.partial(_ffn_kernel, n_parts=len(parts)),
        grid=(n // tm,),
        in_specs=([row] * (1 + len(parts)) + [_resident(a.shape) for a in proj_args]
                  + [_resident(g2.shape), _resident(wi.shape), _resident(wo.shape)]),
        out_specs=row,
        out_shape=jax.ShapeDtypeStruct((n, d), F32),
        scratch_shapes=[pltpu.VMEM((tm, d), F32)],
        compiler_params=_cparams(("parallel",)),
        name="ffn_halfstep",
    )(h, *parts, *proj_args, g2, wi, wo)


def _proj_kernel(h_ref, g_ref, w_ref, cos_ref, sa_ref, sb_ref, *out_refs, plan, tm, n_out):
    xn = _rms(h_ref[...], g_ref[...], NORM_EPS).astype(BF)
    cos = cos_ref[...]
    sa = sa_ref[...]
    sb = sb_ref[...]
    for col, width, rope, dest, off, km_off, scale in plan:
        y = jnp.dot(xn, w_ref[:, col:col + width], preferred_element_type=F32)
        for k in range(width // LANES):
            yk = y[:, k * LANES:(k + 1) * LANES]
            if rope:
                yk = yk * cos + pltpu.roll(yk, LANES - ROT_DIM // 2, 1) * sa + pltpu.roll(yk, ROT_DIM // 2, 1) * sb
            if scale is not None:
                yk = yk * scale
            o_ref = out_refs[dest]
            o_ref[:, off + k * LANES:off + (k + 1) * LANES] = yk.astype(o_ref.dtype)
            if km_off is not None:
                km_ref = out_refs[n_out]
                for r in range(tm // MOBA_BLOCK):
                    blk = yk[r * MOBA_BLOCK:(r + 1) * MOBA_BLOCK, :]
                    km_ref[0, r:r + 1, km_off + k * LANES:km_off + (k + 1) * LANES] = jnp.mean(
                        blk, axis=0, keepdims=True)


def _proj(h, g, w, tables, plan, outs, tm, seq, kmean_width=None):
    n, d = h.shape
    per_seq = seq // tm
    out_shape = [jax.ShapeDtypeStruct((n, wd), dt) for wd, dt in outs]
    out_specs = [pl.BlockSpec((tm, wd), lambda i: (i, 0)) for wd, _ in outs]
    if kmean_width is not None:
        nb = tm // MOBA_BLOCK
        out_shape.append(jax.ShapeDtypeStruct((n // tm, nb, kmean_width), F32))
        out_specs.append(pl.BlockSpec((1, nb, kmean_width), lambda i: (i, 0, 0)))
    tab_spec = pl.BlockSpec((tm, LANES), lambda i: (i % per_seq, 0))
    return pl.pallas_call(
        functools.partial(_proj_kernel, plan=tuple(plan), tm=tm, n_out=len(outs)),
        grid=(n // tm,),
        in_specs=[
            pl.BlockSpec((tm, d), lambda i: (i, 0)),
            _resident(g.shape),
            _resident(w.shape),
            tab_spec, tab_spec, tab_spec,
        ],
        out_specs=out_specs,
        out_shape=out_shape,
        compiler_params=_cparams(("parallel",)),
        name="norm_proj",
    )(h, g, w, *tables)


def _softmax_step(s, v, m_ref, l_ref, acc_ref):
    m_prev = m_ref[...]
    m_new = jnp.maximum(m_prev, jnp.max(s, axis=1, keepdims=True))
    alpha = jnp.exp2(m_prev - m_new)
    ps = [jnp.exp2(s[:, c * LANES:(c + 1) * LANES] - m_new) for c in range(s.shape[1] // LANES)]
    lsum = ps[0]
    for p in ps[1:]:
        lsum = lsum + p
    l_ref[...] = alpha * l_ref[...] + lsum
    p = jnp.concatenate([x.astype(BF) for x in ps], axis=1)
    acc_ref[...] = alpha * acc_ref[...] + jnp.dot(p, v, preferred_element_type=F32)
    m_ref[...] = m_new


def _init_stats(m_ref, l_ref, acc_ref):
    m_ref[...] = jnp.full(m_ref.shape, NEG_INF, F32)
    l_ref[...] = jnp.zeros(l_ref.shape, F32)
    acc_ref[...] = jnp.zeros(acc_ref.shape, F32)


def _normalised(l_ref, acc_ref):
    return acc_ref[...] * (1.0 / jnp.sum(l_ref[...], axis=1, keepdims=True))


def _stats_scratch(chains, rows):
    return [pltpu.VMEM((chains, rows, LANES), F32) for _ in range(3)]


def _split_halves(q):
    lane = lax.broadcasted_iota(jnp.int32, q.shape, 1)
    zero = jnp.zeros_like(q)
    return jnp.concatenate([jnp.where(lane < HEAD_DIM, q, zero), jnp.where(lane >= HEAD_DIM, q, zero)], axis=0)


def _lane_block(c):
    return slice(c * LANES, (c + 1) * LANES)


def _diff_kernel(q_ref, k_ref, v_ref, lam_ref, sg_ref, o_ref, m_ref, l_ref, acc_ref, *, t, lambda_init):
    qi = pl.program_id(2)
    qs = [_split_halves(q_ref[0, :, _lane_block(c)]) for c in range(CHAINS)]
    _init_stats(m_ref, l_ref, acc_ref)

    def tile(j, causal):
        start = pl.multiple_of(j * t, t)
        ss = [lax.dot_general(qs[c], k_ref[0, pl.ds(start, t), _lane_block(c)], _NT, preferred_element_type=F32)
              for c in range(CHAINS)]
        if causal:
            row = lax.broadcasted_iota(jnp.int32, (2 * t, t), 0) & (t - 1)
            col = lax.broadcasted_iota(jnp.int32, (2 * t, t), 1)
            ss = [jnp.where(col <= row, s, NEG_INF) for s in ss]
        for c in range(CHAINS):
            v = v_ref[0, pl.ds(start, t), _lane_block(c)]
            _softmax_step(ss[c], v, m_ref.at[c], l_ref.at[c], acc_ref.at[c])

    tile(qi, True)

    def body(j, carry):
        tile(j, False)
        return carry

    lax.fori_loop(0, qi, body, 0)

    lam = lam_ref[...]
    lam_full = (jnp.exp(jnp.sum(lam[0:1] * lam[1:2], axis=1, keepdims=True))
                - jnp.exp(jnp.sum(lam[2:3] * lam[3:4], axis=1, keepdims=True)) + lambda_init)
    for c in range(CHAINS):
        o = _normalised(l_ref.at[c], acc_ref.at[c])
        od = o[:t] - lam_full * o[t:]
        od = _rms(od, sg_ref[...], DIFF_SUBLN_EPS) * (1.0 - lambda_init)
        o_ref[0, :, _lane_block(c)] = od.astype(BF)


def _diff_attention(qkv, lam, subln, batch, seq, t, lambda_init):
    nh = DIFF_HEADS
    ns = nh // CHAINS
    w = CHAINS * LANES
    qkv3 = qkv.reshape(batch, seq, 3 * nh * LANES)
    out = pl.pallas_call(
        functools.partial(_diff_kernel, t=t, lambda_init=lambda_init),
        grid=(batch, ns, seq // t),
        in_specs=[
            pl.BlockSpec((1, t, w), lambda b, h, i: (b, i, h)),
            pl.BlockSpec((1, seq, w), lambda b, h, i: (b, 0, ns + h)),
            pl.BlockSpec((1, seq, w), lambda b, h, i: (b, 0, 2 * ns + h)),
            pl.BlockSpec(lam.shape, lambda b, h, i: (0, 0)),
            pl.BlockSpec(subln.shape, lambda b, h, i: (0, 0)),
        ],
        out_specs=pl.BlockSpec((1, t, w), lambda b, h, i: (b, i, h)),
        out_shape=jax.ShapeDtypeStruct((batch, seq, nh * LANES), BF),
        scratch_shapes=_stats_scratch(CHAINS, 2 * t),
        compiler_params=_cparams(("parallel", "parallel", "arbitrary")),
        name="diff_attention",
    )(qkv3, qkv3, qkv3, lam, subln)
    return out.reshape(batch * seq, nh * LANES)


def _moba_select(q2, km, qi, t, nb):
    km_hi = km.astype(BF)
    km_lo = (km - km_hi.astype(F32)).astype(BF)
    gate = (lax.dot_general(km_hi, q2, _NT, preferred_element_type=F32)
            + lax.dot_general(km_lo, q2, _NT, preferred_element_type=F32))
    blk = lax.broadcasted_iota(jnp.int32, gate.shape, 0)
    blk_f = blk.astype(F32)
    own = (qi * t + (lax.broadcasted_iota(jnp.int32, gate.shape, 1) & (t - 1))) >> MOBA_SHIFT
    gate = jnp.where(blk < own, gate, NEG_INF)
    sel = jnp.zeros(gate.shape, F32)
    for _ in range(min(MOBA_TOPK, nb)):
        mx = jnp.max(gate, axis=0, keepdims=True)
        idx = jnp.min(jnp.where(gate == mx, blk_f, float(nb)), axis=0, keepdims=True)
        hit = blk_f == idx
        sel = jnp.where(hit & (mx > 0.5 * NEG_INF), 1.0, sel)
        gate = jnp.where(hit, REMOVED, gate)
    bias_t = jnp.where((sel > 0.5) | (blk == own), 0.0, NEG_INF)
    bias_t = jnp.concatenate([bias_t, jnp.zeros((LANES - nb, 2 * t), F32)], axis=0)
    return jnp.concatenate([bias_t[:, _lane_block(n)].T for n in range(2 * t // LANES)], axis=0).astype(BF)


def _moba_kernel(q_ref, k_ref, v_ref, km_ref, ind_ref, o_ref, m_ref, l_ref, acc_ref, *, t, nb):
    qi = pl.program_id(2)
    _init_stats(m_ref, l_ref, acc_ref)
    q_aug = []
    for c in range(CHAINS):
        q2 = _split_halves(q_ref[0, :, _lane_block(c)])
        bias = _moba_select(q2, km_ref[0, :, _lane_block(c)], qi, t, nb)
        q_aug.append(jnp.concatenate([q2, bias], axis=1))

    def tile(j, diagonal):
        start = pl.multiple_of(j * t, t)
        ind = ind_ref[pl.ds(start, t), :]
        if diagonal:
            row = lax.broadcasted_iota(jnp.int32, (2 * t, t), 0) & (t - 1)
            col = lax.broadcasted_iota(jnp.int32, (2 * t, t), 1)
            future = ((row >> MOBA_SHIFT) == (col >> MOBA_SHIFT)) & (col > row)
        for c in range(CHAINS):
            k_aug = jnp.concatenate([k_ref[0, pl.ds(start, t), _lane_block(c)], ind], axis=1)
            v = v_ref[0, pl.ds(start, t), _lane_block(c)]
            s = lax.dot_general(q_aug[c], k_aug, _NT, preferred_element_type=F32)
            if diagonal:
                s = jnp.where(future, NEG_INF, s)
            _softmax_step(s, v, m_ref.at[c], l_ref.at[c], acc_ref.at[c])

    tile(qi, True)

    def body(j, carry):
        tile(j, False)
        return carry

    lax.fori_loop(0, qi, body, 0)

    lane = lax.broadcasted_iota(jnp.int32, (t, LANES), 1)
    for c in range(CHAINS):
        o = _normalised(l_ref.at[c], acc_ref.at[c])
        o_ref[0, :, _lane_block(c)] = jnp.where(lane < HEAD_DIM, o[:t], o[t:]).astype(BF)


def _moba_attention(qkv, kmean, batch, seq, t):
    nb = seq // MOBA_BLOCK
    ns = D_MODEL // LANES // CHAINS
    w = CHAINS * LANES
    qkv3 = qkv.reshape(batch, seq, 3 * D_MODEL)
    km3 = kmean.reshape(batch, nb, D_MODEL)
    ind = _block_indicator(seq, MOBA_SHIFT)
    out = pl.pallas_call(
        functools.partial(_moba_kernel, t=t, nb=nb),
        grid=(batch, ns, seq // t),
        in_specs=[
            pl.BlockSpec((1, t, w), lambda b, p, i: (b, i, p)),
            pl.BlockSpec((1, seq, w), lambda b, p, i: (b, 0, ns + p)),
            pl.BlockSpec((1, seq, w), lambda b, p, i: (b, 0, 2 * ns + p)),
            pl.BlockSpec((1, nb, w), lambda b, p, i: (b, 0, p)),
            pl.BlockSpec(ind.shape, lambda b, p, i: (0, 0)),
        ],
        out_specs=pl.BlockSpec((1, t, w), lambda b, p, i: (b, i, p)),
        out_shape=jax.ShapeDtypeStruct((batch, seq, D_MODEL), BF),
        scratch_shapes=_stats_scratch(CHAINS, 2 * t),
        compiler_params=_cparams(("parallel", "parallel", "arbitrary")),
        name="moba_attention",
    )(qkv3, qkv3, qkv3, km3, ind)
    return out.reshape(batch * seq, D_MODEL)


def _compress_kernel(r_ref, pe_ref, w1_ref, w2_ref, o_ref):
    r = r_ref[0]
    ng = r.shape[0]
    a0 = (r + pe_ref[0]).astype(BF)
    a1 = (r + pe_ref[1]).astype(BF)
    y0 = jnp.dot(a0, w1_ref[0], preferred_element_type=F32)
    y1 = jnp.dot(a1, w1_ref[1], preferred_element_type=F32)
    pre = y0 + pltpu.roll(y1, ng - 1, 0)
    hid = pre * (1.0 / (1.0 + jnp.exp(-pre)))
    o_ref[0] = jnp.dot(hid.astype(BF), w2_ref[...], preferred_element_type=F32).astype(o_ref.dtype)


def _compress(r, pe, w1, w2):
    batch, ng, wd = r.shape
    wout = w2.shape[1]
    return pl.pallas_call(
        _compress_kernel,
        grid=(batch,),
        in_specs=[
            pl.BlockSpec((1, ng, wd), lambda b: (b, 0, 0)),
            _resident(pe.shape),
            _resident(w1.shape),
            _resident(w2.shape),
        ],
        out_specs=pl.BlockSpec((1, ng, wout), lambda b: (b, 0, 0)),
        out_shape=jax.ShapeDtypeStruct((batch, ng, wout), BF),
        compiler_params=_cparams(("parallel",)),
        name="nsa_compress",
    )(r, pe, w1, w2)


def _compress_weights(pe, w1, w2):
    g = NSA_GROUPS
    eye = jnp.eye(g, dtype=F32)
    w1r = w1.reshape(2, CMP_STRIDE, HEAD_DIM, CMP_HIDDEN)
    w1b = jnp.einsum('tldh,gk->tlgdkh', w1r, eye).reshape(2, CMP_STRIDE * g * HEAD_DIM, g * CMP_HIDDEN)
    w2b = jnp.einsum('hd,gk->ghkd', w2, eye).reshape(g * CMP_HIDDEN, g * HEAD_DIM)
    peb = jnp.broadcast_to(pe.reshape(2, CMP_STRIDE, 1, HEAD_DIM), (2, CMP_STRIDE, g, HEAD_DIM))
    return peb.reshape(2, 1, CMP_STRIDE * g * HEAD_DIM), w1b.astype(BF), w2b.astype(BF)


def _nsa_stack_q(q, t):
    lane = lax.broadcasted_iota(jnp.int32, (t, LANES), 1)
    zero = jnp.zeros((t, LANES), q.dtype)
    parts = []
    for half in range(2):
        keep = (lane < HEAD_DIM) if half == 0 else (lane >= HEAD_DIM)
        for r in range(4):
            parts.append(jnp.where(keep, q[:, r * LANES:(r + 1) * LANES], zero))
    return jnp.concatenate(parts, axis=0)


def _nsa_write(o, gl, branch, o_ref, t, gp=0):
    lane = lax.broadcasted_iota(jnp.int32, (t, LANES), 1)
    for r in range(4):
        c0 = r * 3 + branch
        c1 = 12 + r * 3 + branch
        g0 = 1.0 / (1.0 + jnp.exp(-gl[:, c0:c0 + 1]))
        g1 = 1.0 / (1.0 + jnp.exp(-gl[:, c1:c1 + 1]))
        blk = jnp.where(lane < HEAD_DIM, g0 * o[r * t:(r + 1) * t], g1 * o[(4 + r) * t:(5 + r) * t])
        o_ref[0, :, _lane_block(4 * gp + r)] = blk.astype(BF)


def _nsa_cmp_kernel(q_ref, kc_ref, vc_ref, gl_ref, ovl_ref, o_ref, sel_ref, *, t, ncmp):
    qi = pl.program_id(2)
    qs = _nsa_stack_q(q_ref[0], t)
    s = lax.dot_general(qs, kc_ref[0], _NT, preferred_element_type=F32)
    n_idx = lax.broadcasted_iota(jnp.int32, s.shape, 1)
    qidx = qi * t + (lax.broadcasted_iota(jnp.int32, s.shape, 0) & (t - 1))
    mask = (n_idx * CMP_STRIDE + (CMP_LEN - 1)) <= qidx
    sm = jnp.where(mask, s, NEG_INF)
    p = jnp.where(mask, jnp.exp2(sm - jnp.max(sm, axis=1, keepdims=True)), 0.0)
    l = jnp.sum(p, axis=1, keepdims=True)
    pn = p * (1.0 / jnp.where(l > 0.0, l, 1.0))
    o = jnp.dot(pn.astype(BF), vc_ref[0], preferred_element_type=F32)
    _nsa_write(o, gl_ref[0], 0, o_ref, t)

    nsel = ovl_ref.shape[0]
    jb = lax.broadcasted_iota(jnp.int32, (nsel, t), 0)
    qblk =(qi * t + lax.broadcasted_iota(jnp.int32, (nsel, t), 1)) >> SLC_SHIFT
    forced = (jb == 0) | (jb == qblk) | (jb == qblk - 1)
    ovl = ovl_ref[...]
    for half in range(2):
        ps = pn[(half * 4) * t:(half * 4 + 1) * t]
        for r in range(1, 4):
            ps = ps + pn[(half * 4 + r) * t:(half * 4 + r + 1) * t]
        ps_hi = ps.astype(BF)
        ps_lo = (ps - ps_hi.astype(F32)).astype(BF)
        imp = (lax.dot_general(ovl, ps_hi, _NT, preferred_element_type=F32)
               + lax.dot_general(ovl, ps_lo, _NT, preferred_element_type=F32))
        val = jnp.where(jb <= qblk, jnp.where(forced, FORCED_SCORE, imp), NEG_INF)
        slabs = [val[g * 8:(g + 1) * 8] for g in range(nsel // 8)]
        row8 = lax.broadcasted_iota(jnp.int32, (8, t), 0)
        cnt = [jnp.zeros((8, t), F32) for _ in slabs]
        for i in range(nsel):
            vi = val[i:i + 1, :]
            for g, sl in enumerate(slabs):
                if g > i // 8:
                    ahead = vi >= sl
                elif g < i // 8:
                    ahead = vi > sl
                else:
                    ahead = (vi > sl) | ((vi == sl) & (row8 > (i % 8)))
                cnt[g] = jnp.where(ahead, cnt[g] + 1.0, cnt[g])
        rank = jnp.concatenate(cnt, axis=0)
        chosen = jnp.where((rank < float(min(SLC_TOPK, nsel))) & (jb <= qblk), 1.0, 0.0)
        pad = jnp.zeros((LANES - nsel, t), F32)
        sel_ref[0, 0, half] = jnp.concatenate([chosen, pad], axis=0).T.astype(BF)


def _nsa_sel_kernel(q_ref, k_ref, v_ref, gl_ref, sel_ref, ind_ref, o_ref, m_ref, l_ref, acc_ref, *, t, tk):
    qi = pl.program_id(1)
    q_aug = []
    for gp in range(2):
        qs = _nsa_stack_q(q_ref[0, :, gp * 4 * LANES:(gp + 1) * 4 * LANES], t)
        for half in range(2):
            bias = jnp.where(sel_ref[0, gp, half].astype(F32) > 0.5, 0.0, NEG_INF).astype(BF)
            q_aug.append(jnp.concatenate([qs[half * 4 * t:(half + 1) * 4 * t],
                                          jnp.concatenate([bias] * 4, axis=0)], axis=1))
    _init_stats(m_ref, l_ref, acc_ref)
    jd = (qi * t) // tk

    def tile(j, causal):
        start = pl.multiple_of(j * tk, tk)
        ind = ind_ref[pl.ds(start, tk), :]
        if causal:
            qidx = qi * t + (lax.broadcasted_iota(jnp.int32, (4 * t, tk), 0) & (t - 1))
            kidx = j * tk + lax.broadcasted_iota(jnp.int32, (4 * t, tk), 1)
        ss = []
        for gp in range(2):
            k_aug = jnp.concatenate([k_ref[0, pl.ds(start, tk), _lane_block(gp)], ind], axis=1)
            for half in range(2):
                s = lax.dot_general(q_aug[2 * gp + half], k_aug, _NT, preferred_element_type=F32)
                ss.append(jnp.where(kidx <= qidx, s, NEG_INF) if causal else s)
        for c in range(4):
            v = v_ref[0, pl.ds(start, tk), _lane_block(c // 2)]
            _softmax_step(ss[c], v, m_ref.at[c], l_ref.at[c], acc_ref.at[c])

    tile(jd, True)

    def body(j, carry):
        tile(j, False)
        return carry

    lax.fori_loop(0, jd, body, 0)
    for gp in range(2):
        o = jnp.concatenate([_normalised(l_ref.at[2 * gp + half], acc_ref.at[2 * gp + half]) for half in range(2)],
                            axis=0)
        _nsa_write(o, gl_ref[0, :, _lane_block(gp)], 1, o_ref, t, gp)


def _nsa_win_kernel(q_ref, k_ref, v_ref, gl_ref, o_ref, m_ref, l_ref, acc_ref, *, t):
    qi = pl.program_id(1)
    qs = [_nsa_stack_q(q_ref[0, :, gp * 4 * LANES:(gp + 1) * 4 * LANES], t) for gp in range(2)]
    _init_stats(m_ref, l_ref, acc_ref)

    def tile(j, kind):
        start = pl.multiple_of(j * t, t)
        if kind != "full":
            row = lax.broadcasted_iota(jnp.int32, (8 * t, t), 0) & (t - 1)
            col = lax.broadcasted_iota(jnp.int32, (8 * t, t), 1)
            keep = (col <= row) if kind == "causal" else (col > row)
        for gp in range(2):
            k = k_ref[0, pl.ds(start, t), _lane_block(gp)]
            v = v_ref[0, pl.ds(start, t), _lane_block(gp)]
            s = lax.dot_general(qs[gp], k, _NT, preferred_element_type=F32)
            if kind != "full":
                s = jnp.where(keep, s, NEG_INF)
            _softmax_step(s, v, m_ref.at[gp], l_ref.at[gp], acc_ref.at[gp])

    tile(qi, "causal")

    @pl.when(qi >= 1)
    def _():
        tile(qi - 1, "full")

    @pl.when(qi >= 2)
    def _():
        tile(qi - 2, "tail")

    for gp in range(2):
        _nsa_write(_normalised(l_ref.at[gp], acc_ref.at[gp]), gl_ref[0, :, _lane_block(gp)], 2, o_ref, t, gp)


def _nsa_attention(pa, pf, kcmp, vcmp, ovl_t, batch, seq, t, tk):
    pa3 = pa.reshape(batch, seq, pa.shape[1])
    pf3 = pf.reshape(batch, seq, pf.shape[1])
    grid = (batch, 2, seq // t)
    ncmp = kcmp.shape[1]
    sem = _cparams(("parallel", "parallel", "arbitrary"))
    o_shape = jax.ShapeDtypeStruct((batch, seq, D_MODEL), BF)

    def q_specs(rows):
        q_spec = pl.BlockSpec((1, rows, 4 * LANES), lambda b, g, i: (b, i, g))
        gl_spec = pl.BlockSpec((1, rows, LANES), lambda b, g, i: (b, i, g))
        return q_spec, gl_spec, q_spec

    q_spec, gl_spec, o_spec = q_specs(t)
    ind = _block_indicator(seq, SLC_SHIFT)

    def kv_spec(col_block):
        return pl.BlockSpec((1, seq, LANES), lambda b, g, i: (b, 0, col_block + g))

    o_c, sel = pl.pallas_call(
        functools.partial(_nsa_cmp_kernel, t=t, ncmp=ncmp),
        grid=grid,
        in_specs=[
            q_spec,
            pl.BlockSpec((1, ncmp, LANES), lambda b, g, i: (b, 0, g)),
            pl.BlockSpec((1, ncmp, LANES), lambda b, g, i: (b, 0, g)),
            gl_spec,
            pl.BlockSpec(ovl_t.shape, lambda b, g, i: (0, 0)),
        ],
        out_specs=[o_spec, pl.BlockSpec((1, 1, 2, t, LANES), lambda b, g, i: (b, g, 0, i, 0))],
        out_shape=[o_shape, jax.ShapeDtypeStruct((batch, 2, 2, seq, LANES), BF)],
        compiler_params=sem,
        name="nsa_compressed",
    )(pa3, kcmp, vcmp, pf3, ovl_t)

    sem2 = _cparams(("parallel", "arbitrary"))
    kvw = 2 * LANES

    def row_spec(rows):
        return pl.BlockSpec((1, rows, D_MODEL), lambda b, i: (b, i, 0))

    def kv2_spec(col_block):
        return pl.BlockSpec((1, seq, kvw), lambda b, i: (b, 0, col_block))

    def gl2_spec(rows):
        return pl.BlockSpec((1, rows, kvw), lambda b, i: (b, i, 0))

    ts = 2 * t
    o_s = pl.pallas_call(
        functools.partial(_nsa_sel_kernel, t=ts, tk=tk),
        grid=(batch, seq // ts),
        in_specs=[row_spec(ts), kv2_spec(4), kv2_spec(6), gl2_spec(ts),
                  pl.BlockSpec((1, 2, 2, ts, LANES), lambda b, i: (b, 0, 0, i, 0)),
                  pl.BlockSpec(ind.shape, lambda b, i: (0, 0))],
        out_specs=row_spec(ts),
        out_shape=o_shape,
        scratch_shapes=_stats_scratch(4, 4 * ts),
        compiler_params=sem2,
        name="nsa_selected",
    )(pa3, pa3, pa3, pf3, sel, ind)

    tw = WINDOW // 2
    o_w = pl.pallas_call(
        functools.partial(_nsa_win_kernel, t=tw),
        grid=(batch, seq // tw),
        in_specs=[row_spec(tw), kv2_spec(5), kv2_spec(7), gl2_spec(tw)],
        out_specs=row_spec(tw),
        out_shape=o_shape,
        scratch_shapes=_stats_scratch(2, 8 * tw),
        compiler_params=sem2,
        name="nsa_window",
    )(pa3, pa3, pa3, pf3)
    n = batch * seq
    return [o_c.reshape(n, D_MODEL), o_s.reshape(n, D_MODEL), o_w.reshape(n, D_MODEL)]


def _rope_tables(seq):
    half = ROT_DIM // 2
    inv_freq = ROPE_THETA ** (-jnp.arange(half, dtype=F32) / half)
    ang = jnp.arange(seq).astype(F32)[:, None] * inv_freq[None, :]
    cos, sin = jnp.cos(ang), jnp.sin(ang)
    d = np.arange(LANES) % HEAD_DIM
    idx = d % half
    cos_t = jnp.where(d[None, :] < ROT_DIM, cos[:, idx], 1.0)
    sa_t = jnp.where(d[None, :] < half, -sin[:, idx], 0.0)
    sb_t = jnp.where((d[None, :] >= half) & (d[None, :] < ROT_DIM), sin[:, idx], 0.0)
    return cos_t.astype(F32), sa_t.astype(F32), sb_t.astype(F32)


def _block_indicator(seq, shift):
    blk = np.arange(seq)[:, None] >> shift
    return jnp.asarray((blk == np.arange(LANES)[None, :]).astype(np.float32), BF)


def _nsa_q_perm():
    cols = []
    for gp in range(2):
        for r in range(4):
            for half in range(2):
                head = 8 * gp + 4 * half + r
                cols.extend(range(head * HEAD_DIM, (head + 1) * HEAD_DIM))
    return np.asarray(cols, np.int32)


def _nsa_gate_cols():
    src = -np.ones(2 * LANES, np.int32)
    for gp in range(2):
        for half in range(2):
            for r in range(4):
                for br in range(3):
                    src[gp * LANES + half * 12 + r * 3 + br] = (4 * (2 * gp + half) + r) * 3 + br
    return src


def kernel(x, norm_g, ffn_w_in, ffn_w_out, diff_w_in, diff_w_out, diff_lambda, diff_subln, moba_w_in, moba_w_out,
           nsa_w_in, nsa_w_out, nsa_cmp_pe, nsa_cmp_w1, nsa_cmp_w2):
    batch, seq, d = x.shape
    n = batch * seq
    tm = 512
    h = x.reshape(n, d)
    tables = _rope_tables(seq)

    for i in range(DEPTH):
        g = norm_g[i]
        h = _ffn(h, g[0:2], ffn_w_in[i, 0].astype(BF), ffn_w_out[i, 0].astype(BF), 2 * FFN_ROWS)

        kind, j = i % 3, i // 3
        if kind == 0:
            lambda_init = 0.8 - 0.6 * math.exp(-0.3 * i)
            plan = [(c * 256, 256, c < 8, 0, c * 256, None, Q_SCALE if c < 4 else None) for c in range(12)]
            (qkv,) = _proj(h, g[2:3], diff_w_in[j].astype(BF), tables, plan, [(3 * D_MODEL, BF)], tm, seq)
            attn = _diff_attention(qkv, diff_lambda[j], diff_subln[j].reshape(1, LANES), batch, seq, 512,
                                   lambda_init)
            parts, w_out = [attn], diff_w_out[j]
        elif kind == 1:
            plan = [(c * 256, 256, c < 8, 0, c * 256, (c - 4) * 256 if 4 <= c < 8 else None,
                     Q_SCALE if c < 4 else None) for c in range(12)]
            qkv, kmean = _proj(h, g[2:3], moba_w_in[j].astype(BF), tables, plan, [(3 * D_MODEL, BF)], tm, seq,
                               kmean_width=D_MODEL)
            attn = _moba_attention(qkv, kmean, batch, seq, 512)
            parts, w_out = [attn], moba_w_out[j]
        else:
            w = nsa_w_in[j]
            perm = _nsa_q_perm()
            kvw = NSA_GROUPS * HEAD_DIM
            base = NSA_HEADS * HEAD_DIM
            seg = {name: w[:, base + k * kvw: base + (k + 1) * kvw]
                   for k, name in enumerate(["kc", "vc", "ks", "vs", "kw", "vw"])}
            gsrc = _nsa_gate_cols()
            glog = w[:, base + 6 * kvw:]
            gate_w = jnp.where(gsrc[None, :] >= 0, glog[:, np.maximum(gsrc, 0)], 0.0)
            w_all = jnp.concatenate([w[:, perm], seg["ks"], seg["kw"], seg["kc"], seg["vs"], seg["vw"], seg["vc"],
                                     gate_w], axis=1).astype(BF)
            plan = [(c * 256, 256, True, 0, c * 256, None, Q_SCALE if c < 4 else None)
                    for c in range(6)]
            plan.append((1536, 256, True, 1, 0, None, None))
            plan.append((1792, 256, False, 0, 1536, None, None))
            plan.append((2048, 256, False, 0, 1792, None, None))
            plan.append((2304, 256, False, 2, 0, None, None))
            plan.append((2560, 256, False, 3, 0, None, None))
            pa, kc, vc, pf = _proj(h, g[2:3], w_all, tables, plan,
                                   [(2048, BF), (kvw, F32), (kvw, F32), (2 * LANES, F32)], tm, seq)

            ng = seq // CMP_STRIDE
            cmp_tm = []
            for a, src in enumerate((kc, vc)):
                pe, w1, w2 = _compress_weights(nsa_cmp_pe[j][a], nsa_cmp_w1[j][a], nsa_cmp_w2[j][a])
                cmp_tm.append(_compress(src.reshape(batch, ng, CMP_STRIDE * kvw), pe, w1, w2))

            nsel = seq // SLC_BLOCK
            cs = np.arange(ng)[:, None] * CMP_STRIDE
            bs = np.arange(nsel)[None, :] * SLC_BLOCK
            ovl = ((cs < bs + SLC_BLOCK) & (cs + CMP_LEN > bs)).astype(np.float32)
            ovl[ng - 1, :] = 0.0
            ovl_t = jnp.asarray(ovl.T, BF)
            parts = _nsa_attention(pa, pf, cmp_tm[0], cmp_tm[1], ovl_t, batch, seq, 128, 512)
            w_out = nsa_w_out[j][perm, :]

        h = _ffn(h, g[4:6], ffn_w_in[i, 1].astype(BF), ffn_w_out[i, 1].astype(BF),
                 (2 if len(parts) == 1 else 1) * FFN_ROWS, parts=parts, wp=w_out.astype(BF), gp=g[3:4])
    return h.reshape(batch, seq, d)
```

```python
import functools
import math

import jax
import jax.numpy as jnp
import numpy as np
from jax import lax
from jax.experimental import pallas as pl
from jax.experimental.pallas import tpu as pltpu

D_MODEL = 1024
DEPTH = 4
HEAD_DIM = 64
ROT_DIM = HEAD_DIM // 4
ROPE_THETA = 500000.0
NORM_EPS = 1e-6
NEG_INF = -1e30
REMOVED = -3e38

DIFF_HEADS = 8
DIFF_SUBLN_EPS = 1e-5
MOBA_BLOCK = 256
MOBA_SHIFT = 8
MOBA_TOPK = 3
NSA_HEADS = 16
NSA_GROUPS = 4
CMP_LEN = 32
CMP_STRIDE = 16
CMP_HIDDEN = 256
SLC_BLOCK = 64
SLC_SHIFT = 6
SLC_TOPK = 16
WINDOW = 512
FORCED_SCORE = 1e9
D_FF = 2816

LANES = 128
FF_CHUNK = 256
CHAINS = 4
Q_SCALE = HEAD_DIM ** -0.5 * math.log2(math.e)

BF = jnp.bfloat16
F32 = jnp.float32
VMEM_LIMIT = 56 * 1024 * 1024

_NT = (((1,), (1,)), ((), ()))


def _cparams(sem):
    return pltpu.CompilerParams(dimension_semantics=sem, vmem_limit_bytes=VMEM_LIMIT)


def _rms(x, g, eps):
    return x * lax.rsqrt(jnp.mean(x * x, axis=-1, keepdims=True) + eps) * g


def _resident(shape):
    nd = len(shape)
    return pl.BlockSpec(shape, lambda *_: (0,) * nd, pipeline_mode=pl.Buffered(1))


def _ffn_kernel(*refs, n_parts):
    h_ref = refs[0]
    parts = refs[1:1 + n_parts]
    if n_parts:
        wp_ref, gp_ref = refs[1 + n_parts:3 + n_parts]
        refs = refs[3 + n_parts:]
    else:
        refs = refs[1:]
    g_ref, wi_ref, wo_ref, o_ref, acc_ref = refs
    h = h_ref[...]
    if n_parts:
        a = parts[0][...]
        if n_parts > 1:
            a = a.astype(F32)
            for p in parts[1:]:
                a = a + p[...].astype(F32)
            a = a.astype(BF)
        h = h + _rms(jnp.dot(a, wp_ref[...], preferred_element_type=F32), gp_ref[...], NORM_EPS)
    xn = _rms(h, g_ref[0:1, :], NORM_EPS).astype(BF)
    for c in range(D_FF // FF_CHUNK):
        lo = c * FF_CHUNK
        gate = jnp.dot(xn, wi_ref[:, lo:lo + FF_CHUNK], preferred_element_type=F32)
        up = jnp.dot(xn, wi_ref[:, D_FF + lo:D_FF + lo + FF_CHUNK], preferred_element_type=F32)
        act = ((gate * (1.0 / (1.0 + jnp.exp(-gate)))) * up).astype(BF)
        part = jnp.dot(act, wo_ref[lo:lo + FF_CHUNK, :], preferred_element_type=F32)
        if c == 0:
            acc_ref[...] = part
        else:
            acc_ref[...] += part
    o_ref[...] = h + 0.5 * _rms(acc_ref[...], g_ref[1:2, :], NORM_EPS)


def _ffn(h, g2, wi_all, wo_all, layer, which, tm, parts=(), wp=None, gp=None):
    n, d = h.shape
    row = pl.BlockSpec((tm, d), lambda i: (i, 0))
    proj_args = [wp, gp] if parts else []

    def picked(w):
        return pl.BlockSpec((None, None) + w.shape[2:], lambda i: (layer, which, 0, 0), pipeline_mode=pl.Buffered(1))

    return pl.pallas_call(
        functools.partial(_ffn_kernel, n_parts=len(parts)),
        grid=(n // tm,),
        in_specs=([row] * (1 + len(parts)) + [_resident(a.shape) for a in proj_args]
                  + [_resident(g2.shape), picked(wi_all), picked(wo_all)]),
        out_specs=row,
        out_shape=jax.ShapeDtypeStruct((n, d), F32),
        scratch_shapes=[pltpu.VMEM((tm, d), F32)],
        compiler_params=_cparams(("parallel",)),
        name="ffn_halfstep",
    )(h, *parts, *proj_args, g2, wi_all, wo_all)


def _proj_kernel(h_ref, g_ref, w_ref, cos_ref, sa_ref, sb_ref, *out_refs, plan, tm, n_out):
    xn = _rms(h_ref[...], g_ref[...], NORM_EPS).astype(BF)
    cos = cos_ref[...]
    sa = sa_ref[...]
    sb = sb_ref[...]
    for col, width, rope, dest, off, km_off, scale in plan:
        y = jnp.dot(xn, w_ref[:, col:col + width], preferred_element_type=F32)
        for k in range(width // LANES):
            yk = y[:, k * LANES:(k + 1) * LANES]
            if rope:
                yk = yk * cos + pltpu.roll(yk, LANES - ROT_DIM // 2, 1) * sa + pltpu.roll(yk, ROT_DIM // 2, 1) * sb
            if scale is not None:
                yk = yk * scale
            o_ref = out_refs[dest]
            o_ref[:, off + k * LANES:off + (k + 1) * LANES] = yk.astype(o_ref.dtype)
            if km_off is not None:
                km_ref = out_refs[n_out]
                for r in range(tm // MOBA_BLOCK):
                    blk = yk[r * MOBA_BLOCK:(r + 1) * MOBA_BLOCK, :]
                    km_ref[0, r:r + 1, km_off + k * LANES:km_off + (k + 1) * LANES] = jnp.mean(
                        blk, axis=0, keepdims=True)


def _proj(h, g, w, tables, plan, outs, tm, seq, kmean_width=None):
    n, d = h.shape
    per_seq = seq // tm
    out_shape = [jax.ShapeDtypeStruct((n, wd), dt) for wd, dt in outs]
    out_specs = [pl.BlockSpec((tm, wd), lambda i: (i, 0)) for wd, _ in outs]
    if kmean_width is not None:
        nb = tm // MOBA_BLOCK
        out_shape.append(jax.ShapeDtypeStruct((n // tm, nb, kmean_width), F32))
        out_specs.append(pl.BlockSpec((1, nb, kmean_width), lambda i: (i, 0, 0)))
    tab_spec = pl.BlockSpec((tm, LANES), lambda i: (i % per_seq, 0))
    return pl.pallas_call(
        functools.partial(_proj_kernel, plan=tuple(plan), tm=tm, n_out=len(outs)),
        grid=(n // tm,),
        in_specs=[
            pl.BlockSpec((tm, d), lambda i: (i, 0)),
            _resident(g.shape),
            _resident(w.shape),
            tab_spec, tab_spec, tab_spec,
        ],
        out_specs=out_specs,
        out_shape=out_shape,
        compiler_params=_cparams(("parallel",)),
        name="norm_proj",
    )(h, g, w, *tables)


def _softmax_step(s, v, m_ref, l_ref, acc_ref):
    m_prev = m_ref[...]
    m_new = jnp.maximum(m_prev, jnp.max(s, axis=1, keepdims=True))
    alpha = jnp.exp2(m_prev - m_new)
    ps = [jnp.exp2(s[:, c * LANES:(c + 1) * LANES] - m_new) for c in range(s.shape[1] // LANES)]
    lsum = ps[0]
    for p in ps[1:]:
        lsum = lsum + p
    l_ref[...] = alpha * l_ref[...] + lsum
    p = jnp.concatenate([x.astype(BF) for x in ps], axis=1)
    acc_ref[...] = alpha * acc_ref[...] + jnp.dot(p, v, preferred_element_type=F32)
    m_ref[...] = m_new


def _init_stats(m_ref, l_ref, acc_ref):
    m_ref[...] = jnp.full(m_ref.shape, NEG_INF, F32)
    l_ref[...] = jnp.zeros(l_ref.shape, F32)
    acc_ref[...] = jnp.zeros(acc_ref.shape, F32)


def _normalised(l_ref, acc_ref):
    return acc_ref[...] * (1.0 / jnp.sum(l_ref[...], axis=1, keepdims=True))


def _stats_scratch(chains, rows):
    return [pltpu.VMEM((chains, rows, LANES), F32) for _ in range(3)]


def _split_halves(q):
    lane = lax.broadcasted_iota(jnp.int32, q.shape, 1)
    zero = jnp.zeros_like(q)
    return jnp.concatenate([jnp.where(lane < HEAD_DIM, q, zero), jnp.where(lane >= HEAD_DIM, q, zero)], axis=0)


def _lane_block(c):
    return slice(c * LANES, (c + 1) * LANES)


def _diff_kernel(q_ref, k_ref, v_ref, lam_ref, sg_ref, o_ref, m_ref, l_ref, acc_ref, *, t, lambda_init):
    qi = pl.program_id(2)
    qs = [_split_halves(q_ref[0, :, _lane_block(c)]) for c in range(CHAINS)]
    _init_stats(m_ref, l_ref, acc_ref)

    def tile(j, causal):
        start = pl.multiple_of(j * t, t)
        ss = [lax.dot_general(qs[c], k_ref[0, pl.ds(start, t), _lane_block(c)], _NT, preferred_element_type=F32)
              for c in range(CHAINS)]
        if causal:
            row = lax.broadcasted_iota(jnp.int32, (2 * t, t), 0) & (t - 1)
            col = lax.broadcasted_iota(jnp.int32, (2 * t, t), 1)
            ss = [jnp.where(col <= row, s, NEG_INF) for s in ss]
        for c in range(CHAINS):
            v = v_ref[0, pl.ds(start, t), _lane_block(c)]
            _softmax_step(ss[c], v, m_ref.at[c], l_ref.at[c], acc_ref.at[c])

    tile(qi, True)

    def body(j, carry):
        tile(j, False)
        return carry

    lax.fori_loop(0, qi, body, 0)

    lam = lam_ref[...]
    lam_full = (jnp.exp(jnp.sum(lam[0:1] * lam[1:2], axis=1, keepdims=True))
                - jnp.exp(jnp.sum(lam[2:3] * lam[3:4], axis=1, keepdims=True)) + lambda_init)
    for c in range(CHAINS):
        o = _normalised(l_ref.at[c], acc_ref.at[c])
        od = o[:t] - lam_full * o[t:]
        od = _rms(od, sg_ref[...], DIFF_SUBLN_EPS) * (1.0 - lambda_init)
        o_ref[0, :, _lane_block(c)] = od.astype(BF)


def _diff_attention(qkv, lam, subln, batch, seq, t, lambda_init):
    nh = DIFF_HEADS
    ns = nh // CHAINS
    w = CHAINS * LANES
    qkv3 = qkv.reshape(batch, seq, 3 * nh * LANES)
    out = pl.pallas_call(
        functools.partial(_diff_kernel, t=t, lambda_init=lambda_init),
        grid=(batch, ns, seq // t),
        in_specs=[
            pl.BlockSpec((1, t, w), lambda b, h, i: (b, i, h)),
            pl.BlockSpec((1, seq, w), lambda b, h, i: (b, 0, ns + h)),
            pl.BlockSpec((1, seq, w), lambda b, h, i: (b, 0, 2 * ns + h)),
            pl.BlockSpec(lam.shape, lambda b, h, i: (0, 0)),
            pl.BlockSpec(subln.shape, lambda b, h, i: (0, 0)),
        ],
        out_specs=pl.BlockSpec((1, t, w), lambda b, h, i: (b, i, h)),
        out_shape=jax.ShapeDtypeStruct((batch, seq, nh * LANES), BF),
        scratch_shapes=_stats_scratch(CHAINS, 2 * t),
        compiler_params=_cparams(("parallel", "parallel", "arbitrary")),
        name="diff_attention",
    )(qkv3, qkv3, qkv3, lam, subln)
    return out.reshape(batch * seq, nh * LANES)


def _moba_select(q2, km, qi, t, nb):
    km_hi = km.astype(BF)
    km_lo = (km - km_hi.astype(F32)).astype(BF)
    gate = (lax.dot_general(km_hi, q2, _NT, preferred_element_type=F32)
            + lax.dot_general(km_lo, q2, _NT, preferred_element_type=F32))
    blk = lax.broadcasted_iota(jnp.int32, gate.shape, 0)
    blk_f = blk.astype(F32)
    own = (qi * t + (lax.broadcasted_iota(jnp.int32, gate.shape, 1) & (t - 1))) >> MOBA_SHIFT
    gate = jnp.where(blk < own, gate, NEG_INF)
    sel = jnp.zeros(gate.shape, F32)
    for _ in range(min(MOBA_TOPK, nb)):
        mx = jnp.max(gate, axis=0, keepdims=True)
        idx = jnp.min(jnp.where(gate == mx, blk_f, float(nb)), axis=0, keepdims=True)
        hit = blk_f == idx
        sel = jnp.where(hit & (mx > 0.5 * NEG_INF), 1.0, sel)
        gate = jnp.where(hit, REMOVED, gate)
    bias_t = jnp.where((sel > 0.5) | (blk == own), 0.0, NEG_INF)
    bias_t = jnp.concatenate([bias_t, jnp.zeros((LANES - nb, 2 * t), F32)], axis=0)
    return jnp.concatenate([bias_t[:, _lane_block(n)].T for n in range(2 * t // LANES)], axis=0).astype(BF)


def _moba_kernel(q_ref, k_ref, v_ref, km_ref, ind_ref, o_ref, m_ref, l_ref, acc_ref, *, t, nb):
    qi = pl.program_id(2)
    _init_stats(m_ref, l_ref, acc_ref)
    q_aug = []
    for c in range(CHAINS):
        q2 = _split_halves(q_ref[0, :, _lane_block(c)])
        bias = _moba_select(q2, km_ref[0, :, _lane_block(c)], qi, t, nb)
        q_aug.append(jnp.concatenate([q2, bias], axis=1))

    def tile(j, diagonal):
        start = pl.multiple_of(j * t, t)
        ind = ind_ref[pl.ds(start, t), :]
        if diagonal:
            row = lax.broadcasted_iota(jnp.int32, (2 * t, t), 0) & (t - 1)
            col = lax.broadcasted_iota(jnp.int32, (2 * t, t), 1)
            future = ((row >> MOBA_SHIFT) == (col >> MOBA_SHIFT)) & (col > row)
        for c in range(CHAINS):
            k_aug = jnp.concatenate([k_ref[0, pl.ds(start, t), _lane_block(c)], ind], axis=1)
            v = v_ref[0, pl.ds(start, t), _lane_block(c)]
            s = lax.dot_general(q_aug[c], k_aug, _NT, preferred_element_type=F32)
            if diagonal:
                s = jnp.where(future, NEG_INF, s)
            _softmax_step(s, v, m_ref.at[c], l_ref.at[c], acc_ref.at[c])

    tile(qi, True)

    def body(j, carry):
        tile(j, False)
        return carry

    lax.fori_loop(0, qi, body, 0)

    lane = lax.broadcasted_iota(jnp.int32, (t, LANES), 1)
    for c in range(CHAINS):
        o = _normalised(l_ref.at[c], acc_ref.at[c])
        o_ref[0, :, _lane_block(c)] = jnp.where(lane < HEAD_DIM, o[:t], o[t:]).astype(BF)


def _moba_attention(qkv, kmean, batch, seq, t):
    nb = seq // MOBA_BLOCK
    ns = D_MODEL // LANES // CHAINS
    w = CHAINS * LANES
    qkv3 = qkv.reshape(batch, seq, 3 * D_MODEL)
    km3 = kmean.reshape(batch, nb, D_MODEL)
    ind = _block_indicator(seq, MOBA_SHIFT)
    out = pl.pallas_call(
        functools.partial(_moba_kernel, t=t, nb=nb),
        grid=(batch, ns, seq // t),
        in_specs=[
            pl.BlockSpec((1, t, w), lambda b, p, i: (b, i, p)),
            pl.BlockSpec((1, seq, w), lambda b, p, i: (b, 0, ns + p)),
            pl.BlockSpec((1, seq, w), lambda b, p, i: (b, 0, 2 * ns + p)),
            pl.BlockSpec((1, nb, w), lambda b, p, i: (b, 0, p)),
            pl.BlockSpec(ind.shape, lambda b, p, i: (0, 0)),
        ],
        out_specs=pl.BlockSpec((1, t, w), lambda b, p, i: (b, i, p)),
        out_shape=jax.ShapeDtypeStruct((batch, seq, D_MODEL), BF),
        scratch_shapes=_stats_scratch(CHAINS, 2 * t),
        compiler_params=_cparams(("parallel", "parallel", "arbitrary")),
        name="moba_attention",
    )(qkv3, qkv3, qkv3, km3, ind)
    return out.reshape(batch * seq, D_MODEL)


def _compress_kernel(r_ref, pe_ref, w1_ref, w2_ref, o_ref):
    r = r_ref[0]
    ng = r.shape[0]
    a0 = (r + pe_ref[0]).astype(BF)
    a1 = (r + pe_ref[1]).astype(BF)
    y0 = jnp.dot(a0, w1_ref[0], preferred_element_type=F32)
    y1 = jnp.dot(a1, w1_ref[1], preferred_element_type=F32)
    pre = y0 + pltpu.roll(y1, ng - 1, 0)
    hid = pre * (1.0 / (1.0 + jnp.exp(-pre)))
    o_ref[0] = jnp.dot(hid.astype(BF), w2_ref[...], preferred_element_type=F32).astype(o_ref.dtype)


def _compress(r, pe, w1, w2):
    batch, ng, wd = r.shape
    wout = w2.shape[1]
    return pl.pallas_call(
        _compress_kernel,
        grid=(batch,),
        in_specs=[
            pl.BlockSpec((1, ng, wd), lambda b: (b, 0, 0)),
            _resident(pe.shape),
            _resident(w1.shape),
            _resident(w2.shape),
        ],
        out_specs=pl.BlockSpec((1, ng, wout), lambda b: (b, 0, 0)),
        out_shape=jax.ShapeDtypeStruct((batch, ng, wout), BF),
        compiler_params=_cparams(("parallel",)),
        name="nsa_compress",
    )(r, pe, w1, w2)


def _compress_weights(pe, w1, w2):
    g = NSA_GROUPS
    eye = jnp.eye(g, dtype=F32)
    w1r = w1.reshape(2, CMP_STRIDE, HEAD_DIM, CMP_HIDDEN)
    w1b = jnp.einsum('tldh,gk->tlgdkh', w1r, eye).reshape(2, CMP_STRIDE * g * HEAD_DIM, g * CMP_HIDDEN)
    w2b = jnp.einsum('hd,gk->ghkd', w2, eye).reshape(g * CMP_HIDDEN, g * HEAD_DIM)
    peb = jnp.broadcast_to(pe.reshape(2, CMP_STRIDE, 1, HEAD_DIM), (2, CMP_STRIDE, g, HEAD_DIM))
    return peb.reshape(2, 1, CMP_STRIDE * g * HEAD_DIM), w1b.astype(BF), w2b.astype(BF)


def _nsa_stack_q(q, t):
    lane = lax.broadcasted_iota(jnp.int32, (t, LANES), 1)
    zero = jnp.zeros((t, LANES), q.dtype)
    parts = []
    for half in range(2):
        keep = (lane < HEAD_DIM) if half == 0 else (lane >= HEAD_DIM)
        for r in range(4):
            parts.append(jnp.where(keep, q[:, r * LANES:(r + 1) * LANES], zero))
    return jnp.concatenate(parts, axis=0)


def _nsa_write(o, gl, branch, o_ref, t, gp=0):
    lane = lax.broadcasted_iota(jnp.int32, (t, LANES), 1)
    for r in range(4):
        c0 = r * 3 + branch
        c1 = 12 + r * 3 + branch
        g0 = 1.0 / (1.0 + jnp.exp(-gl[:, c0:c0 + 1]))
        g1 = 1.0 / (1.0 + jnp.exp(-gl[:, c1:c1 + 1]))
        blk = jnp.where(lane < HEAD_DIM, g0 * o[r * t:(r + 1) * t], g1 * o[(4 + r) * t:(5 + r) * t])
        o_ref[0, :, _lane_block(4 * gp + r)] = blk.astype(BF)


def _nsa_cmp_kernel(q_ref, kc_ref, vc_ref, gl_ref, ovl_ref, o_ref, sel_ref, *, t, ncmp):
    qi = pl.program_id(2)
    qs = _nsa_stack_q(q_ref[0], t)
    s = lax.dot_general(qs, kc_ref[0], _NT, preferred_element_type=F32)
    n_idx = lax.broadcasted_iota(jnp.int32, s.shape, 1)
    qidx = qi * t + (lax.broadcasted_iota(jnp.int32, s.shape, 0) & (t - 1))
    mask = (n_idx * CMP_STRIDE + (CMP_LEN - 1)) <= qidx
    sm = jnp.where(mask, s, NEG_INF)
    p = jnp.where(mask, jnp.exp2(sm - jnp.max(sm, axis=1, keepdims=True)), 0.0)
    l = jnp.sum(p, axis=1, keepdims=True)
    pn = p * (1.0 / jnp.where(l > 0.0, l, 1.0))
    o = jnp.dot(pn.astype(BF), vc_ref[0], preferred_element_type=F32)
    _nsa_write(o, gl_ref[0], 0, o_ref, t)

    nsel = ovl_ref.shape[0]
    jb = lax.broadcasted_iota(jnp.int32, (nsel, t), 0)
    qblk =(qi * t + lax.broadcasted_iota(jnp.int32, (nsel, t), 1)) >> SLC_SHIFT
    forced = (jb == 0) | (jb == qblk) | (jb == qblk - 1)
    ovl = ovl_ref[...]
    for half in range(2):
        ps = pn[(half * 4) * t:(half * 4 + 1) * t]
        for r in range(1, 4):
            ps = ps + pn[(half * 4 + r) * t:(half * 4 + r + 1) * t]
        ps_hi = ps.astype(BF)
        ps_lo = (ps - ps_hi.astype(F32)).astype(BF)
        imp = (lax.dot_general(ovl, ps_hi, _NT, preferred_element_type=F32)
               + lax.dot_general(ovl, ps_lo, _NT, preferred_element_type=F32))
        val = jnp.where(jb <= qblk, jnp.where(forced, FORCED_SCORE, imp), NEG_INF)
        slabs = [val[g * 8:(g + 1) * 8] for g in range(nsel // 8)]
        row8 = lax.broadcasted_iota(jnp.int32, (8, t), 0)
        cnt = [jnp.zeros((8, t), F32) for _ in slabs]
        for i in range(nsel):
            vi = val[i:i + 1, :]
            for g, sl in enumerate(slabs):
                if g > i // 8:
                    ahead = vi >= sl
                elif g < i // 8:
                    ahead = vi > sl
                else:
                    ahead = (vi > sl) | ((vi == sl) & (row8 > (i % 8)))
                cnt[g] = jnp.where(ahead, cnt[g] + 1.0, cnt[g])
        rank = jnp.concatenate(cnt, axis=0)
        chosen = jnp.where((rank < float(min(SLC_TOPK, nsel))) & (jb <= qblk), 1.0, 0.0)
        pad = jnp.zeros((LANES - nsel, t), F32)
        sel_ref[0, 0, half] = jnp.concatenate([chosen, pad], axis=0).T.astype(BF)


def _nsa_sel_kernel(q_ref, k_ref, v_ref, gl_ref, sel_ref, ind_ref, o_ref, m_ref, l_ref, acc_ref, *, t, tk):
    qi = pl.program_id(1)
    q_aug = []
    for gp in range(2):
        qs = _nsa_stack_q(q_ref[0, :, gp * 4 * LANES:(gp + 1) * 4 * LANES], t)
        for half in range(2):
            bias = jnp.where(sel_ref[0, gp, half].astype(F32) > 0.5, 0.0, NEG_INF).astype(BF)
            q_aug.append(jnp.concatenate([qs[half * 4 * t:(half + 1) * 4 * t],
                                          jnp.concatenate([bias] * 4, axis=0)], axis=1))
    _init_stats(m_ref, l_ref, acc_ref)
    jd = (qi * t) // tk

    def tile(j, causal):
        start = pl.multiple_of(j * tk, tk)
        ind = ind_ref[pl.ds(start, tk), :]
        if causal:
            qidx = qi * t + (lax.broadcasted_iota(jnp.int32, (4 * t, tk), 0) & (t - 1))
            kidx = j * tk + lax.broadcasted_iota(jnp.int32, (4 * t, tk), 1)
        ss = []
        for gp in range(2):
            k_aug = jnp.concatenate([k_ref[0, pl.ds(start, tk), _lane_block(gp)], ind], axis=1)
            for half in range(2):
                s = lax.dot_general(q_aug[2 * gp + half], k_aug, _NT, preferred_element_type=F32)
                ss.append(jnp.where(kidx <= qidx, s, NEG_INF) if causal else s)
        for c in range(4):
            v = v_ref[0, pl.ds(start, tk), _lane_block(c // 2)]
            _softmax_step(ss[c], v, m_ref.at[c], l_ref.at[c], acc_ref.at[c])

    tile(jd, True)

    def body(j, carry):
        tile(j, False)
        return carry

    lax.fori_loop(0, jd, body, 0)
    for gp in range(2):
        o = jnp.concatenate([_normalised(l_ref.at[2 * gp + half], acc_ref.at[2 * gp + half]) for half in range(2)],
                            axis=0)
        _nsa_write(o, gl_ref[0, :, _lane_block(gp)], 1, o_ref, t, gp)


def _nsa_win_kernel(q_ref, k_ref, v_ref, gl_ref, o_ref, m_ref, l_ref, acc_ref, *, t):
    qi = pl.program_id(1)
    qs = [_nsa_stack_q(q_ref[0, :, gp * 4 * LANES:(gp + 1) * 4 * LANES], t) for gp in range(2)]
    _init_stats(m_ref, l_ref, acc_ref)

    def tile(j, kind):
        start = pl.multiple_of(j * t, t)
        if kind != "full":
            row = lax.broadcasted_iota(jnp.int32, (8 * t, t), 0) & (t - 1)
            col = lax.broadcasted_iota(jnp.int32, (8 * t, t), 1)
            keep = (col <= row) if kind == "causal" else (col > row)
        for gp in range(2):
            k = k_ref[0, pl.ds(start, t), _lane_block(gp)]
            v = v_ref[0, pl.ds(start, t), _lane_block(gp)]
            s = lax.dot_general(qs[gp], k, _NT, preferred_element_type=F32)
            if kind != "full":
                s = jnp.where(keep, s, NEG_INF)
            _softmax_step(s, v, m_ref.at[gp], l_ref.at[gp], acc_ref.at[gp])

    tile(qi, "causal")

    @pl.when(qi >= 1)
    def _():
        tile(qi - 1, "full")

    @pl.when(qi >= 2)
    def _():
        tile(qi - 2, "tail")

    for gp in range(2):
        _nsa_write(_normalised(l_ref.at[gp], acc_ref.at[gp]), gl_ref[0, :, _lane_block(gp)], 2, o_ref, t, gp)


def _nsa_attention(pa, pf, kcmp, vcmp, ovl_t, batch, seq, t, tk):
    pa3 = pa.reshape(batch, seq, pa.shape[1])
    pf3 = pf.reshape(batch, seq, pf.shape[1])
    grid = (batch, 2, seq // t)
    ncmp = kcmp.shape[1]
    sem = _cparams(("parallel", "parallel", "arbitrary"))
    o_shape = jax.ShapeDtypeStruct((batch, seq, D_MODEL), BF)

    def q_specs(rows):
        q_spec = pl.BlockSpec((1, rows, 4 * LANES), lambda b, g, i: (b, i, g))
        gl_spec = pl.BlockSpec((1, rows, LANES), lambda b, g, i: (b, i, g))
        return q_spec, gl_spec, q_spec

    q_spec, gl_spec, o_spec = q_specs(t)
    ind = _block_indicator(seq, SLC_SHIFT)

    def kv_spec(col_block):
        return pl.BlockSpec((1, seq, LANES), lambda b, g, i: (b, 0, col_block + g))

    o_c, sel = pl.pallas_call(
        functools.partial(_nsa_cmp_kernel, t=t, ncmp=ncmp),
        grid=grid,
        in_specs=[
            q_spec,
            pl.BlockSpec((1, ncmp, LANES), lambda b, g, i: (b, 0, g)),
            pl.BlockSpec((1, ncmp, LANES), lambda b, g, i: (b, 0, g)),
            gl_spec,
            pl.BlockSpec(ovl_t.shape, lambda b, g, i: (0, 0)),
        ],
        out_specs=[o_spec, pl.BlockSpec((1, 1, 2, t, LANES), lambda b, g, i: (b, g, 0, i, 0))],
        out_shape=[o_shape, jax.ShapeDtypeStruct((batch, 2, 2, seq, LANES), BF)],
        compiler_params=sem,
        name="nsa_compressed",
    )(pa3, kcmp, vcmp, pf3, ovl_t)

    sem2 = _cparams(("parallel", "arbitrary"))
    kvw = 2 * LANES

    def row_spec(rows):
        return pl.BlockSpec((1, rows, D_MODEL), lambda b, i: (b, i, 0))

    def kv2_spec(col_block):
        return pl.BlockSpec((1, seq, kvw), lambda b, i: (b, 0, col_block))

    def gl2_spec(rows):
        return pl.BlockSpec((1, rows, kvw), lambda b, i: (b, i, 0))

    ts = 2 * t
    o_s = pl.pallas_call(
        functools.partial(_nsa_sel_kernel, t=ts, tk=tk),
        grid=(batch, seq // ts),
        in_specs=[row_spec(ts), kv2_spec(4), kv2_spec(6), gl2_spec(ts),
                  pl.BlockSpec((1, 2, 2, ts, LANES), lambda b, i: (b, 0, 0, i, 0)),
                  pl.BlockSpec(ind.shape, lambda b, i: (0, 0))],
        out_specs=row_spec(ts),
        out_shape=o_shape,
        scratch_shapes=_stats_scratch(4, 4 * ts),
        compiler_params=sem2,
        name="nsa_selected",
    )(pa3, pa3, pa3, pf3, sel, ind)

    tw = WINDOW // 2
    o_w = pl.pallas_call(
        functools.partial(_nsa_win_kernel, t=tw),
        grid=(batch, seq // tw),
        in_specs=[row_spec(tw), kv2_spec(5), kv2_spec(7), gl2_spec(tw)],
        out_specs=row_spec(tw),
        out_shape=o_shape,
        scratch_shapes=_stats_scratch(2, 8 * tw),
        compiler_params=sem2,
        name="nsa_window",
    )(pa3, pa3, pa3, pf3)
    n = batch * seq
    return [o_c.reshape(n, D_MODEL), o_s.reshape(n, D_MODEL), o_w.reshape(n, D_MODEL)]


def _rope_tables(seq):
    half = ROT_DIM // 2
    inv_freq = ROPE_THETA ** (-jnp.arange(half, dtype=F32) / half)
    ang = jnp.arange(seq).astype(F32)[:, None] * inv_freq[None, :]
    cos, sin = jnp.cos(ang), jnp.sin(ang)
    d = np.arange(LANES) % HEAD_DIM
    idx = d % half
    cos_t = jnp.where(d[None, :] < ROT_DIM, cos[:, idx], 1.0)
    sa_t = jnp.where(d[None, :] < half, -sin[:, idx], 0.0)
    sb_t = jnp.where((d[None, :] >= half) & (d[None, :] < ROT_DIM), sin[:, idx], 0.0)
    return cos_t.astype(F32), sa_t.astype(F32), sb_t.astype(F32)


def _block_indicator(seq, shift):
    blk = np.arange(seq)[:, None] >> shift
    return jnp.asarray((blk == np.arange(LANES)[None, :]).astype(np.float32), BF)


def _nsa_q_perm():
    cols = []
    for gp in range(2):
        for r in range(4):
            for half in range(2):
                head = 8 * gp + 4 * half + r
                cols.extend(range(head * HEAD_DIM, (head + 1) * HEAD_DIM))
    return np.asarray(cols, np.int32)


def _nsa_gate_cols():
    src = -np.ones(2 * LANES, np.int32)
    for gp in range(2):
        for half in range(2):
            for r in range(4):
                for br in range(3):
                    src[gp * LANES + half * 12 + r * 3 + br] = (4 * (2 * gp + half) + r) * 3 + br
    return src


def kernel(x, norm_g, ffn_w_in, ffn_w_out, diff_w_in, diff_w_out, diff_lambda, diff_subln, moba_w_in, moba_w_out,
           nsa_w_in, nsa_w_out, nsa_cmp_pe, nsa_cmp_w1, nsa_cmp_w2):
    batch, seq, d = x.shape
    n = batch * seq
    tm = 512
    h = x.reshape(n, d)
    tables = _rope_tables(seq)
    wi_all = ffn_w_in.astype(BF)
    wo_all = ffn_w_out.astype(BF)

    for i in range(DEPTH):
        g = norm_g[i]
        h = _ffn(h, g[0:2], wi_all, wo_all, i, 0, tm)

        kind, j = i % 3, i // 3
        if kind == 0:
            lambda_init = 0.8 - 0.6 * math.exp(-0.3 * i)
            plan = [(c * 256, 256, c < 8, 0, c * 256, None, Q_SCALE if c < 4 else None) for c in range(12)]
            (qkv,) = _proj(h, g[2:3], diff_w_in[j].astype(BF), tables, plan, [(3 * D_MODEL, BF)], tm, seq)
            attn = _diff_attention(qkv, diff_lambda[j], diff_subln[j].reshape(1, LANES), batch, seq, 512,
                                   lambda_init)
            parts, w_out = [attn], diff_w_out[j]
        elif kind == 1:
            plan = [(c * 256, 256, c < 8, 0, c * 256, (c - 4) * 256 if 4 <= c < 8 else None,
                     Q_SCALE if c < 4 else None) for c in range(12)]
            qkv, kmean = _proj(h, g[2:3], moba_w_in[j].astype(BF), tables, plan, [(3 * D_MODEL, BF)], tm, seq,
                               kmean_width=D_MODEL)
            attn = _moba_attention(qkv, kmean, batch, seq, 512)
            parts, w_out = [attn], moba_w_out[j]
        else:
            w = nsa_w_in[j]
            perm = _nsa_q_perm()
            kvw = NSA_GROUPS * HEAD_DIM
            base = NSA_HEADS * HEAD_DIM
            seg = {name: w[:, base + k * kvw: base + (k + 1) * kvw]
                   for k, name in enumerate(["kc", "vc", "ks", "vs", "kw", "vw"])}
            gsrc = _nsa_gate_cols()
            glog = w[:, base + 6 * kvw:]
            gate_w = jnp.where(gsrc[None, :] >= 0, glog[:, np.maximum(gsrc, 0)], 0.0)
            w_all = jnp.concatenate([w[:, perm], seg["ks"], seg["kw"], seg["kc"], seg["vs"], seg["vw"], seg["vc"],
                                     gate_w], axis=1).astype(BF)
            plan = [(c * 256, 256, True, 0, c * 256, None, Q_SCALE if c < 4 else None)
                    for c in range(6)]
            plan.append((1536, 256, True, 1, 0, None, None))
            plan.append((1792, 256, False, 0, 1536, None, None))
            plan.append((2048, 256, False, 0, 1792, None, None))
            plan.append((2304, 256, False, 2, 0, None, None))
            plan.append((2560, 256, False, 3, 0, None, None))
            pa, kc, vc, pf = _proj(h, g[2:3], w_all, tables, plan,
                                   [(2048, BF), (kvw, F32), (kvw, F32), (2 * LANES, F32)], tm, seq)

            ng = seq // CMP_STRIDE
            cmp_tm = []
            for a, src in enumerate((kc, vc)):
                pe, w1, w2 = _compress_weights(nsa_cmp_pe[j][a], nsa_cmp_w1[j][a], nsa_cmp_w2[j][a])
                cmp_tm.append(_compress(src.reshape(batch, ng, CMP_STRIDE * kvw), pe, w1, w2))

            nsel = seq // SLC_BLOCK
            cs = np.arange(ng)[:, None] * CMP_STRIDE
            bs = np.arange(nsel)[None, :] * SLC_BLOCK
            ovl = ((cs < bs + SLC_BLOCK) & (cs + CMP_LEN > bs)).astype(np.float32)
            ovl[ng - 1, :] = 0.0
            ovl_t = jnp.asarray(ovl.T, BF)
            parts = _nsa_attention(pa, pf, cmp_tm[0], cmp_tm[1], ovl_t, batch, seq, 128, 512)
            w_out = nsa_w_out[j][perm, :]

        h = _ffn(h, g[4:6], wi_all, wo_all, i, 1, tm, parts=parts, wp=w_out.astype(BF), gp=g[3:4])
    return h.reshape(batch, seq, d)
```

```python
import functools
import math

import jax
import jax.numpy as jnp
import numpy as np
from jax import lax
from jax.experimental import pallas as pl
from jax.experimental.pallas import tpu as pltpu

D_MODEL = 1024
DEPTH = 4
HEAD_DIM = 64
ROT_DIM = HEAD_DIM // 4
ROPE_THETA = 500000.0
NORM_EPS = 1e-6
NEG_INF = -1e30
REMOVED = -3e38

DIFF_HEADS = 8
DIFF_SUBLN_EPS = 1e-5
MOBA_BLOCK = 256
MOBA_SHIFT = 8
MOBA_TOPK = 3
NSA_HEADS = 16
NSA_GROUPS = 4
CMP_LEN = 32
CMP_STRIDE = 16
CMP_HIDDEN = 256
SLC_BLOCK = 64
SLC_SHIFT = 6
SLC_TOPK = 16
WINDOW = 512
FORCED_SCORE = 1e9
D_FF = 2816

LANES = 128
FF_CHUNK = 256
CHAINS = 4
Q_SCALE = HEAD_DIM ** -0.5 * math.log2(math.e)

BF = jnp.bfloat16
F32 = jnp.float32
VMEM_LIMIT = 56 * 1024 * 1024

_NT = (((1,), (1,)), ((), ()))


def _cparams(sem):
    return pltpu.CompilerParams(dimension_semantics=sem, vmem_limit_bytes=VMEM_LIMIT)


def _rms(x, g, eps):
    return x * lax.rsqrt(jnp.mean(x * x, axis=-1, keepdims=True) + eps) * g


def _resident(shape):
    nd = len(shape)
    return pl.BlockSpec(shape, lambda *_: (0,) * nd, pipeline_mode=pl.Buffered(1))


def _ffn_kernel(*refs, n_parts):
    h_ref = refs[0]
    parts = refs[1:1 + n_parts]
    if n_parts:
        wp_ref, gp_ref = refs[1 + n_parts:3 + n_parts]
        refs = refs[3 + n_parts:]
    else:
        refs = refs[1:]
    g_ref, wi_ref, wo_ref, o_ref, acc_ref = refs
    h = h_ref[...]
    if n_parts:
        a = parts[0][...]
        if n_parts > 1:
            a = a.astype(F32)
            for p in parts[1:]:
                a = a + p[...].astype(F32)
            a = a.astype(BF)
        h = h + _rms(jnp.dot(a, wp_ref[...], preferred_element_type=F32), gp_ref[...], NORM_EPS)
    xn = _rms(h, g_ref[0:1, :], NORM_EPS).astype(BF)
    for c in range(D_FF // FF_CHUNK):
        lo = c * FF_CHUNK
        gate = jnp.dot(xn, wi_ref[:, lo:lo + FF_CHUNK], preferred_element_type=F32)
        up = jnp.dot(xn, wi_ref[:, D_FF + lo:D_FF + lo + FF_CHUNK], preferred_element_type=F32)
        act = ((gate * (1.0 / (1.0 + jnp.exp(-gate)))) * up).astype(BF)
        part = jnp.dot(act, wo_ref[lo:lo + FF_CHUNK, :], preferred_element_type=F32)
        if c == 0:
            acc_ref[...] = part
        else:
            acc_ref[...] += part
    o_ref[...] = h + 0.5 * _rms(acc_ref[...], g_ref[1:2, :], NORM_EPS)


def _ffn(h, g2, wi_all, wo_all, layer, which, tm, parts=(), wp=None, gp=None):
    n, d = h.shape
    row = pl.BlockSpec((tm, d), lambda i: (i, 0))
    proj_args = [wp, gp] if parts else []

    def picked(w):
        return pl.BlockSpec((None, None) + w.shape[2:], lambda i: (layer, which, 0, 0), pipeline_mode=pl.Buffered(1))

    return pl.pallas_call(
        functools.partial(_ffn_kernel, n_parts=len(parts)),
        grid=(n // tm,),
        in_specs=([row] * (1 + len(parts)) + [_resident(a.shape) for a in proj_args]
                  + [_resident(g2.shape), picked(wi_all), picked(wo_all)]),
        out_specs=row,
        out_shape=jax.ShapeDtypeStruct((n, d), F32),
        scratch_shapes=[pltpu.VMEM((tm, d), F32)],
        compiler_params=_cparams(("parallel",)),
        name="ffn_halfstep",
    )(h, *parts, *proj_args, g2, wi_all, wo_all)


def _proj_kernel(h_ref, g_ref, w_ref, cos_ref, sa_ref, sb_ref, *out_refs, plan, tm, n_out):
    xn = _rms(h_ref[...], g_ref[...], NORM_EPS).astype(BF)
    cos = cos_ref[...]
    sa = sa_ref[...]
    sb = sb_ref[...]
    for col, width, rope, dest, off, km_off, scale in plan:
        y = jnp.dot(xn, w_ref[:, col:col + width], preferred_element_type=F32)
        for k in range(width // LANES):
            yk = y[:, k * LANES:(k + 1) * LANES]
            if rope:
                yk = yk * cos + pltpu.roll(yk, LANES - ROT_DIM // 2, 1) * sa + pltpu.roll(yk, ROT_DIM // 2, 1) * sb
            if scale is not None:
                yk = yk * scale
            o_ref = out_refs[dest]
            o_ref[:, off + k * LANES:off + (k + 1) * LANES] = yk.astype(o_ref.dtype)
            if km_off is not None:
                km_ref = out_refs[n_out]
                for r in range(tm // MOBA_BLOCK):
                    blk = yk[r * MOBA_BLOCK:(r + 1) * MOBA_BLOCK, :]
                    km_ref[0, r:r + 1, km_off + k * LANES:km_off + (k + 1) * LANES] = jnp.mean(
                        blk, axis=0, keepdims=True)


def _proj(h, g, w, tables, plan, outs, tm, seq, kmean_width=None):
    n, d = h.shape
    per_seq = seq // tm
    out_shape = [jax.ShapeDtypeStruct((n, wd), dt) for wd, dt in outs]
    out_specs = [pl.BlockSpec((tm, wd), lambda i: (i, 0)) for wd, _ in outs]
    if kmean_width is not None:
        nb = tm // MOBA_BLOCK
        out_shape.append(jax.ShapeDtypeStruct((n // tm, nb, kmean_width), F32))
        out_specs.append(pl.BlockSpec((1, nb, kmean_width), lambda i: (i, 0, 0)))
    tab_spec = pl.BlockSpec((tm, LANES), lambda i: (i % per_seq, 0))
    return pl.pallas_call(
        functools.partial(_proj_kernel, plan=tuple(plan), tm=tm, n_out=len(outs)),
        grid=(n // tm,),
        in_specs=[
            pl.BlockSpec((tm, d), lambda i: (i, 0)),
            _resident(g.shape),
            _resident(w.shape),
            tab_spec, tab_spec, tab_spec,
        ],
        out_specs=out_specs,
        out_shape=out_shape,
        compiler_params=_cparams(("parallel",)),
        name="norm_proj",
    )(h, g, w, *tables)


def _softmax_step(s, v, m_ref, l_ref, acc_ref):
    m_prev = m_ref[...]
    m_new = jnp.maximum(m_prev, jnp.max(s, axis=1, keepdims=True))
    alpha = jnp.exp2(m_prev - m_new)
    ps = [jnp.exp2(s[:, c * LANES:(c + 1) * LANES] - m_new) for c in range(s.shape[1] // LANES)]
    lsum = ps[0]
    for p in ps[1:]:
        lsum = lsum + p
    l_ref[...] = alpha * l_ref[...] + lsum
    p = jnp.concatenate([x.astype(BF) for x in ps], axis=1)
    acc_ref[...] = alpha * acc_ref[...] + jnp.dot(p, v, preferred_element_type=F32)
    m_ref[...] = m_new


def _init_stats(m_ref, l_ref, acc_ref):
    m_ref[...] = jnp.full(m_ref.shape, NEG_INF, F32)
    l_ref[...] = jnp.zeros(l_ref.shape, F32)
    acc_ref[...] = jnp.zeros(acc_ref.shape, F32)


def _normalised(l_ref, acc_ref):
    return acc_ref[...] * (1.0 / jnp.sum(l_ref[...], axis=1, keepdims=True))


def _stats_scratch(chains, rows):
    return [pltpu.VMEM((chains, rows, LANES), F32) for _ in range(3)]


def _split_halves(q):
    lane = lax.broadcasted_iota(jnp.int32, q.shape, 1)
    zero = jnp.zeros_like(q)
    return jnp.concatenate([jnp.where(lane < HEAD_DIM, q, zero), jnp.where(lane >= HEAD_DIM, q, zero)], axis=0)


def _lane_block(c):
    return slice(c * LANES, (c + 1) * LANES)


def _diff_kernel(q_ref, k_ref, v_ref, lam_ref, sg_ref, o_ref, m_ref, l_ref, acc_ref, *, t, lambda_init):
    qi = pl.program_id(2)
    qs = [_split_halves(q_ref[0, :, _lane_block(c)]) for c in range(CHAINS)]
    _init_stats(m_ref, l_ref, acc_ref)

    def tile(j, causal):
        start = pl.multiple_of(j * t, t)
        ss = [lax.dot_general(qs[c], k_ref[0, pl.ds(start, t), _lane_block(c)], _NT, preferred_element_type=F32)
              for c in range(CHAINS)]
        if causal:
            row = lax.broadcasted_iota(jnp.int32, (2 * t, t), 0) & (t - 1)
            col = lax.broadcasted_iota(jnp.int32, (2 * t, t), 1)
            ss = [jnp.where(col <= row, s, NEG_INF) for s in ss]
        for c in range(CHAINS):
            v = v_ref[0, pl.ds(start, t), _lane_block(c)]
            _softmax_step(ss[c], v, m_ref.at[c], l_ref.at[c], acc_ref.at[c])

    tile(qi, True)

    def body(j, carry):
        tile(j, False)
        return carry

    lax.fori_loop(0, qi, body, 0)

    lam = lam_ref[...]
    lam_full = (jnp.exp(jnp.sum(lam[0:1] * lam[1:2], axis=1, keepdims=True))
                - jnp.exp(jnp.sum(lam[2:3] * lam[3:4], axis=1, keepdims=True)) + lambda_init)
    for c in range(CHAINS):
        o = _normalised(l_ref.at[c], acc_ref.at[c])
        od = o[:t] - lam_full * o[t:]
        od = _rms(od, sg_ref[...], DIFF_SUBLN_EPS) * (1.0 - lambda_init)
        o_ref[0, :, _lane_block(c)] = od.astype(BF)


def _diff_attention(qkv, lam, subln, batch, seq, t, lambda_init):
    nh = DIFF_HEADS
    ns = nh // CHAINS
    w = CHAINS * LANES
    qkv3 = qkv.reshape(batch, seq, 3 * nh * LANES)
    out = pl.pallas_call(
        functools.partial(_diff_kernel, t=t, lambda_init=lambda_init),
        grid=(batch, ns, seq // t),
        in_specs=[
            pl.BlockSpec((1, t, w), lambda b, h, i: (b, i, h)),
            pl.BlockSpec((1, seq, w), lambda b, h, i: (b, 0, ns + h)),
            pl.BlockSpec((1, seq, w), lambda b, h, i: (b, 0, 2 * ns + h)),
            pl.BlockSpec(lam.shape, lambda b, h, i: (0, 0)),
            pl.BlockSpec(subln.shape, lambda b, h, i: (0, 0)),
        ],
        out_specs=pl.BlockSpec((1, t, w), lambda b, h, i: (b, i, h)),
        out_shape=jax.ShapeDtypeStruct((batch, seq, nh * LANES), BF),
        scratch_shapes=_stats_scratch(CHAINS, 2 * t),
        compiler_params=_cparams(("parallel", "parallel", "arbitrary")),
        name="diff_attention",
    )(qkv3, qkv3, qkv3, lam, subln)
    return out.reshape(batch * seq, nh * LANES)


def _moba_select(q2, km, qi, t, nb):
    km_hi = km.astype(BF)
    km_lo = (km - km_hi.astype(F32)).astype(BF)
    gate = (lax.dot_general(km_hi, q2, _NT, preferred_element_type=F32)
            + lax.dot_general(km_lo, q2, _NT, preferred_element_type=F32))
    blk = lax.broadcasted_iota(jnp.int32, gate.shape, 0)
    blk_f = blk.astype(F32)
    own = (qi * t + (lax.broadcasted_iota(jnp.int32, gate.shape, 1) & (t - 1))) >> MOBA_SHIFT
    gate = jnp.where(blk < own, gate, NEG_INF)
    sel = jnp.zeros(gate.shape, F32)
    for _ in range(min(MOBA_TOPK, nb)):
        mx = jnp.max(gate, axis=0, keepdims=True)
        idx = jnp.min(jnp.where(gate == mx, blk_f, float(nb)), axis=0, keepdims=True)
        hit = blk_f == idx
        sel = jnp.where(hit & (mx > 0.5 * NEG_INF), 1.0, sel)
        gate = jnp.where(hit, REMOVED, gate)
    bias_t = jnp.where((sel > 0.5) | (blk == own), 0.0, NEG_INF)
    bias_t = jnp.concatenate([bias_t, jnp.zeros((LANES - nb, 2 * t), F32)], axis=0)
    return jnp.concatenate([bias_t[:, _lane_block(n)].T for n in range(2 * t // LANES)], axis=0).astype(BF)


def _moba_kernel(q_ref, k_ref, v_ref, km_ref, ind_ref, o_ref, m_ref, l_ref, acc_ref, *, t, nb):
    qi = pl.program_id(2)
    _init_stats(m_ref, l_ref, acc_ref)
    q_aug = []
    for c in range(CHAINS):
        q2 = _split_halves(q_ref[0, :, _lane_block(c)])
        bias = _moba_select(q2, km_ref[0, :, _lane_block(c)], qi, t, nb)
        q_aug.append(jnp.concatenate([q2, bias], axis=1))

    def tile(j, diagonal):
        start = pl.multiple_of(j * t, t)
        ind = ind_ref[pl.ds(start, t), :]
        if diagonal:
            row = lax.broadcasted_iota(jnp.int32, (2 * t, t), 0) & (t - 1)
            col = lax.broadcasted_iota(jnp.int32, (2 * t, t), 1)
            future = ((row >> MOBA_SHIFT) == (col >> MOBA_SHIFT)) & (col > row)
        for c in range(CHAINS):
            k_aug = jnp.concatenate([k_ref[0, pl.ds(start, t), _lane_block(c)], ind], axis=1)
            v = v_ref[0, pl.ds(start, t), _lane_block(c)]
            s = lax.dot_general(q_aug[c], k_aug, _NT, preferred_element_type=F32)
            if diagonal:
                s = jnp.where(future, NEG_INF, s)
            _softmax_step(s, v, m_ref.at[c], l_ref.at[c], acc_ref.at[c])

    tile(qi, True)

    def body(j, carry):
        tile(j, False)
        return carry

    lax.fori_loop(0, qi, body, 0)

    lane = lax.broadcasted_iota(jnp.int32, (t, LANES), 1)
    for c in range(CHAINS):
        o = _normalised(l_ref.at[c], acc_ref.at[c])
        o_ref[0, :, _lane_block(c)] = jnp.where(lane < HEAD_DIM, o[:t], o[t:]).astype(BF)


def _moba_attention(qkv, kmean, batch, seq, t):
    nb = seq // MOBA_BLOCK
    ns = D_MODEL // LANES // CHAINS
    w = CHAINS * LANES
    qkv3 = qkv.reshape(batch, seq, 3 * D_MODEL)
    km3 = kmean.reshape(batch, nb, D_MODEL)
    ind = _block_indicator(seq, MOBA_SHIFT)
    out = pl.pallas_call(
        functools.partial(_moba_kernel, t=t, nb=nb),
        grid=(batch, ns, seq // t),
        in_specs=[
            pl.BlockSpec((1, t, w), lambda b, p, i: (b, i, p)),
            pl.BlockSpec((1, seq, w), lambda b, p, i: (b, 0, ns + p)),
            pl.BlockSpec((1, seq, w), lambda b, p, i: (b, 0, 2 * ns + p)),
            pl.BlockSpec((1, nb, w), lambda b, p, i: (b, 0, p)),
            pl.BlockSpec(ind.shape, lambda b, p, i: (0, 0)),
        ],
        out_specs=pl.BlockSpec((1, t, w), lambda b, p, i: (b, i, p)),
        out_shape=jax.ShapeDtypeStruct((batch, seq, D_MODEL), BF),
        scratch_shapes=_stats_scratch(CHAINS, 2 * t),
        compiler_params=_cparams(("parallel", "parallel", "arbitrary")),
        name="moba_attention",
    )(qkv3, qkv3, qkv3, km3, ind)
    return out.reshape(batch * seq, D_MODEL)


def _compress_kernel(r_ref, pe_ref, w1_ref, w2_ref, o_ref):
    r = r_ref[0]
    ng = r.shape[0]
    a0 = (r + pe_ref[0]).astype(BF)
    a1 = (r + pe_ref[1]).astype(BF)
    y0 = jnp.dot(a0, w1_ref[0], preferred_element_type=F32)
    y1 = jnp.dot(a1, w1_ref[1], preferred_element_type=F32)
    pre = y0 + pltpu.roll(y1, ng - 1, 0)
    hid = pre * (1.0 / (1.0 + jnp.exp(-pre)))
    o_ref[0] = jnp.dot(hid.astype(BF), w2_ref[...], preferred_element_type=F32).astype(o_ref.dtype)


def _compress(r, pe, w1, w2):
    batch, ng, wd = r.shape
    wout = w2.shape[1]
    return pl.pallas_call(
        _compress_kernel,
        grid=(batch,),
        in_specs=[
            pl.BlockSpec((1, ng, wd), lambda b: (b, 0, 0)),
            _resident(pe.shape),
            _resident(w1.shape),
            _resident(w2.shape),
        ],
        out_specs=pl.BlockSpec((1, ng, wout), lambda b: (b, 0, 0)),
        out_shape=jax.ShapeDtypeStruct((batch, ng, wout), BF),
        compiler_params=_cparams(("parallel",)),
        name="nsa_compress",
    )(r, pe, w1, w2)


def _compress_weights(pe, w1, w2):
    g = NSA_GROUPS
    eye = jnp.eye(g, dtype=F32)
    w1r = w1.reshape(2, CMP_STRIDE, HEAD_DIM, CMP_HIDDEN)
    w1b = jnp.einsum('tldh,gk->tlgdkh', w1r, eye).reshape(2, CMP_STRIDE * g * HEAD_DIM, g * CMP_HIDDEN)
    w2b = jnp.einsum('hd,gk->ghkd', w2, eye).reshape(g * CMP_HIDDEN, g * HEAD_DIM)
    peb = jnp.broadcast_to(pe.reshape(2, CMP_STRIDE, 1, HEAD_DIM), (2, CMP_STRIDE, g, HEAD_DIM))
    return peb.reshape(2, 1, CMP_STRIDE * g * HEAD_DIM), w1b.astype(BF), w2b.astype(BF)


def _nsa_stack_q(q, t):
    lane = lax.broadcasted_iota(jnp.int32, (t, LANES), 1)
    zero = jnp.zeros((t, LANES), q.dtype)
    parts = []
    for half in range(2):
        keep = (lane < HEAD_DIM) if half == 0 else (lane >= HEAD_DIM)
        for r in range(4):
            parts.append(jnp.where(keep, q[:, r * LANES:(r + 1) * LANES], zero))
    return jnp.concatenate(parts, axis=0)


def _nsa_write(o, gl, branch, o_ref, t, gp=0):
    lane = lax.broadcasted_iota(jnp.int32, (t, LANES), 1)
    for r in range(4):
        c0 = r * 3 + branch
        c1 = 12 + r * 3 + branch
        g0 = 1.0 / (1.0 + jnp.exp(-gl[:, c0:c0 + 1]))
        g1 = 1.0 / (1.0 + jnp.exp(-gl[:, c1:c1 + 1]))
        blk = jnp.where(lane < HEAD_DIM, g0 * o[r * t:(r + 1) * t], g1 * o[(4 + r) * t:(5 + r) * t])
        o_ref[0, :, _lane_block(4 * gp + r)] = blk.astype(BF)


def _nsa_cmp_kernel(q_ref, kc_ref, vc_ref, gl_ref, ovl_ref, o_ref, sel_ref, *, t, ncmp):
    qi = pl.program_id(2)
    qs = _nsa_stack_q(q_ref[0], t)
    s = lax.dot_general(qs, kc_ref[0], _NT, preferred_element_type=F32)
    n_idx = lax.broadcasted_iota(jnp.int32, s.shape, 1)
    qidx = qi * t + (lax.broadcasted_iota(jnp.int32, s.shape, 0) & (t - 1))
    mask = (n_idx * CMP_STRIDE + (CMP_LEN - 1)) <= qidx
    sm = jnp.where(mask, s, NEG_INF)
    p = jnp.where(mask, jnp.exp2(sm - jnp.max(sm, axis=1, keepdims=True)), 0.0)
    l = jnp.sum(p, axis=1, keepdims=True)
    pn = p * (1.0 / jnp.where(l > 0.0, l, 1.0))
    o = jnp.dot(pn.astype(BF), vc_ref[0], preferred_element_type=F32)
    _nsa_write(o, gl_ref[0], 0, o_ref, t)

    nsel = ovl_ref.shape[0]
    jb = lax.broadcasted_iota(jnp.int32, (nsel, t), 0)
    qblk =(qi * t + lax.broadcasted_iota(jnp.int32, (nsel, t), 1)) >> SLC_SHIFT
    forced = (jb == 0) | (jb == qblk) | (jb == qblk - 1)
    ovl = ovl_ref[...]
    for half in range(2):
        ps = pn[(half * 4) * t:(half * 4 + 1) * t]
        for r in range(1, 4):
            ps = ps + pn[(half * 4 + r) * t:(half * 4 + r + 1) * t]
        ps_hi = ps.astype(BF)
        ps_lo = (ps - ps_hi.astype(F32)).astype(BF)
        imp = (lax.dot_general(ovl, ps_hi, _NT, preferred_element_type=F32)
               + lax.dot_general(ovl, ps_lo, _NT, preferred_element_type=F32))
        val = jnp.where(jb <= qblk, jnp.where(forced, FORCED_SCORE, imp), NEG_INF)
        slabs = [val[g * 8:(g + 1) * 8] for g in range(nsel // 8)]
        row8 = lax.broadcasted_iota(jnp.int32, (8, t), 0)
        cnt = [jnp.zeros((8, t), F32) for _ in slabs]
        for i in range(nsel):
            vi = val[i:i + 1, :]
            for g, sl in enumerate(slabs):
                if g > i // 8:
                    ahead = vi >= sl
                elif g < i // 8:
                    ahead = vi > sl
                else:
                    ahead = (vi > sl) | ((vi == sl) & (row8 > (i % 8)))
                cnt[g] = jnp.where(ahead, cnt[g] + 1.0, cnt[g])
        rank = jnp.concatenate(cnt, axis=0)
        chosen = jnp.where((rank < float(min(SLC_TOPK, nsel))) & (jb <= qblk), 1.0, 0.0)
        pad = jnp.zeros((LANES - nsel, t), F32)
        chosen = jnp.concatenate([chosen, pad], axis=0)
        sel_ref[0, 0, half] = jnp.concatenate([chosen[:, _lane_block(n)].T for n in range(t // LANES)],
                                              axis=0).astype(BF)


def _nsa_sel_kernel(q_ref, k_ref, v_ref, gl_ref, sel_ref, ind_ref, o_ref, m_ref, l_ref, acc_ref, *, t, tk):
    qi = pl.program_id(1)
    q_aug = []
    for gp in range(2):
        qs = _nsa_stack_q(q_ref[0, :, gp * 4 * LANES:(gp + 1) * 4 * LANES], t)
        for half in range(2):
            bias = jnp.where(sel_ref[0, gp, half].astype(F32) > 0.5, 0.0, NEG_INF).astype(BF)
            q_aug.append(jnp.concatenate([qs[half * 4 * t:(half + 1) * 4 * t],
                                          jnp.concatenate([bias] * 4, axis=0)], axis=1))
    _init_stats(m_ref, l_ref, acc_ref)
    jd = (qi * t) // tk

    def tile(j, causal):
        start = pl.multiple_of(j * tk, tk)
        ind = ind_ref[pl.ds(start, tk), :]
        if causal:
            qidx = qi * t + (lax.broadcasted_iota(jnp.int32, (4 * t, tk), 0) & (t - 1))
            kidx = j * tk + lax.broadcasted_iota(jnp.int32, (4 * t, tk), 1)
        ss = []
        for gp in range(2):
            k_aug = jnp.concatenate([k_ref[0, pl.ds(start, tk), _lane_block(gp)], ind], axis=1)
            for half in range(2):
                s = lax.dot_general(q_aug[2 * gp + half], k_aug, _NT, preferred_element_type=F32)
                ss.append(jnp.where(kidx <= qidx, s, NEG_INF) if causal else s)
        for c in range(4):
            v = v_ref[0, pl.ds(start, tk), _lane_block(c // 2)]
            _softmax_step(ss[c], v, m_ref.at[c], l_ref.at[c], acc_ref.at[c])

    tile(jd, True)

    def body(j, carry):
        tile(j, False)
        return carry

    lax.fori_loop(0, jd, body, 0)
    for gp in range(2):
        o = jnp.concatenate([_normalised(l_ref.at[2 * gp + half], acc_ref.at[2 * gp + half]) for half in range(2)],
                            axis=0)
        _nsa_write(o, gl_ref[0, :, _lane_block(gp)], 1, o_ref, t, gp)


def _nsa_win_kernel(q_ref, k_ref, v_ref, gl_ref, o_ref, m_ref, l_ref, acc_ref, *, t):
    qi = pl.program_id(1)
    qs = [_nsa_stack_q(q_ref[0, :, gp * 4 * LANES:(gp + 1) * 4 * LANES], t) for gp in range(2)]
    _init_stats(m_ref, l_ref, acc_ref)

    def tile(j, kind):
        start = pl.multiple_of(j * t, t)
        if kind != "full":
            row = lax.broadcasted_iota(jnp.int32, (8 * t, t), 0) & (t - 1)
            col = lax.broadcasted_iota(jnp.int32, (8 * t, t), 1)
            keep = (col <= row) if kind == "causal" else (col > row)
        for gp in range(2):
            k = k_ref[0, pl.ds(start, t), _lane_block(gp)]
            v = v_ref[0, pl.ds(start, t), _lane_block(gp)]
            s = lax.dot_general(qs[gp], k, _NT, preferred_element_type=F32)
            if kind != "full":
                s = jnp.where(keep, s, NEG_INF)
            _softmax_step(s, v, m_ref.at[gp], l_ref.at[gp], acc_ref.at[gp])

    tile(qi, "causal")

    @pl.when(qi >= 1)
    def _():
        tile(qi - 1, "full")

    @pl.when(qi >= 2)
    def _():
        tile(qi - 2, "tail")

    for gp in range(2):
        _nsa_write(_normalised(l_ref.at[gp], acc_ref.at[gp]), gl_ref[0, :, _lane_block(gp)], 2, o_ref, t, gp)


def _nsa_attention(pa, pf, kcmp, vcmp, ovl_t, batch, seq, t, tk):
    pa3 = pa.reshape(batch, seq, pa.shape[1])
    pf3 = pf.reshape(batch, seq, pf.shape[1])
    grid = (batch, 2, seq // t)
    ncmp = kcmp.shape[1]
    sem = _cparams(("parallel", "parallel", "arbitrary"))
    o_shape = jax.ShapeDtypeStruct((batch, seq, D_MODEL), BF)

    def q_specs(rows):
        q_spec = pl.BlockSpec((1, rows, 4 * LANES), lambda b, g, i: (b, i, g))
        gl_spec = pl.BlockSpec((1, rows, LANES), lambda b, g, i: (b, i, g))
        return q_spec, gl_spec, q_spec

    q_spec, gl_spec, o_spec = q_specs(t)
    ind = _block_indicator(seq, SLC_SHIFT)

    def kv_spec(col_block):
        return pl.BlockSpec((1, seq, LANES), lambda b, g, i: (b, 0, col_block + g))

    o_c, sel = pl.pallas_call(
        functools.partial(_nsa_cmp_kernel, t=t, ncmp=ncmp),
        grid=grid,
        in_specs=[
            q_spec,
            pl.BlockSpec((1, ncmp, LANES), lambda b, g, i: (b, 0, g)),
            pl.BlockSpec((1, ncmp, LANES), lambda b, g, i: (b, 0, g)),
            gl_spec,
            pl.BlockSpec(ovl_t.shape, lambda b, g, i: (0, 0)),
        ],
        out_specs=[o_spec, pl.BlockSpec((1, 1, 2, t, LANES), lambda b, g, i: (b, g, 0, i, 0))],
        out_shape=[o_shape, jax.ShapeDtypeStruct((batch, 2, 2, seq, LANES), BF)],
        compiler_params=sem,
        name="nsa_compressed",
    )(pa3, kcmp, vcmp, pf3, ovl_t)

    sem2 = _cparams(("parallel", "arbitrary"))
    kvw = 2 * LANES

    def row_spec(rows):
        return pl.BlockSpec((1, rows, D_MODEL), lambda b, i: (b, i, 0))

    def kv2_spec(col_block):
        return pl.BlockSpec((1, seq, kvw), lambda b, i: (b, 0, col_block))

    def gl2_spec(rows):
        return pl.BlockSpec((1, rows, kvw), lambda b, i: (b, i, 0))

    ts = t
    o_s = pl.pallas_call(
        functools.partial(_nsa_sel_kernel, t=ts, tk=tk),
        grid=(batch, seq // ts),
        in_specs=[row_spec(ts), kv2_spec(4), kv2_spec(6), gl2_spec(ts),
                  pl.BlockSpec((1, 2, 2, ts, LANES), lambda b, i: (b, 0, 0, i, 0)),
                  pl.BlockSpec(ind.shape, lambda b, i: (0, 0))],
        out_specs=row_spec(ts),
        out_shape=o_shape,
        scratch_shapes=_stats_scratch(4, 4 * ts),
        compiler_params=sem2,
        name="nsa_selected",
    )(pa3, pa3, pa3, pf3, sel, ind)

    tw = WINDOW // 2
    o_w = pl.pallas_call(
        functools.partial(_nsa_win_kernel, t=tw),
        grid=(batch, seq // tw),
        in_specs=[row_spec(tw), kv2_spec(5), kv2_spec(7), gl2_spec(tw)],
        out_specs=row_spec(tw),
        out_shape=o_shape,
        scratch_shapes=_stats_scratch(2, 8 * tw),
        compiler_params=sem2,
        name="nsa_window",
    )(pa3, pa3, pa3, pf3)
    n = batch * seq
    return [o_c.reshape(n, D_MODEL), o_s.reshape(n, D_MODEL), o_w.reshape(n, D_MODEL)]


def _rope_tables(seq):
    half = ROT_DIM // 2
    inv_freq = ROPE_THETA ** (-jnp.arange(half, dtype=F32) / half)
    ang = jnp.arange(seq).astype(F32)[:, None] * inv_freq[None, :]
    cos, sin = jnp.cos(ang), jnp.sin(ang)
    d = np.arange(LANES) % HEAD_DIM
    idx = d % half
    cos_t = jnp.where(d[None, :] < ROT_DIM, cos[:, idx], 1.0)
    sa_t = jnp.where(d[None, :] < half, -sin[:, idx], 0.0)
    sb_t = jnp.where((d[None, :] >= half) & (d[None, :] < ROT_DIM), sin[:, idx], 0.0)
    return cos_t.astype(F32), sa_t.astype(F32), sb_t.astype(F32)


def _block_indicator(seq, shift):
    blk = np.arange(seq)[:, None] >> shift
    return jnp.asarray((blk == np.arange(LANES)[None, :]).astype(np.float32), BF)


def _nsa_q_perm():
    cols = []
    for gp in range(2):
        for r in range(4):
            for half in range(2):
                head = 8 * gp + 4 * half + r
                cols.extend(range(head * HEAD_DIM, (head + 1) * HEAD_DIM))
    return np.asarray(cols, np.int32)


def _nsa_gate_cols():
    src = -np.ones(2 * LANES, np.int32)
    for gp in range(2):
        for half in range(2):
            for r in range(4):
                for br in range(3):
                    src[gp * LANES + half * 12 + r * 3 + br] = (4 * (2 * gp + half) + r) * 3 + br
    return src


def kernel(x, norm_g, ffn_w_in, ffn_w_out, diff_w_in, diff_w_out, diff_lambda, diff_subln, moba_w_in, moba_w_out,
           nsa_w_in, nsa_w_out, nsa_cmp_pe, nsa_cmp_w1, nsa_cmp_w2):
    batch, seq, d = x.shape
    n = batch * seq
    tm = 512
    h = x.reshape(n, d)
    tables = _rope_tables(seq)
    wi_all = ffn_w_in.astype(BF)
    wo_all = ffn_w_out.astype(BF)

    for i in range(DEPTH):
        g = norm_g[i]
        h = _ffn(h, g[0:2], wi_all, wo_all, i, 0, tm)

        kind, j = i % 3, i // 3
        if kind == 0:
            lambda_init = 0.8 - 0.6 * math.exp(-0.3 * i)
            plan = [(c * 256, 256, c < 8, 0, c * 256, None, Q_SCALE if c < 4 else None) for c in range(12)]
            (qkv,) = _proj(h, g[2:3], diff_w_in[j].astype(BF), tables, plan, [(3 * D_MODEL, BF)], tm, seq)
            attn = _diff_attention(qkv, diff_lambda[j], diff_subln[j].reshape(1, LANES), batch, seq, 512,
                                   lambda_init)
            parts, w_out = [attn], diff_w_out[j]
        elif kind == 1:
            plan = [(c * 256, 256, c < 8, 0, c * 256, (c - 4) * 256 if 4 <= c < 8 else None,
                     Q_SCALE if c < 4 else None) for c in range(12)]
            qkv, kmean = _proj(h, g[2:3], moba_w_in[j].astype(BF), tables, plan, [(3 * D_MODEL, BF)], tm, seq,
                               kmean_width=D_MODEL)
            attn = _moba_attention(qkv, kmean, batch, seq, 512)
            parts, w_out = [attn], moba_w_out[j]
        else:
            w = nsa_w_in[j]
            perm = _nsa_q_perm()
            kvw = NSA_GROUPS * HEAD_DIM
            base = NSA_HEADS * HEAD_DIM
            seg = {name: w[:, base + k * kvw: base + (k + 1) * kvw]
                   for k, name in enumerate(["kc", "vc", "ks", "vs", "kw", "vw"])}
            gsrc = _nsa_gate_cols()
            glog = w[:, base + 6 * kvw:]
            gate_w = jnp.where(gsrc[None, :] >= 0, glog[:, np.maximum(gsrc, 0)], 0.0)
            w_all = jnp.concatenate([w[:, perm], seg["ks"], seg["kw"], seg["kc"], seg["vs"], seg["vw"], seg["vc"],
                                     gate_w], axis=1).astype(BF)
            plan = [(c * 256, 256, True, 0, c * 256, None, Q_SCALE if c < 4 else None)
                    for c in range(6)]
            plan.append((1536, 256, True, 1, 0, None, None))
            plan.append((1792, 256, False, 0, 1536, None, None))
            plan.append((2048, 256, False, 0, 1792, None, None))
            plan.append((2304, 256, False, 2, 0, None, None))
            plan.append((2560, 256, False, 3, 0, None, None))
            pa, kc, vc, pf = _proj(h, g[2:3], w_all, tables, plan,
                                   [(2048, BF), (kvw, F32), (kvw, F32), (2 * LANES, F32)], tm, seq)

            ng = seq // CMP_STRIDE
            cmp_tm = []
            for a, src in enumerate((kc, vc)):
                pe, w1, w2 = _compress_weights(nsa_cmp_pe[j][a], nsa_cmp_w1[j][a], nsa_cmp_w2[j][a])
                cmp_tm.append(_compress(src.reshape(batch, ng, CMP_STRIDE * kvw), pe, w1, w2))

            nsel = seq // SLC_BLOCK
            cs = np.arange(ng)[:, None] * CMP_STRIDE
            bs = np.arange(nsel)[None, :] * SLC_BLOCK
            ovl = ((cs < bs + SLC_BLOCK) & (cs + CMP_LEN > bs)).astype(np.float32)
            ovl[ng - 1, :] = 0.0
            ovl_t = jnp.asarray(ovl.T, BF)
            parts = _nsa_attention(pa, pf, cmp_tm[0], cmp_tm[1], ovl_t, batch, seq, 256, 512)
            w_out = nsa_w_out[j][perm, :]

        h = _ffn(h, g[4:6], wi_all, wo_all, i, 1, tm, parts=parts, wp=w_out.astype(BF), gp=g[3:4])
    return h.reshape(batch, seq, d)
```

```python
import functools
import math

import jax
import jax.numpy as jnp
import numpy as np
from jax import lax
from jax.experimental import pallas as pl
from jax.experimental.pallas import tpu as pltpu

D_MODEL = 1024
DEPTH = 4
HEAD_DIM = 64
ROT_DIM = HEAD_DIM // 4
ROPE_THETA = 500000.0
NORM_EPS = 1e-6
NEG_INF = -1e30
REMOVED = -3e38

DIFF_HEADS = 8
DIFF_SUBLN_EPS = 1e-5
MOBA_BLOCK = 256
MOBA_SHIFT = 8
MOBA_TOPK = 3
NSA_HEADS = 16
NSA_GROUPS = 4
CMP_LEN = 32
CMP_STRIDE = 16
CMP_HIDDEN = 256
SLC_BLOCK = 64
SLC_SHIFT = 6
SLC_TOPK = 16
WINDOW = 512
FORCED_SCORE = 1e9
D_FF = 2816

LANES = 128
FF_CHUNK = 256
CHAINS = 4
Q_SCALE = HEAD_DIM ** -0.5 * math.log2(math.e)

BF = jnp.bfloat16
F32 = jnp.float32
VMEM_LIMIT = 56 * 1024 * 1024

_NT = (((1,), (1,)), ((), ()))


def _cparams(sem):
    return pltpu.CompilerParams(dimension_semantics=sem, vmem_limit_bytes=VMEM_LIMIT)


def _rms(x, g, eps):
    return x * lax.rsqrt(jnp.mean(x * x, axis=-1, keepdims=True) + eps) * g


def _resident(shape):
    nd = len(shape)
    return pl.BlockSpec(shape, lambda *_: (0,) * nd, pipeline_mode=pl.Buffered(1))


def _ffn_kernel(*refs, n_parts):
    h_ref = refs[0]
    parts = refs[1:1 + n_parts]
    if n_parts:
        wp_ref, gp_ref = refs[1 + n_parts:3 + n_parts]
        refs = refs[3 + n_parts:]
    else:
        refs = refs[1:]
    g_ref, wi_ref, wo_ref, o_ref, acc_ref = refs
    h = h_ref[...]
    if n_parts:
        a = parts[0][...]
        if n_parts > 1:
            a = a.astype(F32)
            for p in parts[1:]:
                a = a + p[...].astype(F32)
            a = a.astype(BF)
        h = h + _rms(jnp.dot(a, wp_ref[...], preferred_element_type=F32), gp_ref[...], NORM_EPS)
    xn = _rms(h, g_ref[0:1, :], NORM_EPS).astype(BF)
    for c in range(D_FF // FF_CHUNK):
        lo = c * FF_CHUNK
        gate = jnp.dot(xn, wi_ref[:, lo:lo + FF_CHUNK], preferred_element_type=F32)
        up = jnp.dot(xn, wi_ref[:, D_FF + lo:D_FF + lo + FF_CHUNK], preferred_element_type=F32)
        act = ((gate * (1.0 / (1.0 + jnp.exp(-gate)))) * up).astype(BF)
        part = jnp.dot(act, wo_ref[lo:lo + FF_CHUNK, :], preferred_element_type=F32)
        if c == 0:
            acc_ref[...] = part
        else:
            acc_ref[...] += part
    o_ref[...] = h + 0.5 * _rms(acc_ref[...], g_ref[1:2, :], NORM_EPS)


def _ffn(h, g2, wi_all, wo_all, layer, which, tm, parts=(), wp=None, gp=None):
    n, d = h.shape
    row = pl.BlockSpec((tm, d), lambda i: (i, 0))
    proj_args = [wp, gp] if parts else []

    def picked(w):
        return pl.BlockSpec((None, None) + w.shape[2:], lambda i: (layer, which, 0, 0), pipeline_mode=pl.Buffered(1))

    return pl.pallas_call(
        functools.partial(_ffn_kernel, n_parts=len(parts)),
        grid=(n // tm,),
        in_specs=([row] * (1 + len(parts)) + [_resident(a.shape) for a in proj_args]
                  + [_resident(g2.shape), picked(wi_all), picked(wo_all)]),
        out_specs=row,
        out_shape=jax.ShapeDtypeStruct((n, d), F32),
        scratch_shapes=[pltpu.VMEM((tm, d), F32)],
        compiler_params=_cparams(("parallel",)),
        name="ffn_halfstep",
    )(h, *parts, *proj_args, g2, wi_all, wo_all)


def _proj_kernel(h_ref, g_ref, w_ref, cos_ref, sa_ref, sb_ref, *refs, plan, tm, n_out):
    out_refs, slab_ref = refs[:-1], refs[-1]
    xn = _rms(h_ref[...], g_ref[...], NORM_EPS).astype(BF)
    cos = cos_ref[...]
    sa = sa_ref[...]
    sb = sb_ref[...]
    for col, width, rope, dest, off, km_off, scale in plan:
        y = jnp.dot(xn, w_ref[:, col:col + width], preferred_element_type=F32)
        for k in range(width // LANES):
            yk = y[:, k * LANES:(k + 1) * LANES]
            if rope:
                yk = yk * cos + pltpu.roll(yk, LANES - ROT_DIM // 2, 1) * sa + pltpu.roll(yk, ROT_DIM // 2, 1) * sb
            if scale is not None:
                yk = yk * scale
            o_ref = out_refs[dest]
            if o_ref.shape[0] == tm:
                o_ref[:, off + k * LANES:off + (k + 1) * LANES] = yk.astype(o_ref.dtype)
            else:
                grp = tm // o_ref.shape[0]
                slab_ref[...] = yk
                for tok in range(grp):
                    lo = tok * width + off + k * LANES
                    o_ref[:, lo:lo + LANES] = slab_ref[pl.ds(tok, tm // grp, stride=grp), :].astype(o_ref.dtype)
            if km_off is not None:
                km_ref = out_refs[n_out]
                for r in range(tm // MOBA_BLOCK):
                    blk = yk[r * MOBA_BLOCK:(r + 1) * MOBA_BLOCK, :]
                    km_ref[0, r:r + 1, km_off + k * LANES:km_off + (k + 1) * LANES] = jnp.mean(
                        blk, axis=0, keepdims=True)


def _proj(h, g, w, tables, plan, outs, tm, seq, kmean_width=None):
    n, d = h.shape
    per_seq = seq // tm
    out_shape = [jax.ShapeDtypeStruct((n // gr, wd * gr), dt) for wd, dt, gr in outs]
    out_specs = [pl.BlockSpec((tm // gr, wd * gr), lambda i: (i, 0)) for wd, _, gr in outs]
    if kmean_width is not None:
        nb = tm // MOBA_BLOCK
        out_shape.append(jax.ShapeDtypeStruct((n // tm, nb, kmean_width), F32))
        out_specs.append(pl.BlockSpec((1, nb, kmean_width), lambda i: (i, 0, 0)))
    tab_spec = pl.BlockSpec((tm, LANES), lambda i: (i % per_seq, 0))
    return pl.pallas_call(
        functools.partial(_proj_kernel, plan=tuple(plan), tm=tm, n_out=len(outs)),
        grid=(n // tm,),
        in_specs=[
            pl.BlockSpec((tm, d), lambda i: (i, 0)),
            _resident(g.shape),
            _resident(w.shape),
            tab_spec, tab_spec, tab_spec,
        ],
        out_specs=out_specs,
        out_shape=out_shape,
        scratch_shapes=[pltpu.VMEM((tm, LANES), F32)],
        compiler_params=_cparams(("parallel",)),
        name="norm_proj",
    )(h, g, w, *tables)


def _softmax_step(s, v, m_ref, l_ref, acc_ref):
    m_prev = m_ref[...]
    m_new = jnp.maximum(m_prev, jnp.max(s, axis=1, keepdims=True))
    alpha = jnp.exp2(m_prev - m_new)
    ps = [jnp.exp2(s[:, c * LANES:(c + 1) * LANES] - m_new) for c in range(s.shape[1] // LANES)]
    lsum = ps[0]
    for p in ps[1:]:
        lsum = lsum + p
    l_ref[...] = alpha * l_ref[...] + lsum
    p = jnp.concatenate([x.astype(BF) for x in ps], axis=1)
    acc_ref[...] = alpha * acc_ref[...] + jnp.dot(p, v, preferred_element_type=F32)
    m_ref[...] = m_new


def _init_stats(m_ref, l_ref, acc_ref):
    m_ref[...] = jnp.full(m_ref.shape, NEG_INF, F32)
    l_ref[...] = jnp.zeros(l_ref.shape, F32)
    acc_ref[...] = jnp.zeros(acc_ref.shape, F32)


def _normalised(l_ref, acc_ref):
    return acc_ref[...] * (1.0 / jnp.sum(l_ref[...], axis=1, keepdims=True))


def _stats_scratch(chains, rows):
    return [pltpu.VMEM((chains, rows, LANES), F32) for _ in range(3)]


def _split_halves(q):
    lane = lax.broadcasted_iota(jnp.int32, q.shape, 1)
    zero = jnp.zeros_like(q)
    return jnp.concatenate([jnp.where(lane < HEAD_DIM, q, zero), jnp.where(lane >= HEAD_DIM, q, zero)], axis=0)


def _lane_block(c):
    return slice(c * LANES, (c + 1) * LANES)


def _diff_kernel(q_ref, k_ref, v_ref, lam_ref, sg_ref, o_ref, m_ref, l_ref, acc_ref, *, t, lambda_init):
    qi = pl.program_id(2)
    qs = [_split_halves(q_ref[0, :, _lane_block(c)]) for c in range(CHAINS)]
    _init_stats(m_ref, l_ref, acc_ref)

    def tile(j, causal):
        start = pl.multiple_of(j * t, t)
        ss = [lax.dot_general(qs[c], k_ref[0, pl.ds(start, t), _lane_block(c)], _NT, preferred_element_type=F32)
              for c in range(CHAINS)]
        if causal:
            row = lax.broadcasted_iota(jnp.int32, (2 * t, t), 0) & (t - 1)
            col = lax.broadcasted_iota(jnp.int32, (2 * t, t), 1)
            ss = [jnp.where(col <= row, s, NEG_INF) for s in ss]
        for c in range(CHAINS):
            v = v_ref[0, pl.ds(start, t), _lane_block(c)]
            _softmax_step(ss[c], v, m_ref.at[c], l_ref.at[c], acc_ref.at[c])

    tile(qi, True)

    def body(j, carry):
        tile(j, False)
        return carry

    lax.fori_loop(0, qi, body, 0)

    lam = lam_ref[...]
    lam_full = (jnp.exp(jnp.sum(lam[0:1] * lam[1:2], axis=1, keepdims=True))
                - jnp.exp(jnp.sum(lam[2:3] * lam[3:4], axis=1, keepdims=True)) + lambda_init)
    for c in range(CHAINS):
        o = _normalised(l_ref.at[c], acc_ref.at[c])
        od = o[:t] - lam_full * o[t:]
        od = _rms(od, sg_ref[...], DIFF_SUBLN_EPS) * (1.0 - lambda_init)
        o_ref[0, :, _lane_block(c)] = od.astype(BF)


def _diff_attention(qkv, lam, subln, batch, seq, t, lambda_init):
    nh = DIFF_HEADS
    ns = nh // CHAINS
    w = CHAINS * LANES
    qkv3 = qkv.reshape(batch, seq, 3 * nh * LANES)
    out = pl.pallas_call(
        functools.partial(_diff_kernel, t=t, lambda_init=lambda_init),
        grid=(batch, ns, seq // t),
        in_specs=[
            pl.BlockSpec((1, t, w), lambda b, h, i: (b, i, h)),
            pl.BlockSpec((1, seq, w), lambda b, h, i: (b, 0, ns + h)),
            pl.BlockSpec((1, seq, w), lambda b, h, i: (b, 0, 2 * ns + h)),
            pl.BlockSpec(lam.shape, lambda b, h, i: (0, 0)),
            pl.BlockSpec(subln.shape, lambda b, h, i: (0, 0)),
        ],
        out_specs=pl.BlockSpec((1, t, w), lambda b, h, i: (b, i, h)),
        out_shape=jax.ShapeDtypeStruct((batch, seq, nh * LANES), BF),
        scratch_shapes=_stats_scratch(CHAINS, 2 * t),
        compiler_params=_cparams(("parallel", "parallel", "arbitrary")),
        name="diff_attention",
    )(qkv3, qkv3, qkv3, lam, subln)
    return out.reshape(batch * seq, nh * LANES)


def _moba_select(q2, km, qi, t, nb):
    km_hi = km.astype(BF)
    km_lo = (km - km_hi.astype(F32)).astype(BF)
    gate = (lax.dot_general(km_hi, q2, _NT, preferred_element_type=F32)
            + lax.dot_general(km_lo, q2, _NT, preferred_element_type=F32))
    blk = lax.broadcasted_iota(jnp.int32, gate.shape, 0)
    blk_f = blk.astype(F32)
    own = (qi * t + (lax.broadcasted_iota(jnp.int32, gate.shape, 1) & (t - 1))) >> MOBA_SHIFT
    gate = jnp.where(blk < own, gate, NEG_INF)
    sel = jnp.zeros(gate.shape, F32)
    for _ in range(min(MOBA_TOPK, nb)):
        mx = jnp.max(gate, axis=0, keepdims=True)
        idx = jnp.min(jnp.where(gate == mx, blk_f, float(nb)), axis=0, keepdims=True)
        hit = blk_f == idx
        sel = jnp.where(hit & (mx > 0.5 * NEG_INF), 1.0, sel)
        gate = jnp.where(hit, REMOVED, gate)
    bias_t = jnp.where((sel > 0.5) | (blk == own), 0.0, NEG_INF)
    bias_t = jnp.concatenate([bias_t, jnp.zeros((LANES - nb, 2 * t), F32)], axis=0)
    return jnp.concatenate([bias_t[:, _lane_block(n)].T for n in range(2 * t // LANES)], axis=0).astype(BF)


def _moba_kernel(q_ref, k_ref, v_ref, km_ref, ind_ref, o_ref, m_ref, l_ref, acc_ref, *, t, nb):
    qi = pl.program_id(2)
    _init_stats(m_ref, l_ref, acc_ref)
    q_aug = []
    for c in range(CHAINS):
        q2 = _split_halves(q_ref[0, :, _lane_block(c)])
        bias = _moba_select(q2, km_ref[0, :, _lane_block(c)], qi, t, nb)
        q_aug.append(jnp.concatenate([q2, bias], axis=1))

    def tile(j, diagonal):
        start = pl.multiple_of(j * t, t)
        ind = ind_ref[pl.ds(start, t), :]
        if diagonal:
            row = lax.broadcasted_iota(jnp.int32, (2 * t, t), 0) & (t - 1)
            col = lax.broadcasted_iota(jnp.int32, (2 * t, t), 1)
            future = ((row >> MOBA_SHIFT) == (col >> MOBA_SHIFT)) & (col > row)
        for c in range(CHAINS):
            k_aug = jnp.concatenate([k_ref[0, pl.ds(start, t), _lane_block(c)], ind], axis=1)
            v = v_ref[0, pl.ds(start, t), _lane_block(c)]
            s = lax.dot_general(q_aug[c], k_aug, _NT, preferred_element_type=F32)
            if diagonal:
                s = jnp.where(future, NEG_INF, s)
            _softmax_step(s, v, m_ref.at[c], l_ref.at[c], acc_ref.at[c])

    tile(qi, True)

    def body(j, carry):
        tile(j, False)
        return carry

    lax.fori_loop(0, qi, body, 0)

    lane = lax.broadcasted_iota(jnp.int32, (t, LANES), 1)
    for c in range(CHAINS):
        o = _normalised(l_ref.at[c], acc_ref.at[c])
        o_ref[0, :, _lane_block(c)] = jnp.where(lane < HEAD_DIM, o[:t], o[t:]).astype(BF)


def _moba_attention(qkv, kmean, batch, seq, t):
    nb = seq // MOBA_BLOCK
    ns = D_MODEL // LANES // CHAINS
    w = CHAINS * LANES
    qkv3 = qkv.reshape(batch, seq, 3 * D_MODEL)
    km3 = kmean.reshape(batch, nb, D_MODEL)
    ind = _block_indicator(seq, MOBA_SHIFT)
    out = pl.pallas_call(
        functools.partial(_moba_kernel, t=t, nb=nb),
        grid=(batch, ns, seq // t),
        in_specs=[
            pl.BlockSpec((1, t, w), lambda b, p, i: (b, i, p)),
            pl.BlockSpec((1, seq, w), lambda b, p, i: (b, 0, ns + p)),
            pl.BlockSpec((1, seq, w), lambda b, p, i: (b, 0, 2 * ns + p)),
            pl.BlockSpec((1, nb, w), lambda b, p, i: (b, 0, p)),
            pl.BlockSpec(ind.shape, lambda b, p, i: (0, 0)),
        ],
        out_specs=pl.BlockSpec((1, t, w), lambda b, p, i: (b, i, p)),
        out_shape=jax.ShapeDtypeStruct((batch, seq, D_MODEL), BF),
        scratch_shapes=_stats_scratch(CHAINS, 2 * t),
        compiler_params=_cparams(("parallel", "parallel", "arbitrary")),
        name="moba_attention",
    )(qkv3, qkv3, qkv3, km3, ind)
    return out.reshape(batch * seq, D_MODEL)


def _compress_kernel(r_ref, pe_ref, w1_ref, w2_ref, o_ref):
    r = r_ref[0]
    ng = r.shape[0]
    a0 = (r + pe_ref[0]).astype(BF)
    a1 = (r + pe_ref[1]).astype(BF)
    y0 = jnp.dot(a0, w1_ref[0], preferred_element_type=F32)
    y1 = jnp.dot(a1, w1_ref[1], preferred_element_type=F32)
    pre = y0 + pltpu.roll(y1, ng - 1, 0)
    hid = pre * (1.0 / (1.0 + jnp.exp(-pre)))
    o_ref[0] = jnp.dot(hid.astype(BF), w2_ref[...], preferred_element_type=F32).astype(o_ref.dtype)


def _compress(r, pe, w1, w2):
    batch, ng, wd = r.shape
    wout = w2.shape[1]
    return pl.pallas_call(
        _compress_kernel,
        grid=(batch,),
        in_specs=[
            pl.BlockSpec((1, ng, wd), lambda b: (b, 0, 0)),
            _resident(pe.shape),
            _resident(w1.shape),
            _resident(w2.shape),
        ],
        out_specs=pl.BlockSpec((1, ng, wout), lambda b: (b, 0, 0)),
        out_shape=jax.ShapeDtypeStruct((batch, ng, wout), BF),
        compiler_params=_cparams(("parallel",)),
        name="nsa_compress",
    )(r, pe, w1, w2)


def _compress_weights(pe, w1, w2):
    g = NSA_GROUPS

    def block_diag(w):
        c = w.shape[-1]
        lead = [(0, 0)] * (w.ndim - 1)
        return jnp.concatenate([jnp.pad(w, lead + [(k * c, (g - 1 - k) * c)]) for k in range(g)], axis=-2)

    w1r = w1.astype(BF).reshape(2, CMP_STRIDE, HEAD_DIM, CMP_HIDDEN)
    w1b = block_diag(w1r).reshape(2, CMP_STRIDE * g * HEAD_DIM, g * CMP_HIDDEN)
    w2b = block_diag(w2.astype(BF))
    peb = jnp.broadcast_to(pe.reshape(2, CMP_STRIDE, 1, HEAD_DIM), (2, CMP_STRIDE, g, HEAD_DIM))
    return peb.reshape(2, 1, CMP_STRIDE * g * HEAD_DIM), w1b, w2b


def _nsa_stack_q(q, t):
    lane = lax.broadcasted_iota(jnp.int32, (t, LANES), 1)
    zero = jnp.zeros((t, LANES), q.dtype)
    parts = []
    for half in range(2):
        keep = (lane < HEAD_DIM) if half == 0 else (lane >= HEAD_DIM)
        for r in range(4):
            parts.append(jnp.where(keep, q[:, r * LANES:(r + 1) * LANES], zero))
    return jnp.concatenate(parts, axis=0)


def _nsa_write(o, gl, branch, o_ref, t, gp=0):
    lane = lax.broadcasted_iota(jnp.int32, (t, LANES), 1)
    for r in range(4):
        c0 = r * 3 + branch
        c1 = 12 + r * 3 + branch
        g0 = 1.0 / (1.0 + jnp.exp(-gl[:, c0:c0 + 1]))
        g1 = 1.0 / (1.0 + jnp.exp(-gl[:, c1:c1 + 1]))
        blk = jnp.where(lane < HEAD_DIM, g0 * o[r * t:(r + 1) * t], g1 * o[(4 + r) * t:(5 + r) * t])
        o_ref[0, :, _lane_block(4 * gp + r)] = blk.astype(BF)


def _nsa_cmp_kernel(q_ref, kc_ref, vc_ref, gl_ref, ovl_ref, o_ref, sel_ref, *, t, ncmp):
    qi = pl.program_id(2)
    qs = _nsa_stack_q(q_ref[0], t)
    s = lax.dot_general(qs, kc_ref[0], _NT, preferred_element_type=F32)
    n_idx = lax.broadcasted_iota(jnp.int32, s.shape, 1)
    qidx = qi * t + (lax.broadcasted_iota(jnp.int32, s.shape, 0) & (t - 1))
    mask = (n_idx * CMP_STRIDE + (CMP_LEN - 1)) <= qidx
    sm = jnp.where(mask, s, NEG_INF)
    p = jnp.where(mask, jnp.exp2(sm - jnp.max(sm, axis=1, keepdims=True)), 0.0)
    l = jnp.sum(p, axis=1, keepdims=True)
    pn = p * (1.0 / jnp.where(l > 0.0, l, 1.0))
    o = jnp.dot(pn.astype(BF), vc_ref[0], preferred_element_type=F32)
    _nsa_write(o, gl_ref[0], 0, o_ref, t)

    nsel = ovl_ref.shape[0]
    jb = lax.broadcasted_iota(jnp.int32, (nsel, t), 0)
    qblk =(qi * t + lax.broadcasted_iota(jnp.int32, (nsel, t), 1)) >> SLC_SHIFT
    forced = (jb == 0) | (jb == qblk) | (jb == qblk - 1)
    ovl = ovl_ref[...]
    for half in range(2):
        ps = pn[(half * 4) * t:(half * 4 + 1) * t]
        for r in range(1, 4):
            ps = ps + pn[(half * 4 + r) * t:(half * 4 + r + 1) * t]
        ps_hi = ps.astype(BF)
        ps_lo = (ps - ps_hi.astype(F32)).astype(BF)
        imp = (lax.dot_general(ovl, ps_hi, _NT, preferred_element_type=F32)
               + lax.dot_general(ovl, ps_lo, _NT, preferred_element_type=F32))
        val = jnp.where(jb <= qblk, jnp.where(forced, FORCED_SCORE, imp), NEG_INF)
        slabs = [val[g * 8:(g + 1) * 8] for g in range(nsel // 8)]
        row8 = lax.broadcasted_iota(jnp.int32, (8, t), 0)
        cnt = [jnp.zeros((8, t), F32) for _ in slabs]
        for i in range(nsel):
            vi = val[i:i + 1, :]
            for g, sl in enumerate(slabs):
                if g > i // 8:
                    ahead = vi >= sl
                elif g < i // 8:
                    ahead = vi > sl
                else:
                    ahead = (vi > sl) | ((vi == sl) & (row8 > (i % 8)))
                cnt[g] = jnp.where(ahead, cnt[g] + 1.0, cnt[g])
        rank = jnp.concatenate(cnt, axis=0)
        chosen = jnp.where((rank < float(min(SLC_TOPK, nsel))) & (jb <= qblk), 1.0, 0.0)
        pad = jnp.zeros((LANES - nsel, t), F32)
        chosen = jnp.concatenate([chosen, pad], axis=0)
        sel_ref[0, 0, half] = jnp.concatenate([chosen[:, _lane_block(n)].T for n in range(t // LANES)],
                                              axis=0).astype(BF)


def _nsa_sel_kernel(q_ref, k_ref, v_ref, gl_ref, sel_ref, ind_ref, o_ref, m_ref, l_ref, acc_ref, *, t, tk):
    qi = pl.program_id(1)
    q_aug = []
    for gp in range(2):
        qs = _nsa_stack_q(q_ref[0, :, gp * 4 * LANES:(gp + 1) * 4 * LANES], t)
        for half in range(2):
            bias = jnp.where(sel_ref[0, gp, half].astype(F32) > 0.5, 0.0, NEG_INF).astype(BF)
            q_aug.append(jnp.concatenate([qs[half * 4 * t:(half + 1) * 4 * t],
                                          jnp.concatenate([bias] * 4, axis=0)], axis=1))
    _init_stats(m_ref, l_ref, acc_ref)
    jd = (qi * t) // tk

    def tile(j, causal):
        start = pl.multiple_of(j * tk, tk)
        ind = ind_ref[pl.ds(start, tk), :]
        if causal:
            qidx = qi * t + (lax.broadcasted_iota(jnp.int32, (4 * t, tk), 0) & (t - 1))
            kidx = j * tk + lax.broadcasted_iota(jnp.int32, (4 * t, tk), 1)
        ss = []
        for gp in range(2):
            k_aug = jnp.concatenate([k_ref[0, pl.ds(start, tk), _lane_block(gp)], ind], axis=1)
            for half in range(2):
                s = lax.dot_general(q_aug[2 * gp + half], k_aug, _NT, preferred_element_type=F32)
                ss.append(jnp.where(kidx <= qidx, s, NEG_INF) if causal else s)
        for c in range(4):
            v = v_ref[0, pl.ds(start, tk), _lane_block(c // 2)]
            _softmax_step(ss[c], v, m_ref.at[c], l_ref.at[c], acc_ref.at[c])

    tile(jd, True)

    def body(j, carry):
        tile(j, False)
        return carry

    lax.fori_loop(0, jd, body, 0)
    for gp in range(2):
        o = jnp.concatenate([_normalised(l_ref.at[2 * gp + half], acc_ref.at[2 * gp + half]) for half in range(2)],
                            axis=0)
        _nsa_write(o, gl_ref[0, :, _lane_block(gp)], 1, o_ref, t, gp)


def _nsa_win_kernel(q_ref, k_ref, v_ref, gl_ref, o_ref, m_ref, l_ref, acc_ref, *, t):
    qi = pl.program_id(1)
    qs = [_nsa_stack_q(q_ref[0, :, gp * 4 * LANES:(gp + 1) * 4 * LANES], t) for gp in range(2)]
    _init_stats(m_ref, l_ref, acc_ref)

    def tile(j, kind):
        start = pl.multiple_of(j * t, t)
        if kind != "full":
            row = lax.broadcasted_iota(jnp.int32, (8 * t, t), 0) & (t - 1)
            col = lax.broadcasted_iota(jnp.int32, (8 * t, t), 1)
            keep = (col <= row) if kind == "causal" else (col > row)
        for gp in range(2):
            k = k_ref[0, pl.ds(start, t), _lane_block(gp)]
            v = v_ref[0, pl.ds(start, t), _lane_block(gp)]
            s = lax.dot_general(qs[gp], k, _NT, preferred_element_type=F32)
            if kind != "full":
                s = jnp.where(keep, s, NEG_INF)
            _softmax_step(s, v, m_ref.at[gp], l_ref.at[gp], acc_ref.at[gp])

    tile(qi, "causal")

    @pl.when(qi >= 1)
    def _():
        tile(qi - 1, "full")

    @pl.when(qi >= 2)
    def _():
        tile(qi - 2, "tail")

    for gp in range(2):
        _nsa_write(_normalised(l_ref.at[gp], acc_ref.at[gp]), gl_ref[0, :, _lane_block(gp)], 2, o_ref, t, gp)


def _nsa_attention(pa, pf, kcmp, vcmp, ovl_t, batch, seq, t, tk):
    pa3 = pa.reshape(batch, seq, pa.shape[1])
    pf3 = pf.reshape(batch, seq, pf.shape[1])
    grid = (batch, 2, seq // t)
    ncmp = kcmp.shape[1]
    sem = _cparams(("parallel", "parallel", "arbitrary"))
    o_shape = jax.ShapeDtypeStruct((batch, seq, D_MODEL), BF)

    def q_specs(rows):
        q_spec = pl.BlockSpec((1, rows, 4 * LANES), lambda b, g, i: (b, i, g))
        gl_spec = pl.BlockSpec((1, rows, LANES), lambda b, g, i: (b, i, g))
        return q_spec, gl_spec, q_spec

    q_spec, gl_spec, o_spec = q_specs(t)
    ind = _block_indicator(seq, SLC_SHIFT)

    def kv_spec(col_block):
        return pl.BlockSpec((1, seq, LANES), lambda b, g, i: (b, 0, col_block + g))

    o_c, sel = pl.pallas_call(
        functools.partial(_nsa_cmp_kernel, t=t, ncmp=ncmp),
        grid=grid,
        in_specs=[
            q_spec,
            pl.BlockSpec((1, ncmp, LANES), lambda b, g, i: (b, 0, g)),
            pl.BlockSpec((1, ncmp, LANES), lambda b, g, i: (b, 0, g)),
            gl_spec,
            pl.BlockSpec(ovl_t.shape, lambda b, g, i: (0, 0)),
        ],
        out_specs=[o_spec, pl.BlockSpec((1, 1, 2, t, LANES), lambda b, g, i: (b, g, 0, i, 0))],
        out_shape=[o_shape, jax.ShapeDtypeStruct((batch, 2, 2, seq, LANES), BF)],
        compiler_params=sem,
        name="nsa_compressed",
    )(pa3, kcmp, vcmp, pf3, ovl_t)

    sem2 = _cparams(("parallel", "arbitrary"))
    kvw = 2 * LANES

    def row_spec(rows):
        return pl.BlockSpec((1, rows, D_MODEL), lambda b, i: (b, i, 0))

    def kv2_spec(col_block):
        return pl.BlockSpec((1, seq, kvw), lambda b, i: (b, 0, col_block))

    def gl2_spec(rows):
        return pl.BlockSpec((1, rows, kvw), lambda b, i: (b, i, 0))

    ts = t
    o_s = pl.pallas_call(
        functools.partial(_nsa_sel_kernel, t=ts, tk=tk),
        grid=(batch, seq // ts),
        in_specs=[row_spec(ts), kv2_spec(4), kv2_spec(6), gl2_spec(ts),
                  pl.BlockSpec((1, 2, 2, ts, LANES), lambda b, i: (b, 0, 0, i, 0)),
                  pl.BlockSpec(ind.shape, lambda b, i: (0, 0))],
        out_specs=row_spec(ts),
        out_shape=o_shape,
        scratch_shapes=_stats_scratch(4, 4 * ts),
        compiler_params=sem2,
        name="nsa_selected",
    )(pa3, pa3, pa3, pf3, sel, ind)

    tw = WINDOW // 2
    o_w = pl.pallas_call(
        functools.partial(_nsa_win_kernel, t=tw),
        grid=(batch, seq // tw),
        in_specs=[row_spec(tw), kv2_spec(5), kv2_spec(7), gl2_spec(tw)],
        out_specs=row_spec(tw),
        out_shape=o_shape,
        scratch_shapes=_stats_scratch(2, 8 * tw),
        compiler_params=sem2,
        name="nsa_window",
    )(pa3, pa3, pa3, pf3)
    n = batch * seq
    return [o_c.reshape(n, D_MODEL), o_s.reshape(n, D_MODEL), o_w.reshape(n, D_MODEL)]


def _rope_tables(seq):
    half = ROT_DIM // 2
    inv_freq = ROPE_THETA ** (-jnp.arange(half, dtype=F32) / half)
    ang = jnp.arange(seq).astype(F32)[:, None] * inv_freq[None, :]
    cos, sin = jnp.cos(ang), jnp.sin(ang)
    d = np.arange(LANES) % HEAD_DIM
    idx = d % half
    cos_t = jnp.where(d[None, :] < ROT_DIM, cos[:, idx], 1.0)
    sa_t = jnp.where(d[None, :] < half, -sin[:, idx], 0.0)
    sb_t = jnp.where((d[None, :] >= half) & (d[None, :] < ROT_DIM), sin[:, idx], 0.0)
    return cos_t.astype(F32), sa_t.astype(F32), sb_t.astype(F32)


def _block_indicator(seq, shift):
    blk = np.arange(seq)[:, None] >> shift
    return jnp.asarray((blk == np.arange(LANES)[None, :]).astype(np.float32), BF)


def _nsa_q_perm():
    cols = []
    for gp in range(2):
        for r in range(4):
            for half in range(2):
                head = 8 * gp + 4 * half + r
                cols.extend(range(head * HEAD_DIM, (head + 1) * HEAD_DIM))
    return np.asarray(cols, np.int32)


def _nsa_gate_cols():
    src = -np.ones(2 * LANES, np.int32)
    for gp in range(2):
        for half in range(2):
            for r in range(4):
                for br in range(3):
                    src[gp * LANES + half * 12 + r * 3 + br] = (4 * (2 * gp + half) + r) * 3 + br
    return src


def kernel(x, norm_g, ffn_w_in, ffn_w_out, diff_w_in, diff_w_out, diff_lambda, diff_subln, moba_w_in, moba_w_out,
           nsa_w_in, nsa_w_out, nsa_cmp_pe, nsa_cmp_w1, nsa_cmp_w2):
    batch, seq, d = x.shape
    n = batch * seq
    tm = 512
    assert d == D_MODEL and seq % 512 == 0 and seq // SLC_BLOCK <= LANES and seq // MOBA_BLOCK >= 8
    h = x.reshape(n, d)
    tables = _rope_tables(seq)
    wi_all = ffn_w_in.astype(BF)
    wo_all = ffn_w_out.astype(BF)

    for i in range(DEPTH):
        g = norm_g[i]
        h = _ffn(h, g[0:2], wi_all, wo_all, i, 0, tm)

        kind, j = i % 3, i // 3
        if kind == 0:
            lambda_init = 0.8 - 0.6 * math.exp(-0.3 * i)
            plan = [(c * 256, 256, c < 8, 0, c * 256, None, Q_SCALE if c < 4 else None) for c in range(12)]
            (qkv,) = _proj(h, g[2:3], diff_w_in[j].astype(BF), tables, plan, [(3 * D_MODEL, BF, 1)], tm, seq)
            attn = _diff_attention(qkv, diff_lambda[j], diff_subln[j].reshape(1, LANES), batch, seq, 512,
                                   lambda_init)
            parts, w_out = [attn], diff_w_out[j]
        elif kind == 1:
            plan = [(c * 256, 256, c < 8, 0, c * 256, (c - 4) * 256 if 4 <= c < 8 else None,
                     Q_SCALE if c < 4 else None) for c in range(12)]
            qkv, kmean = _proj(h, g[2:3], moba_w_in[j].astype(BF), tables, plan, [(3 * D_MODEL, BF, 1)], tm, seq,
                               kmean_width=D_MODEL)
            attn = _moba_attention(qkv, kmean, batch, seq, 512)
            parts, w_out = [attn], moba_w_out[j]
        else:
            w = nsa_w_in[j]
            perm = _nsa_q_perm()
            kvw = NSA_GROUPS * HEAD_DIM
            base = NSA_HEADS * HEAD_DIM
            seg = {name: w[:, base + k * kvw: base + (k + 1) * kvw]
                   for k, name in enumerate(["kc", "vc", "ks", "vs", "kw", "vw"])}
            gsrc = _nsa_gate_cols()
            glog = w[:, base + 6 * kvw:]
            gate_w = jnp.where(gsrc[None, :] >= 0, glog[:, np.maximum(gsrc, 0)], 0.0)
            w_all = jnp.concatenate([w[:, perm], seg["ks"], seg["kw"], seg["kc"], seg["vs"], seg["vw"], seg["vc"],
                                     gate_w], axis=1).astype(BF)
            plan = [(c * 256, 256, True, 0, c * 256, None, Q_SCALE if c < 4 else None)
                    for c in range(6)]
            plan.append((1536, 256, True, 1, 0, None, None))
            plan.append((1792, 256, False, 0, 1536, None, None))
            plan.append((2048, 256, False, 0, 1792, None, None))
            plan.append((2304, 256, False, 2, 0, None, None))
            plan.append((2560, 256, False, 3, 0, None, None))
            pa, kc, vc, pf = _proj(h, g[2:3], w_all, tables, plan,
                                   [(2048, BF, 1), (kvw, F32, CMP_STRIDE), (kvw, F32, CMP_STRIDE),
                                    (2 * LANES, F32, 1)], tm, seq)

            ng = seq // CMP_STRIDE
            cmp_tm = []
            for a, src in enumerate((kc, vc)):
                pe, w1, w2 = _compress_weights(nsa_cmp_pe[j][a], nsa_cmp_w1[j][a], nsa_cmp_w2[j][a])
                cmp_tm.append(_compress(src.reshape(batch, ng, CMP_STRIDE * kvw), pe, w1, w2))

            nsel = seq // SLC_BLOCK
            cs = np.arange(ng)[:, None] * CMP_STRIDE
            bs = np.arange(nsel)[None, :] * SLC_BLOCK
            ovl = ((cs < bs + SLC_BLOCK) & (cs + CMP_LEN > bs)).astype(np.float32)
            ovl[ng - 1, :] = 0.0
            ovl_t = jnp.asarray(ovl.T, BF)
            parts = _nsa_attention(pa, pf, cmp_tm[0], cmp_tm[1], ovl_t, batch, seq, 256, 512)
            w_out = nsa_w_out[j][perm, :]

        h = _ffn(h, g[4:6], wi_all, wo_all, i, 1, tm, parts=parts, wp=w_out.astype(BF), gp=g[3:4])
    return h.reshape(batch, seq, d)
```

```python
import functools
import math

import jax
import jax.numpy as jnp
import numpy as np
from jax import lax
from jax.experimental import pallas as pl
from jax.experimental.pallas import tpu as pltpu

D_MODEL = 1024
DEPTH = 4
HEAD_DIM = 64
ROT_DIM = HEAD_DIM // 4
ROPE_THETA = 500000.0
NORM_EPS = 1e-6
NEG_INF = -1e30
REMOVED = -3e38

DIFF_HEADS = 8
DIFF_SUBLN_EPS = 1e-5
MOBA_BLOCK = 256
MOBA_SHIFT = 8
MOBA_TOPK = 3
NSA_HEADS = 16
NSA_GROUPS = 4
CMP_LEN = 32
CMP_STRIDE = 16
CMP_HIDDEN = 256
SLC_BLOCK = 64
SLC_SHIFT = 6
SLC_TOPK = 16
WINDOW = 512
FORCED_SCORE = 1e9
D_FF = 2816

LANES = 128
FF_CHUNK = 256
CHAINS = 4
Q_SCALE = HEAD_DIM ** -0.5 * math.log2(math.e)

BF = jnp.bfloat16
F32 = jnp.float32
VMEM_LIMIT = 56 * 1024 * 1024

_NT = (((1,), (1,)), ((), ()))


def _cparams(sem):
    return pltpu.CompilerParams(dimension_semantics=sem, vmem_limit_bytes=VMEM_LIMIT)


def _rms(x, g, eps):
    return x * lax.rsqrt(jnp.mean(x * x, axis=-1, keepdims=True) + eps) * g


def _resident(shape):
    nd = len(shape)
    return pl.BlockSpec(shape, lambda *_: (0,) * nd, pipeline_mode=pl.Buffered(1))


def _ffn_kernel(*refs, n_parts):
    h_ref = refs[0]
    parts = refs[1:1 + n_parts]
    if n_parts:
        wp_ref, gp_ref = refs[1 + n_parts:3 + n_parts]
        refs = refs[3 + n_parts:]
    else:
        refs = refs[1:]
    g_ref, wi_ref, wo_ref, o_ref, acc_ref = refs
    h = h_ref[...]
    if n_parts:
        a = parts[0][...]
        if n_parts > 1:
            a = a.astype(F32)
            for p in parts[1:]:
                a = a + p[...].astype(F32)
            a = a.astype(BF)
        h = h + _rms(jnp.dot(a, wp_ref[...], preferred_element_type=F32), gp_ref[...], NORM_EPS)
    xn = _rms(h, g_ref[0:1, :], NORM_EPS).astype(BF)
    for c in range(D_FF // FF_CHUNK):
        lo = c * FF_CHUNK
        gate = jnp.dot(xn, wi_ref[:, lo:lo + FF_CHUNK], preferred_element_type=F32)
        up = jnp.dot(xn, wi_ref[:, D_FF + lo:D_FF + lo + FF_CHUNK], preferred_element_type=F32)
        act = ((gate * (1.0 / (1.0 + jnp.exp(-gate)))) * up).astype(BF)
        part = jnp.dot(act, wo_ref[lo:lo + FF_CHUNK, :], preferred_element_type=F32)
        if c == 0:
            acc_ref[...] = part
        else:
            acc_ref[...] += part
    o_ref[...] = h + 0.5 * _rms(acc_ref[...], g_ref[1:2, :], NORM_EPS)


def _ffn(h, g2, wi_all, wo_all, layer, which, tm, parts=(), wp=None, gp=None):
    n, d = h.shape
    row = pl.BlockSpec((tm, d), lambda i: (i, 0))
    proj_args = [wp, gp] if parts else []

    def picked(w):
        return pl.BlockSpec((None, None) + w.shape[2:], lambda i: (layer, which, 0, 0), pipeline_mode=pl.Buffered(1))

    return pl.pallas_call(
        functools.partial(_ffn_kernel, n_parts=len(parts)),
        grid=(n // tm,),
        in_specs=([row] * (1 + len(parts)) + [_resident(a.shape) for a in proj_args]
                  + [_resident(g2.shape), picked(wi_all), picked(wo_all)]),
        out_specs=row,
        out_shape=jax.ShapeDtypeStruct((n, d), F32),
        scratch_shapes=[pltpu.VMEM((tm, d), F32)],
        compiler_params=_cparams(("parallel",)),
        name="ffn_halfstep",
    )(h, *parts, *proj_args, g2, wi_all, wo_all)


def _proj_kernel(h_ref, g_ref, w_ref, cos_ref, sa_ref, sb_ref, *refs, plan, tm, n_out):
    out_refs, slab_ref = refs[:-1], refs[-1]
    xn = _rms(h_ref[...], g_ref[...], NORM_EPS).astype(BF)
    cos = cos_ref[...]
    sa = sa_ref[...]
    sb = sb_ref[...]
    for col, width, rope, dest, off, km_off, scale in plan:
        y = jnp.dot(xn, w_ref[:, col:col + width], preferred_element_type=F32)
        for k in range(width // LANES):
            yk = y[:, k * LANES:(k + 1) * LANES]
            if rope:
                yk = yk * cos + pltpu.roll(yk, LANES - ROT_DIM // 2, 1) * sa + pltpu.roll(yk, ROT_DIM // 2, 1) * sb
            if scale is not None:
                yk = yk * scale
            o_ref = out_refs[dest]
            if o_ref.shape[0] == tm:
                o_ref[:, off + k * LANES:off + (k + 1) * LANES] = yk.astype(o_ref.dtype)
            else:
                grp = tm // o_ref.shape[0]
                slab_ref[...] = yk
                for tok in range(grp):
                    lo = tok * width + off + k * LANES
                    o_ref[:, lo:lo + LANES] = slab_ref[pl.ds(tok, tm // grp, stride=grp), :].astype(o_ref.dtype)
            if km_off is not None:
                km_ref = out_refs[n_out]
                for r in range(tm // MOBA_BLOCK):
                    blk = yk[r * MOBA_BLOCK:(r + 1) * MOBA_BLOCK, :]
                    km_ref[0, r:r + 1, km_off + k * LANES:km_off + (k + 1) * LANES] = jnp.mean(
                        blk, axis=0, keepdims=True)


def _proj(h, g, w, tables, plan, outs, tm, seq, kmean_width=None):
    n, d = h.shape
    per_seq = seq // tm
    out_shape = [jax.ShapeDtypeStruct((n // gr, wd * gr), dt) for wd, dt, gr in outs]
    out_specs = [pl.BlockSpec((tm // gr, wd * gr), lambda i: (i, 0)) for wd, _, gr in outs]
    if kmean_width is not None:
        nb = tm // MOBA_BLOCK
        out_shape.append(jax.ShapeDtypeStruct((n // tm, nb, kmean_width), F32))
        out_specs.append(pl.BlockSpec((1, nb, kmean_width), lambda i: (i, 0, 0)))
    tab_spec = pl.BlockSpec((tm, LANES), lambda i: (i % per_seq, 0))
    return pl.pallas_call(
        functools.partial(_proj_kernel, plan=tuple(plan), tm=tm, n_out=len(outs)),
        grid=(n // tm,),
        in_specs=[
            pl.BlockSpec((tm, d), lambda i: (i, 0)),
            _resident(g.shape),
            _resident(w.shape),
            tab_spec, tab_spec, tab_spec,
        ],
        out_specs=out_specs,
        out_shape=out_shape,
        scratch_shapes=[pltpu.VMEM((tm, LANES), F32)],
        compiler_params=_cparams(("parallel",)),
        name="norm_proj",
    )(h, g, w, *tables)


def _softmax_step(s, v, m_ref, l_ref, acc_ref):
    m_prev = m_ref[...]
    m_new = jnp.maximum(m_prev, jnp.max(s, axis=1, keepdims=True))
    alpha = jnp.exp2(m_prev - m_new)
    ps = [jnp.exp2(s[:, c * LANES:(c + 1) * LANES] - m_new) for c in range(s.shape[1] // LANES)]
    lsum = ps[0]
    for p in ps[1:]:
        lsum = lsum + p
    l_ref[...] = alpha * l_ref[...] + lsum
    p = jnp.concatenate([x.astype(BF) for x in ps], axis=1)
    acc_ref[...] = alpha * acc_ref[...] + jnp.dot(p, v, preferred_element_type=F32)
    m_ref[...] = m_new


def _init_stats(m_ref, l_ref, acc_ref):
    m_ref[...] = jnp.full(m_ref.shape, NEG_INF, F32)
    l_ref[...] = jnp.zeros(l_ref.shape, F32)
    acc_ref[...] = jnp.zeros(acc_ref.shape, F32)


def _normalised(l_ref, acc_ref):
    return acc_ref[...] * (1.0 / jnp.sum(l_ref[...], axis=1, keepdims=True))


def _stats_scratch(chains, rows):
    return [pltpu.VMEM((chains, rows, LANES), F32) for _ in range(3)]


def _split_halves(q):
    lane = lax.broadcasted_iota(jnp.int32, q.shape, 1)
    zero = jnp.zeros_like(q)
    return jnp.concatenate([jnp.where(lane < HEAD_DIM, q, zero), jnp.where(lane >= HEAD_DIM, q, zero)], axis=0)


def _lane_block(c):
    return slice(c * LANES, (c + 1) * LANES)


def _diff_kernel(q_ref, k_ref, v_ref, lam_ref, sg_ref, o_ref, m_ref, l_ref, acc_ref, *, t, lambda_init):
    qi = pl.program_id(2)
    qs = [_split_halves(q_ref[0, :, _lane_block(c)]) for c in range(CHAINS)]
    _init_stats(m_ref, l_ref, acc_ref)

    def tile(j, causal, width=t):
        start = pl.multiple_of(j * t, t)
        ss = [lax.dot_general(qs[c], k_ref[0, pl.ds(start, width), _lane_block(c)], _NT,
                              preferred_element_type=F32) for c in range(CHAINS)]
        if causal:
            row = lax.broadcasted_iota(jnp.int32, (2 * t, t), 0) & (t - 1)
            col = lax.broadcasted_iota(jnp.int32, (2 * t, t), 1)
            ss = [jnp.where(col <= row, s, NEG_INF) for s in ss]
        for c in range(CHAINS):
            v = v_ref[0, pl.ds(start, width), _lane_block(c)]
            _softmax_step(ss[c], v, m_ref.at[c], l_ref.at[c], acc_ref.at[c])

    tile(qi, True)

    def body(jj, carry):
        tile(2 * jj, False, 2 * t)
        return carry

    lax.fori_loop(0, qi // 2, body, 0)

    @pl.when((qi & 1) == 1)
    def _():
        tile(qi - 1, False)

    lam = lam_ref[...]
    lam_full = (jnp.exp(jnp.sum(lam[0:1] * lam[1:2], axis=1, keepdims=True))
                - jnp.exp(jnp.sum(lam[2:3] * lam[3:4], axis=1, keepdims=True)) + lambda_init)
    for c in range(CHAINS):
        o = _normalised(l_ref.at[c], acc_ref.at[c])
        od = o[:t] - lam_full * o[t:]
        od = _rms(od, sg_ref[...], DIFF_SUBLN_EPS) * (1.0 - lambda_init)
        o_ref[0, :, _lane_block(c)] = od.astype(BF)


def _diff_attention(qkv, lam, subln, batch, seq, t, lambda_init):
    nh = DIFF_HEADS
    ns = nh // CHAINS
    w = CHAINS * LANES
    qkv3 = qkv.reshape(batch, seq, 3 * nh * LANES)
    out = pl.pallas_call(
        functools.partial(_diff_kernel, t=t, lambda_init=lambda_init),
        grid=(batch, ns, seq // t),
        in_specs=[
            pl.BlockSpec((1, t, w), lambda b, h, i: (b, i, h)),
            pl.BlockSpec((1, seq, w), lambda b, h, i: (b, 0, ns + h)),
            pl.BlockSpec((1, seq, w), lambda b, h, i: (b, 0, 2 * ns + h)),
            pl.BlockSpec(lam.shape, lambda b, h, i: (0, 0)),
            pl.BlockSpec(subln.shape, lambda b, h, i: (0, 0)),
        ],
        out_specs=pl.BlockSpec((1, t, w), lambda b, h, i: (b, i, h)),
        out_shape=jax.ShapeDtypeStruct((batch, seq, nh * LANES), BF),
        scratch_shapes=_stats_scratch(CHAINS, 2 * t),
        compiler_params=_cparams(("parallel", "parallel", "arbitrary")),
        name="diff_attention",
    )(qkv3, qkv3, qkv3, lam, subln)
    return out.reshape(batch * seq, nh * LANES)


def _moba_select(q2, km, qi, t, nb):
    km_hi = km.astype(BF)
    km_lo = (km - km_hi.astype(F32)).astype(BF)
    gate = (lax.dot_general(km_hi, q2, _NT, preferred_element_type=F32)
            + lax.dot_general(km_lo, q2, _NT, preferred_element_type=F32))
    blk = lax.broadcasted_iota(jnp.int32, gate.shape, 0)
    blk_f = blk.astype(F32)
    own = (qi * t + (lax.broadcasted_iota(jnp.int32, gate.shape, 1) & (t - 1))) >> MOBA_SHIFT
    gate = jnp.where(blk < own, gate, NEG_INF)
    sel = jnp.zeros(gate.shape, F32)
    for _ in range(min(MOBA_TOPK, nb)):
        mx = jnp.max(gate, axis=0, keepdims=True)
        idx = jnp.min(jnp.where(gate == mx, blk_f, float(nb)), axis=0, keepdims=True)
        hit = blk_f == idx
        sel = jnp.where(hit & (mx > 0.5 * NEG_INF), 1.0, sel)
        gate = jnp.where(hit, REMOVED, gate)
    bias_t = jnp.where((sel > 0.5) | (blk == own), 0.0, NEG_INF)
    bias_t = jnp.concatenate([bias_t, jnp.zeros((LANES - nb, 2 * t), F32)], axis=0)
    return jnp.concatenate([bias_t[:, _lane_block(n)].T for n in range(2 * t // LANES)], axis=0).astype(BF)


def _moba_kernel(q_ref, k_ref, v_ref, km_ref, ind_ref, o_ref, m_ref, l_ref, acc_ref, *, t, nb):
    qi = pl.program_id(2)
    _init_stats(m_ref, l_ref, acc_ref)
    q_aug = []
    for c in range(CHAINS):
        q2 = _split_halves(q_ref[0, :, _lane_block(c)])
        bias = _moba_select(q2, km_ref[0, :, _lane_block(c)], qi, t, nb)
        q_aug.append(jnp.concatenate([q2, bias], axis=1))

    def tile(j, diagonal, width=t):
        start = pl.multiple_of(j * t, t)
        ind = ind_ref[pl.ds(start, width), :]
        if diagonal:
            row = lax.broadcasted_iota(jnp.int32, (2 * t, t), 0) & (t - 1)
            col = lax.broadcasted_iota(jnp.int32, (2 * t, t), 1)
            future = ((row >> MOBA_SHIFT) == (col >> MOBA_SHIFT)) & (col > row)
        for c in range(CHAINS):
            k_aug = jnp.concatenate([k_ref[0, pl.ds(start, width), _lane_block(c)], ind], axis=1)
            v = v_ref[0, pl.ds(start, width), _lane_block(c)]
            s = lax.dot_general(q_aug[c], k_aug, _NT, preferred_element_type=F32)
            if diagonal:
                s = jnp.where(future, NEG_INF, s)
            _softmax_step(s, v, m_ref.at[c], l_ref.at[c], acc_ref.at[c])

    tile(qi, True)

    def body(jj, carry):
        tile(2 * jj, False, 2 * t)
        return carry

    lax.fori_loop(0, qi // 2, body, 0)

    @pl.when((qi & 1) == 1)
    def _():
        tile(qi - 1, False)

    lane = lax.broadcasted_iota(jnp.int32, (t, LANES), 1)
    for c in range(CHAINS):
        o = _normalised(l_ref.at[c], acc_ref.at[c])
        o_ref[0, :, _lane_block(c)] = jnp.where(lane < HEAD_DIM, o[:t], o[t:]).astype(BF)


def _moba_attention(qkv, kmean, batch, seq, t):
    nb = seq // MOBA_BLOCK
    ns = D_MODEL // LANES // CHAINS
    w = CHAINS * LANES
    qkv3 = qkv.reshape(batch, seq, 3 * D_MODEL)
    km3 = kmean.reshape(batch, nb, D_MODEL)
    ind = _block_indicator(seq, MOBA_SHIFT)
    out = pl.pallas_call(
        functools.partial(_moba_kernel, t=t, nb=nb),
        grid=(batch, ns, seq // t),
        in_specs=[
            pl.BlockSpec((1, t, w), lambda b, p, i: (b, i, p)),
            pl.BlockSpec((1, seq, w), lambda b, p, i: (b, 0, ns + p)),
            pl.BlockSpec((1, seq, w), lambda b, p, i: (b, 0, 2 * ns + p)),
            pl.BlockSpec((1, nb, w), lambda b, p, i: (b, 0, p)),
            pl.BlockSpec(ind.shape, lambda b, p, i: (0, 0)),
        ],
        out_specs=pl.BlockSpec((1, t, w), lambda b, p, i: (b, i, p)),
        out_shape=jax.ShapeDtypeStruct((batch, seq, D_MODEL), BF),
        scratch_shapes=_stats_scratch(CHAINS, 2 * t),
        compiler_params=_cparams(("parallel", "parallel", "arbitrary")),
        name="moba_attention",
    )(qkv3, qkv3, qkv3, km3, ind)
    return out.reshape(batch * seq, D_MODEL)


def _compress_kernel(r_ref, pe_ref, w1_ref, w2_ref, o_ref):
    r = r_ref[0]
    ng = r.shape[0]
    a0 = (r + pe_ref[0]).astype(BF)
    a1 = (r + pe_ref[1]).astype(BF)
    y0 = jnp.dot(a0, w1_ref[0], preferred_element_type=F32)
    y1 = jnp.dot(a1, w1_ref[1], preferred_element_type=F32)
    pre = y0 + pltpu.roll(y1, ng - 1, 0)
    hid = pre * (1.0 / (1.0 + jnp.exp(-pre)))
    o_ref[0] = jnp.dot(hid.astype(BF), w2_ref[...], preferred_element_type=F32).astype(o_ref.dtype)


def _compress(r, pe, w1, w2):
    batch, ng, wd = r.shape
    wout = w2.shape[1]
    return pl.pallas_call(
        _compress_kernel,
        grid=(batch,),
        in_specs=[
            pl.BlockSpec((1, ng, wd), lambda b: (b, 0, 0)),
            _resident(pe.shape),
            _resident(w1.shape),
            _resident(w2.shape),
        ],
        out_specs=pl.BlockSpec((1, ng, wout), lambda b: (b, 0, 0)),
        out_shape=jax.ShapeDtypeStruct((batch, ng, wout), BF),
        compiler_params=_cparams(("parallel",)),
        name="nsa_compress",
    )(r, pe, w1, w2)


def _compress_weights(pe, w1, w2):
    g = NSA_GROUPS

    def block_diag(w):
        c = w.shape[-1]
        lead = [(0, 0)] * (w.ndim - 1)
        return jnp.concatenate([jnp.pad(w, lead + [(k * c, (g - 1 - k) * c)]) for k in range(g)], axis=-2)

    w1r = w1.astype(BF).reshape(2, CMP_STRIDE, HEAD_DIM, CMP_HIDDEN)
    w1b = block_diag(w1r).reshape(2, CMP_STRIDE * g * HEAD_DIM, g * CMP_HIDDEN)
    w2b = block_diag(w2.astype(BF))
    peb = jnp.broadcast_to(pe.reshape(2, CMP_STRIDE, 1, HEAD_DIM), (2, CMP_STRIDE, g, HEAD_DIM))
    return peb.reshape(2, 1, CMP_STRIDE * g * HEAD_DIM), w1b, w2b


def _nsa_stack_q(q, t):
    lane = lax.broadcasted_iota(jnp.int32, (t, LANES), 1)
    zero = jnp.zeros((t, LANES), q.dtype)
    parts = []
    for half in range(2):
        keep = (lane < HEAD_DIM) if half == 0 else (lane >= HEAD_DIM)
        for r in range(4):
            parts.append(jnp.where(keep, q[:, r * LANES:(r + 1) * LANES], zero))
    return jnp.concatenate(parts, axis=0)


def _nsa_write(o, gl, branch, o_ref, t, gp=0):
    lane = lax.broadcasted_iota(jnp.int32, (t, LANES), 1)
    for r in range(4):
        c0 = r * 3 + branch
        c1 = 12 + r * 3 + branch
        g0 = 1.0 / (1.0 + jnp.exp(-gl[:, c0:c0 + 1]))
        g1 = 1.0 / (1.0 + jnp.exp(-gl[:, c1:c1 + 1]))
        blk = jnp.where(lane < HEAD_DIM, g0 * o[r * t:(r + 1) * t], g1 * o[(4 + r) * t:(5 + r) * t])
        o_ref[0, :, _lane_block(4 * gp + r)] = blk.astype(BF)


def _nsa_cmp_kernel(q_ref, kc_ref, vc_ref, gl_ref, ovl_ref, o_ref, sel_ref, *, t, ncmp):
    qi = pl.program_id(2)
    qs = _nsa_stack_q(q_ref[0], t)
    s = lax.dot_general(qs, kc_ref[0], _NT, preferred_element_type=F32)
    n_idx = lax.broadcasted_iota(jnp.int32, s.shape, 1)
    qidx = qi * t + (lax.broadcasted_iota(jnp.int32, s.shape, 0) & (t - 1))
    mask = (n_idx * CMP_STRIDE + (CMP_LEN - 1)) <= qidx
    sm = jnp.where(mask, s, NEG_INF)
    p = jnp.where(mask, jnp.exp2(sm - jnp.max(sm, axis=1, keepdims=True)), 0.0)
    l = jnp.sum(p, axis=1, keepdims=True)
    pn = p * (1.0 / jnp.where(l > 0.0, l, 1.0))
    o = jnp.dot(pn.astype(BF), vc_ref[0], preferred_element_type=F32)
    _nsa_write(o, gl_ref[0], 0, o_ref, t)

    nsel = ovl_ref.shape[0]
    jb = lax.broadcasted_iota(jnp.int32, (nsel, t), 0)
    qblk =(qi * t + lax.broadcasted_iota(jnp.int32, (nsel, t), 1)) >> SLC_SHIFT
    forced = (jb == 0) | (jb == qblk) | (jb == qblk - 1)
    ovl = ovl_ref[...]
    for half in range(2):
        ps = pn[(half * 4) * t:(half * 4 + 1) * t]
        for r in range(1, 4):
            ps = ps + pn[(half * 4 + r) * t:(half * 4 + r + 1) * t]
        ps_hi = ps.astype(BF)
        ps_lo = (ps - ps_hi.astype(F32)).astype(BF)
        imp = (lax.dot_general(ovl, ps_hi, _NT, preferred_element_type=F32)
               + lax.dot_general(ovl, ps_lo, _NT, preferred_element_type=F32))
        val = jnp.where(jb <= qblk, jnp.where(forced, FORCED_SCORE, imp), NEG_INF)
        slabs = [val[g * 8:(g + 1) * 8] for g in range(nsel // 8)]
        row8 = lax.broadcasted_iota(jnp.int32, (8, t), 0)
        cnt = [jnp.zeros((8, t), F32) for _ in slabs]
        for i in range(nsel):
            vi = val[i:i + 1, :]
            for g, sl in enumerate(slabs):
                if g > i // 8:
                    ahead = vi >= sl
                elif g < i // 8:
                    ahead = vi > sl
                else:
                    ahead = (vi > sl) | ((vi == sl) & (row8 > (i % 8)))
                cnt[g] = jnp.where(ahead, cnt[g] + 1.0, cnt[g])
        rank = jnp.concatenate(cnt, axis=0)
        chosen = jnp.where((rank < float(min(SLC_TOPK, nsel))) & (jb <= qblk), 1.0, 0.0)
        pad = jnp.zeros((LANES - nsel, t), F32)
        chosen = jnp.concatenate([chosen, pad], axis=0)
        sel_ref[0, 0, half] = jnp.concatenate([chosen[:, _lane_block(n)].T for n in range(t // LANES)],
                                              axis=0).astype(BF)


def _nsa_sel_kernel(q_ref, k_ref, v_ref, gl_ref, sel_ref, ind_ref, o_ref, m_ref, l_ref, acc_ref, *, t, tk):
    qi = pl.program_id(1)
    q_aug = []
    for gp in range(2):
        qs = _nsa_stack_q(q_ref[0, :, gp * 4 * LANES:(gp + 1) * 4 * LANES], t)
        for half in range(2):
            bias = jnp.where(sel_ref[0, gp, half].astype(F32) > 0.5, 0.0, NEG_INF).astype(BF)
            q_aug.append(jnp.concatenate([qs[half * 4 * t:(half + 1) * 4 * t],
                                          jnp.concatenate([bias] * 4, axis=0)], axis=1))
    _init_stats(m_ref, l_ref, acc_ref)
    jd = (qi * t) // tk

    def tile(j, causal, width=tk):
        start = pl.multiple_of(j * tk, tk)
        ind = ind_ref[pl.ds(start, width), :]
        if causal:
            qidx = qi * t + (lax.broadcasted_iota(jnp.int32, (4 * t, tk), 0) & (t - 1))
            kidx = j * tk + lax.broadcasted_iota(jnp.int32, (4 * t, tk), 1)
        ss = []
        for gp in range(2):
            k_aug = jnp.concatenate([k_ref[0, pl.ds(start, width), _lane_block(gp)], ind], axis=1)
            for half in range(2):
                s = lax.dot_general(q_aug[2 * gp + half], k_aug, _NT, preferred_element_type=F32)
                ss.append(jnp.where(kidx <= qidx, s, NEG_INF) if causal else s)
        for c in range(4):
            v = v_ref[0, pl.ds(start, width), _lane_block(c // 2)]
            _softmax_step(ss[c], v, m_ref.at[c], l_ref.at[c], acc_ref.at[c])

    tile(jd, True)

    def body(jj, carry):
        tile(2 * jj, False, 2 * tk)
        return carry

    lax.fori_loop(0, jd // 2, body, 0)

    @pl.when((jd & 1) == 1)
    def _():
        tile(jd - 1, False)

    for gp in range(2):
        o = jnp.concatenate([_normalised(l_ref.at[2 * gp + half], acc_ref.at[2 * gp + half]) for half in range(2)],
                            axis=0)
        _nsa_write(o, gl_ref[0, :, _lane_block(gp)], 1, o_ref, t, gp)


def _nsa_win_kernel(q_ref, k_ref, v_ref, gl_ref, o_ref, m_ref, l_ref, acc_ref, *, t):
    qi = pl.program_id(1)
    qs = [_nsa_stack_q(q_ref[0, :, gp * 4 * LANES:(gp + 1) * 4 * LANES], t) for gp in range(2)]
    _init_stats(m_ref, l_ref, acc_ref)

    def tile(j, kind):
        start = pl.multiple_of(j * t, t)
        if kind != "full":
            row = lax.broadcasted_iota(jnp.int32, (8 * t, t), 0) & (t - 1)
            col = lax.broadcasted_iota(jnp.int32, (8 * t, t), 1)
            keep = (col <= row) if kind == "causal" else (col > row)
        for gp in range(2):
            k = k_ref[0, pl.ds(start, t), _lane_block(gp)]
            v = v_ref[0, pl.ds(start, t), _lane_block(gp)]
            s = lax.dot_general(qs[gp], k, _NT, preferred_element_type=F32)
            if kind != "full":
                s = jnp.where(keep, s, NEG_INF)
            _softmax_step(s, v, m_ref.at[gp], l_ref.at[gp], acc_ref.at[gp])

    tile(qi, "causal")

    @pl.when(qi >= 1)
    def _():
        tile(qi - 1, "full")

    @pl.when(qi >= 2)
    def _():
        tile(qi - 2, "tail")

    for gp in range(2):
        _nsa_write(_normalised(l_ref.at[gp], acc_ref.at[gp]), gl_ref[0, :, _lane_block(gp)], 2, o_ref, t, gp)


def _nsa_attention(pa, pf, kcmp, vcmp, ovl_t, batch, seq, t, tk):
    pa3 = pa.reshape(batch, seq, pa.shape[1])
    pf3 = pf.reshape(batch, seq, pf.shape[1])
    grid = (batch, 2, seq // t)
    ncmp = kcmp.shape[1]
    sem = _cparams(("parallel", "parallel", "arbitrary"))
    o_shape = jax.ShapeDtypeStruct((batch, seq, D_MODEL), BF)

    def q_specs(rows):
        q_spec = pl.BlockSpec((1, rows, 4 * LANES), lambda b, g, i: (b, i, g))
        gl_spec = pl.BlockSpec((1, rows, LANES), lambda b, g, i: (b, i, g))
        return q_spec, gl_spec, q_spec

    q_spec, gl_spec, o_spec = q_specs(t)
    ind = _block_indicator(seq, SLC_SHIFT)

    def kv_spec(col_block):
        return pl.BlockSpec((1, seq, LANES), lambda b, g, i: (b, 0, col_block + g))

    o_c, sel = pl.pallas_call(
        functools.partial(_nsa_cmp_kernel, t=t, ncmp=ncmp),
        grid=grid,
        in_specs=[
            q_spec,
            pl.BlockSpec((1, ncmp, LANES), lambda b, g, i: (b, 0, g)),
            pl.BlockSpec((1, ncmp, LANES), lambda b, g, i: (b, 0, g)),
            gl_spec,
            pl.BlockSpec(ovl_t.shape, lambda b, g, i: (0, 0)),
        ],
        out_specs=[o_spec, pl.BlockSpec((1, 1, 2, t, LANES), lambda b, g, i: (b, g, 0, i, 0))],
        out_shape=[o_shape, jax.ShapeDtypeStruct((batch, 2, 2, seq, LANES), BF)],
        compiler_params=sem,
        name="nsa_compressed",
    )(pa3, kcmp, vcmp, pf3, ovl_t)

    sem2 = _cparams(("parallel", "arbitrary"))
    kvw = 2 * LANES

    def row_spec(rows):
        return pl.BlockSpec((1, rows, D_MODEL), lambda b, i: (b, i, 0))

    def kv2_spec(col_block):
        return pl.BlockSpec((1, seq, kvw), lambda b, i: (b, 0, col_block))

    def gl2_spec(rows):
        return pl.BlockSpec((1, rows, kvw), lambda b, i: (b, i, 0))

    ts = t
    o_s = pl.pallas_call(
        functools.partial(_nsa_sel_kernel, t=ts, tk=tk),
        grid=(batch, seq // ts),
        in_specs=[row_spec(ts), kv2_spec(4), kv2_spec(6), gl2_spec(ts),
                  pl.BlockSpec((1, 2, 2, ts, LANES), lambda b, i: (b, 0, 0, i, 0)),
                  pl.BlockSpec(ind.shape, lambda b, i: (0, 0))],
        out_specs=row_spec(ts),
        out_shape=o_shape,
        scratch_shapes=_stats_scratch(4, 4 * ts),
        compiler_params=sem2,
        name="nsa_selected",
    )(pa3, pa3, pa3, pf3, sel, ind)

    tw = WINDOW // 2
    o_w = pl.pallas_call(
        functools.partial(_nsa_win_kernel, t=tw),
        grid=(batch, seq // tw),
        in_specs=[row_spec(tw), kv2_spec(5), kv2_spec(7), gl2_spec(tw)],
        out_specs=row_spec(tw),
        out_shape=o_shape,
        scratch_shapes=_stats_scratch(2, 8 * tw),
        compiler_params=sem2,
        name="nsa_window",
    )(pa3, pa3, pa3, pf3)
    n = batch * seq
    return [o_c.reshape(n, D_MODEL), o_s.reshape(n, D_MODEL), o_w.reshape(n, D_MODEL)]


def _rope_tables(seq):
    half = ROT_DIM // 2
    inv_freq = ROPE_THETA ** (-jnp.arange(half, dtype=F32) / half)
    ang = jnp.arange(seq).astype(F32)[:, None] * inv_freq[None, :]
    cos, sin = jnp.cos(ang), jnp.sin(ang)
    d = np.arange(LANES) % HEAD_DIM
    idx = d % half
    cos_t = jnp.where(d[None, :] < ROT_DIM, cos[:, idx], 1.0)
    sa_t = jnp.where(d[None, :] < half, -sin[:, idx], 0.0)
    sb_t = jnp.where((d[None, :] >= half) & (d[None, :] < ROT_DIM), sin[:, idx], 0.0)
    return cos_t.astype(F32), sa_t.astype(F32), sb_t.astype(F32)


def _block_indicator(seq, shift):
    blk = np.arange(seq)[:, None] >> shift
    return jnp.asarray((blk == np.arange(LANES)[None, :]).astype(np.float32), BF)


def _nsa_q_perm():
    cols = []
    for gp in range(2):
        for r in range(4):
            for half in range(2):
                head = 8 * gp + 4 * half + r
                cols.extend(range(head * HEAD_DIM, (head + 1) * HEAD_DIM))
    return np.asarray(cols, np.int32)


def _nsa_gate_cols():
    src = -np.ones(2 * LANES, np.int32)
    for gp in range(2):
        for half in range(2):
            for r in range(4):
                for br in range(3):
                    src[gp * LANES + half * 12 + r * 3 + br] = (4 * (2 * gp + half) + r) * 3 + br
    return src


def kernel(x, norm_g, ffn_w_in, ffn_w_out, diff_w_in, diff_w_out, diff_lambda, diff_subln, moba_w_in, moba_w_out,
           nsa_w_in, nsa_w_out, nsa_cmp_pe, nsa_cmp_w1, nsa_cmp_w2):
    batch, seq, d = x.shape
    n = batch * seq
    tm = 512
    assert d == D_MODEL and seq % 512 == 0 and seq // SLC_BLOCK <= LANES and seq // MOBA_BLOCK >= 8
    h = x.reshape(n, d)
    tables = _rope_tables(seq)
    wi_all = ffn_w_in.astype(BF)
    wo_all = ffn_w_out.astype(BF)

    for i in range(DEPTH):
        g = norm_g[i]
        h = _ffn(h, g[0:2], wi_all, wo_all, i, 0, tm)

        kind, j = i % 3, i // 3
        if kind == 0:
            lambda_init = 0.8 - 0.6 * math.exp(-0.3 * i)
            plan = [(c * 256, 256, c < 8, 0, c * 256, None, Q_SCALE if c < 4 else None) for c in range(12)]
            (qkv,) = _proj(h, g[2:3], diff_w_in[j].astype(BF), tables, plan, [(3 * D_MODEL, BF, 1)], tm, seq)
            attn = _diff_attention(qkv, diff_lambda[j], diff_subln[j].reshape(1, LANES), batch, seq, 512,
                                   lambda_init)
            parts, w_out = [attn], diff_w_out[j]
        elif kind == 1:
            plan = [(c * 256, 256, c < 8, 0, c * 256, (c - 4) * 256 if 4 <= c < 8 else None,
                     Q_SCALE if c < 4 else None) for c in range(12)]
            qkv, kmean = _proj(h, g[2:3], moba_w_in[j].astype(BF), tables, plan, [(3 * D_MODEL, BF, 1)], tm, seq,
                               kmean_width=D_MODEL)
            attn = _moba_attention(qkv, kmean, batch, seq, 512)
            parts, w_out = [attn], moba_w_out[j]
        else:
            w = nsa_w_in[j]
            perm = _nsa_q_perm()
            kvw = NSA_GROUPS * HEAD_DIM
            base = NSA_HEADS * HEAD_DIM
            seg = {name: w[:, base + k * kvw: base + (k + 1) * kvw]
                   for k, name in enumerate(["kc", "vc", "ks", "vs", "kw", "vw"])}
            gsrc = _nsa_gate_cols()
            glog = w[:, base + 6 * kvw:]
            gate_w = jnp.where(gsrc[None, :] >= 0, glog[:, np.maximum(gsrc, 0)], 0.0)
            w_all = jnp.concatenate([w[:, perm], seg["ks"], seg["kw"], seg["kc"], seg["vs"], seg["vw"], seg["vc"],
                                     gate_w], axis=1).astype(BF)
            plan = [(c * 256, 256, True, 0, c * 256, None, Q_SCALE if c < 4 else None)
                    for c in range(6)]
            plan.append((1536, 256, True, 1, 0, None, None))
            plan.append((1792, 256, False, 0, 1536, None, None))
            plan.append((2048, 256, False, 0, 1792, None, None))
            plan.append((2304, 256, False, 2, 0, None, None))
            plan.append((2560, 256, False, 3, 0, None, None))
            pa, kc, vc, pf = _proj(h, g[2:3], w_all, tables, plan,
                                   [(2048, BF, 1), (kvw, F32, CMP_STRIDE), (kvw, F32, CMP_STRIDE),
                                    (2 * LANES, F32, 1)], tm, seq)

            ng = seq // CMP_STRIDE
            cmp_tm = []
            for a, src in enumerate((kc, vc)):
                pe, w1, w2 = _compress_weights(nsa_cmp_pe[j][a], nsa_cmp_w1[j][a], nsa_cmp_w2[j][a])
                cmp_tm.append(_compress(src.reshape(batch, ng, CMP_STRIDE * kvw), pe, w1, w2))

            nsel = seq // SLC_BLOCK
            cs = np.arange(ng)[:, None] * CMP_STRIDE
            bs = np.arange(nsel)[None, :] * SLC_BLOCK
            ovl = ((cs < bs + SLC_BLOCK) & (cs + CMP_LEN > bs)).astype(np.float32)
            ovl[ng - 1, :] = 0.0
            ovl_t = jnp.asarray(ovl.T, BF)
            parts = _nsa_attention(pa, pf, cmp_tm[0], cmp_tm[1], ovl_t, batch, seq, 256, 512)
            w_out = nsa_w_out[j][perm, :]

        h = _ffn(h, g[4:6], wi_all, wo_all, i, 1, tm, parts=parts, wp=w_out.astype(BF), gp=g[3:4])
    return h.reshape(batch, seq, d)
```

```python
import functools
import math

import jax
import jax.numpy as jnp
import numpy as np
from jax import lax
from jax.experimental import pallas as pl
from jax.experimental.pallas import tpu as pltpu

D_MODEL = 1024
DEPTH = 4
HEAD_DIM = 64
ROT_DIM = HEAD_DIM // 4
ROPE_THETA = 500000.0
NORM_EPS = 1e-6
NEG_INF = -1e30
REMOVED = -3e38

DIFF_HEADS = 8
DIFF_SUBLN_EPS = 1e-5
MOBA_BLOCK = 256
MOBA_SHIFT = 8
MOBA_TOPK = 3
NSA_HEADS = 16
NSA_GROUPS = 4
CMP_LEN = 32
CMP_STRIDE = 16
CMP_HIDDEN = 256
SLC_BLOCK = 64
SLC_SHIFT = 6
SLC_TOPK = 16
WINDOW = 512
FORCED_SCORE = 1e9
D_FF = 2816

LANES = 128
FF_CHUNK = 256
CHAINS = 4
Q_SCALE = HEAD_DIM ** -0.5 * math.log2(math.e)

BF = jnp.bfloat16
F32 = jnp.float32
VMEM_LIMIT = 56 * 1024 * 1024

_NT = (((1,), (1,)), ((), ()))


def _cparams(sem):
    return pltpu.CompilerParams(dimension_semantics=sem, vmem_limit_bytes=VMEM_LIMIT)


def _rms(x, g, eps):
    return x * lax.rsqrt(jnp.mean(x * x, axis=-1, keepdims=True) + eps) * g


def _resident(shape):
    nd = len(shape)
    return pl.BlockSpec(shape, lambda *_: (0,) * nd, pipeline_mode=pl.Buffered(1))


def _ffn_kernel(*refs, n_parts):
    h_ref = refs[0]
    parts = refs[1:1 + n_parts]
    if n_parts:
        wp_ref, gp_ref = refs[1 + n_parts:3 + n_parts]
        refs = refs[3 + n_parts:]
    else:
        refs = refs[1:]
    g_ref, wi_ref, wo_ref, o_ref, acc_ref = refs
    h = h_ref[...]
    if n_parts:
        a = parts[0][...]
        if n_parts > 1:
            a = a.astype(F32)
            for p in parts[1:]:
                a = a + p[...].astype(F32)
            a = a.astype(BF)
        h = h + _rms(jnp.dot(a, wp_ref[...], preferred_element_type=F32), gp_ref[...], NORM_EPS)
    xn = _rms(h, g_ref[0:1, :], NORM_EPS).astype(BF)
    for c in range(D_FF // FF_CHUNK):
        lo = c * FF_CHUNK
        gate = jnp.dot(xn, wi_ref[:, lo:lo + FF_CHUNK], preferred_element_type=F32)
        up = jnp.dot(xn, wi_ref[:, D_FF + lo:D_FF + lo + FF_CHUNK], preferred_element_type=F32)
        act = ((gate * (1.0 / (1.0 + jnp.exp(-gate)))) * up).astype(BF)
        part = jnp.dot(act, wo_ref[lo:lo + FF_CHUNK, :], preferred_element_type=F32)
        if c == 0:
            acc_ref[...] = part
        else:
            acc_ref[...] += part
    o_ref[...] = h + 0.5 * _rms(acc_ref[...], g_ref[1:2, :], NORM_EPS)


def _ffn(h, g2, wi_all, wo_all, layer, which, tm, parts=(), wp=None, gp=None):
    n, d = h.shape
    row = pl.BlockSpec((tm, d), lambda i: (i, 0))
    proj_args = [wp, gp] if parts else []

    def picked(w):
        return pl.BlockSpec((None, None) + w.shape[2:], lambda i: (layer, which, 0, 0), pipeline_mode=pl.Buffered(1))

    return pl.pallas_call(
        functools.partial(_ffn_kernel, n_parts=len(parts)),
        grid=(n // tm,),
        in_specs=([row] * (1 + len(parts)) + [_resident(a.shape) for a in proj_args]
                  + [_resident(g2.shape), picked(wi_all), picked(wo_all)]),
        out_specs=row,
        out_shape=jax.ShapeDtypeStruct((n, d), F32),
        scratch_shapes=[pltpu.VMEM((tm, d), F32)],
        compiler_params=_cparams(("parallel",)),
        name="ffn_halfstep",
    )(h, *parts, *proj_args, g2, wi_all, wo_all)


def _proj_kernel(h_ref, g_ref, w_ref, cos_ref, sa_ref, sb_ref, *refs, plan, tm, n_out):
    out_refs, slab_ref = refs[:-1], refs[-1]
    xn = _rms(h_ref[...], g_ref[...], NORM_EPS).astype(BF)
    cos = cos_ref[...]
    sa = sa_ref[...]
    sb = sb_ref[...]
    for col, width, rope, dest, off, km_off, scale in plan:
        y = jnp.dot(xn, w_ref[:, col:col + width], preferred_element_type=F32)
        for k in range(width // LANES):
            yk = y[:, k * LANES:(k + 1) * LANES]
            if rope:
                yk = yk * cos + pltpu.roll(yk, LANES - ROT_DIM // 2, 1) * sa + pltpu.roll(yk, ROT_DIM // 2, 1) * sb
            if scale is not None:
                yk = yk * scale
            o_ref = out_refs[dest]
            if o_ref.shape[0] == tm:
                o_ref[:, off + k * LANES:off + (k + 1) * LANES] = yk.astype(o_ref.dtype)
            else:
                grp = tm // o_ref.shape[0]
                slab_ref[...] = yk
                for tok in range(grp):
                    lo = tok * width + off + k * LANES
                    o_ref[:, lo:lo + LANES] = slab_ref[pl.ds(tok, tm // grp, stride=grp), :].astype(o_ref.dtype)
            if km_off is not None:
                km_ref = out_refs[n_out]
                for r in range(tm // MOBA_BLOCK):
                    blk = yk[r * MOBA_BLOCK:(r + 1) * MOBA_BLOCK, :]
                    km_ref[0, r:r + 1, km_off + k * LANES:km_off + (k + 1) * LANES] = jnp.mean(
                        blk, axis=0, keepdims=True)


def _proj(h, g, w, tables, plan, outs, tm, seq, kmean_width=None):
    n, d = h.shape
    per_seq = seq // tm
    out_shape = [jax.ShapeDtypeStruct((n // gr, wd * gr), dt) for wd, dt, gr in outs]
    out_specs = [pl.BlockSpec((tm // gr, wd * gr), lambda i: (i, 0)) for wd, _, gr in outs]
    if kmean_width is not None:
        nb = tm // MOBA_BLOCK
        out_shape.append(jax.ShapeDtypeStruct((n // tm, nb, kmean_width), F32))
        out_specs.append(pl.BlockSpec((1, nb, kmean_width), lambda i: (i, 0, 0)))
    tab_spec = pl.BlockSpec((tm, LANES), lambda i: (i % per_seq, 0))
    return pl.pallas_call(
        functools.partial(_proj_kernel, plan=tuple(plan), tm=tm, n_out=len(outs)),
        grid=(n // tm,),
        in_specs=[
            pl.BlockSpec((tm, d), lambda i: (i, 0)),
            _resident(g.shape),
            _resident(w.shape),
            tab_spec, tab_spec, tab_spec,
        ],
        out_specs=out_specs,
        out_shape=out_shape,
        scratch_shapes=[pltpu.VMEM((tm, LANES), F32)],
        compiler_params=_cparams(("parallel",)),
        name="norm_proj",
    )(h, g, w, *tables)


def _softmax_step(s, v, m_ref, l_ref, acc_ref):
    m_prev = m_ref[...]
    m_new = jnp.maximum(m_prev, jnp.max(s, axis=1, keepdims=True))
    alpha = jnp.exp2(m_prev - m_new)
    ps = [jnp.exp2(s[:, c * LANES:(c + 1) * LANES] - m_new) for c in range(s.shape[1] // LANES)]
    lsum = ps[0]
    for p in ps[1:]:
        lsum = lsum + p
    l_ref[...] = alpha * l_ref[...] + lsum
    p = jnp.concatenate([x.astype(BF) for x in ps], axis=1)
    acc_ref[...] = alpha * acc_ref[...] + jnp.dot(p, v, preferred_element_type=F32)
    m_ref[...] = m_new


def _init_stats(m_ref, l_ref, acc_ref):
    m_ref[...] = jnp.full(m_ref.shape, NEG_INF, F32)
    l_ref[...] = jnp.zeros(l_ref.shape, F32)
    acc_ref[...] = jnp.zeros(acc_ref.shape, F32)


def _normalised(l_ref, acc_ref):
    return acc_ref[...] * (1.0 / jnp.sum(l_ref[...], axis=1, keepdims=True))


def _stats_scratch(chains, rows):
    return [pltpu.VMEM((chains, rows, LANES), F32) for _ in range(3)]


def _split_halves(q):
    lane = lax.broadcasted_iota(jnp.int32, q.shape, 1)
    zero = jnp.zeros_like(q)
    return jnp.concatenate([jnp.where(lane < HEAD_DIM, q, zero), jnp.where(lane >= HEAD_DIM, q, zero)], axis=0)


def _lane_block(c):
    return slice(c * LANES, (c + 1) * LANES)


def _diff_kernel(q_ref, k_ref, v_ref, lam_ref, sg_ref, o_ref, m_ref, l_ref, acc_ref, *, t, lambda_init):
    qi = pl.program_id(2)
    qs = [_split_halves(q_ref[0, :, _lane_block(c)]) for c in range(CHAINS)]
    _init_stats(m_ref, l_ref, acc_ref)

    def tile(j, causal, width=t):
        start = pl.multiple_of(j * t, t)
        ss = [lax.dot_general(qs[c], k_ref[0, pl.ds(start, width), _lane_block(c)], _NT,
                              preferred_element_type=F32) for c in range(CHAINS)]
        if causal:
            row = lax.broadcasted_iota(jnp.int32, (2 * t, t), 0) & (t - 1)
            col = lax.broadcasted_iota(jnp.int32, (2 * t, t), 1)
            ss = [jnp.where(col <= row, s, NEG_INF) for s in ss]
        for c in range(CHAINS):
            v = v_ref[0, pl.ds(start, width), _lane_block(c)]
            _softmax_step(ss[c], v, m_ref.at[c], l_ref.at[c], acc_ref.at[c])

    tile(qi, True)

    def body(jj, carry):
        tile(2 * jj, False, 2 * t)
        return carry

    lax.fori_loop(0, qi // 2, body, 0)

    @pl.when((qi & 1) == 1)
    def _():
        tile(qi - 1, False)

    lam = lam_ref[...]
    lam_full = (jnp.exp(jnp.sum(lam[0:1] * lam[1:2], axis=1, keepdims=True))
                - jnp.exp(jnp.sum(lam[2:3] * lam[3:4], axis=1, keepdims=True)) + lambda_init)
    for c in range(CHAINS):
        o = _normalised(l_ref.at[c], acc_ref.at[c])
        od = o[:t] - lam_full * o[t:]
        od = _rms(od, sg_ref[...], DIFF_SUBLN_EPS) * (1.0 - lambda_init)
        o_ref[0, :, _lane_block(c)] = od.astype(BF)


def _diff_attention(qkv, lam, subln, batch, seq, t, lambda_init):
    nh = DIFF_HEADS
    ns = nh // CHAINS
    w = CHAINS * LANES
    qkv3 = qkv.reshape(batch, seq, 3 * nh * LANES)
    out = pl.pallas_call(
        functools.partial(_diff_kernel, t=t, lambda_init=lambda_init),
        grid=(batch, ns, seq // t),
        in_specs=[
            pl.BlockSpec((1, t, w), lambda b, h, i: (b, i, h)),
            pl.BlockSpec((1, seq, w), lambda b, h, i: (b, 0, ns + h)),
            pl.BlockSpec((1, seq, w), lambda b, h, i: (b, 0, 2 * ns + h)),
            pl.BlockSpec(lam.shape, lambda b, h, i: (0, 0)),
            pl.BlockSpec(subln.shape, lambda b, h, i: (0, 0)),
        ],
        out_specs=pl.BlockSpec((1, t, w), lambda b, h, i: (b, i, h)),
        out_shape=jax.ShapeDtypeStruct((batch, seq, nh * LANES), BF),
        scratch_shapes=_stats_scratch(CHAINS, 2 * t),
        compiler_params=_cparams(("parallel", "parallel", "arbitrary")),
        name="diff_attention",
    )(qkv3, qkv3, qkv3, lam, subln)
    return out.reshape(batch * seq, nh * LANES)


def _moba_select(q2, km, qi, t, nb):
    km_hi = km.astype(BF)
    km_lo = (km - km_hi.astype(F32)).astype(BF)
    gate = (lax.dot_general(km_hi, q2, _NT, preferred_element_type=F32)
            + lax.dot_general(km_lo, q2, _NT, preferred_element_type=F32))
    blk = lax.broadcasted_iota(jnp.int32, gate.shape, 0)
    blk_f = blk.astype(F32)
    own = (qi * t + (lax.broadcasted_iota(jnp.int32, gate.shape, 1) & (t - 1))) >> MOBA_SHIFT
    gate = jnp.where(blk < own, gate, NEG_INF)
    sel = jnp.zeros(gate.shape, F32)
    for _ in range(min(MOBA_TOPK, nb)):
        mx = jnp.max(gate, axis=0, keepdims=True)
        idx = jnp.min(jnp.where(gate == mx, blk_f, float(nb)), axis=0, keepdims=True)
        hit = blk_f == idx
        sel = jnp.where(hit & (mx > 0.5 * NEG_INF), 1.0, sel)
        gate = jnp.where(hit, REMOVED, gate)
    bias_t = jnp.where((sel > 0.5) | (blk == own), 0.0, NEG_INF)
    bias_t = jnp.concatenate([bias_t, jnp.zeros((LANES - nb, 2 * t), F32)], axis=0)
    return jnp.concatenate([bias_t[:, _lane_block(n)].T for n in range(2 * t // LANES)], axis=0).astype(BF)


def _moba_kernel(q_ref, k_ref, v_ref, km_ref, ind_ref, o_ref, m_ref, l_ref, acc_ref, *, t, nb):
    qi = pl.program_id(2)
    _init_stats(m_ref, l_ref, acc_ref)
    q_aug = []
    for c in range(CHAINS):
        q2 = _split_halves(q_ref[0, :, _lane_block(c)])
        bias = _moba_select(q2, km_ref[0, :, _lane_block(c)], qi, t, nb)
        q_aug.append(jnp.concatenate([q2, bias], axis=1))

    def tile(j, diagonal, width=t):
        start = pl.multiple_of(j * t, t)
        ind = ind_ref[pl.ds(start, width), :]
        if diagonal:
            row = lax.broadcasted_iota(jnp.int32, (2 * t, t), 0) & (t - 1)
            col = lax.broadcasted_iota(jnp.int32, (2 * t, t), 1)
            future = ((row >> MOBA_SHIFT) == (col >> MOBA_SHIFT)) & (col > row)
        for c in range(CHAINS):
            k_aug = jnp.concatenate([k_ref[0, pl.ds(start, width), _lane_block(c)], ind], axis=1)
            v = v_ref[0, pl.ds(start, width), _lane_block(c)]
            s = lax.dot_general(q_aug[c], k_aug, _NT, preferred_element_type=F32)
            if diagonal:
                s = jnp.where(future, NEG_INF, s)
            _softmax_step(s, v, m_ref.at[c], l_ref.at[c], acc_ref.at[c])

    tile(qi, True)

    def body(jj, carry):
        tile(2 * jj, False, 2 * t)
        return carry

    lax.fori_loop(0, qi // 2, body, 0)

    @pl.when((qi & 1) == 1)
    def _():
        tile(qi - 1, False)

    lane = lax.broadcasted_iota(jnp.int32, (t, LANES), 1)
    for c in range(CHAINS):
        o = _normalised(l_ref.at[c], acc_ref.at[c])
        o_ref[0, :, _lane_block(c)] = jnp.where(lane < HEAD_DIM, o[:t], o[t:]).astype(BF)


def _moba_attention(qkv, kmean, batch, seq, t):
    nb = seq // MOBA_BLOCK
    ns = D_MODEL // LANES // CHAINS
    w = CHAINS * LANES
    qkv3 = qkv.reshape(batch, seq, 3 * D_MODEL)
    km3 = kmean.reshape(batch, nb, D_MODEL)
    ind = _block_indicator(seq, MOBA_SHIFT)
    out = pl.pallas_call(
        functools.partial(_moba_kernel, t=t, nb=nb),
        grid=(batch, ns, seq // t),
        in_specs=[
            pl.BlockSpec((1, t, w), lambda b, p, i: (b, i, p)),
            pl.BlockSpec((1, seq, w), lambda b, p, i: (b, 0, ns + p)),
            pl.BlockSpec((1, seq, w), lambda b, p, i: (b, 0, 2 * ns + p)),
            pl.BlockSpec((1, nb, w), lambda b, p, i: (b, 0, p)),
            pl.BlockSpec(ind.shape, lambda b, p, i: (0, 0)),
        ],
        out_specs=pl.BlockSpec((1, t, w), lambda b, p, i: (b, i, p)),
        out_shape=jax.ShapeDtypeStruct((batch, seq, D_MODEL), BF),
        scratch_shapes=_stats_scratch(CHAINS, 2 * t),
        compiler_params=_cparams(("parallel", "parallel", "arbitrary")),
        name="moba_attention",
    )(qkv3, qkv3, qkv3, km3, ind)
    return out.reshape(batch * seq, D_MODEL)


def _compress_kernel(r_ref, pe_ref, w1_ref, w2_ref, o_ref):
    r = r_ref[0]
    ng = r.shape[0]
    a0 = (r + pe_ref[0]).astype(BF)
    a1 = (r + pe_ref[1]).astype(BF)
    y0 = jnp.dot(a0, w1_ref[0], preferred_element_type=F32)
    y1 = jnp.dot(a1, w1_ref[1], preferred_element_type=F32)
    pre = y0 + pltpu.roll(y1, ng - 1, 0)
    hid = pre * (1.0 / (1.0 + jnp.exp(-pre)))
    o_ref[0] = jnp.dot(hid.astype(BF), w2_ref[...], preferred_element_type=F32).astype(o_ref.dtype)


def _compress(r, pe, w1, w2):
    batch, ng, wd = r.shape
    wout = w2.shape[1]
    return pl.pallas_call(
        _compress_kernel,
        grid=(batch,),
        in_specs=[
            pl.BlockSpec((1, ng, wd), lambda b: (b, 0, 0)),
            _resident(pe.shape),
            _resident(w1.shape),
            _resident(w2.shape),
        ],
        out_specs=pl.BlockSpec((1, ng, wout), lambda b: (b, 0, 0)),
        out_shape=jax.ShapeDtypeStruct((batch, ng, wout), BF),
        compiler_params=_cparams(("parallel",)),
        name="nsa_compress",
    )(r, pe, w1, w2)


def _compress_weights(pe, w1, w2):
    g = NSA_GROUPS

    def block_diag(w):
        c = w.shape[-1]
        lead = [(0, 0)] * (w.ndim - 1)
        return jnp.concatenate([jnp.pad(w, lead + [(k * c, (g - 1 - k) * c)]) for k in range(g)], axis=-2)

    w1r = w1.astype(BF).reshape(2, CMP_STRIDE, HEAD_DIM, CMP_HIDDEN)
    w1b = block_diag(w1r).reshape(2, CMP_STRIDE * g * HEAD_DIM, g * CMP_HIDDEN)
    w2b = block_diag(w2.astype(BF))
    peb = jnp.broadcast_to(pe.reshape(2, CMP_STRIDE, 1, HEAD_DIM), (2, CMP_STRIDE, g, HEAD_DIM))
    return peb.reshape(2, 1, CMP_STRIDE * g * HEAD_DIM), w1b, w2b


def _nsa_stack_q(q, t):
    lane = lax.broadcasted_iota(jnp.int32, (t, LANES), 1)
    zero = jnp.zeros((t, LANES), q.dtype)
    parts = []
    for half in range(2):
        keep = (lane < HEAD_DIM) if half == 0 else (lane >= HEAD_DIM)
        for r in range(4):
            parts.append(jnp.where(keep, q[:, r * LANES:(r + 1) * LANES], zero))
    return jnp.concatenate(parts, axis=0)


def _nsa_write(o, gl, branch, o_ref, t, gp=0):
    lane = lax.broadcasted_iota(jnp.int32, (t, LANES), 1)
    for r in range(4):
        c0 = r * 3 + branch
        c1 = 12 + r * 3 + branch
        g0 = 1.0 / (1.0 + jnp.exp(-gl[:, c0:c0 + 1]))
        g1 = 1.0 / (1.0 + jnp.exp(-gl[:, c1:c1 + 1]))
        blk = jnp.where(lane < HEAD_DIM, g0 * o[r * t:(r + 1) * t], g1 * o[(4 + r) * t:(5 + r) * t])
        o_ref[0, :, _lane_block(4 * gp + r)] = blk.astype(BF)


def _nsa_cmp_body(qi, q, kc, vc, gl, ovl_ref, o_ref, sel_ref, gp, t):
    qs = _nsa_stack_q(q, t)
    s = lax.dot_general(qs, kc, _NT, preferred_element_type=F32)
    n_idx = lax.broadcasted_iota(jnp.int32, s.shape, 1)
    qidx = qi * t + (lax.broadcasted_iota(jnp.int32, s.shape, 0) & (t - 1))
    mask = (n_idx * CMP_STRIDE + (CMP_LEN - 1)) <= qidx
    sm = jnp.where(mask, s, NEG_INF)
    p = jnp.where(mask, jnp.exp2(sm - jnp.max(sm, axis=1, keepdims=True)), 0.0)
    l = jnp.sum(p, axis=1, keepdims=True)
    pn = p * (1.0 / jnp.where(l > 0.0, l, 1.0))
    o = jnp.dot(pn.astype(BF), vc, preferred_element_type=F32)
    _nsa_write(o, gl, 0, o_ref, t, gp)

    nsel = ovl_ref.shape[0]
    jb = lax.broadcasted_iota(jnp.int32, (nsel, t), 0)
    qblk =(qi * t + lax.broadcasted_iota(jnp.int32, (nsel, t), 1)) >> SLC_SHIFT
    forced = (jb == 0) | (jb == qblk) | (jb == qblk - 1)
    ovl = ovl_ref[...]
    for half in range(2):
        ps = pn[(half * 4) * t:(half * 4 + 1) * t]
        for r in range(1, 4):
            ps = ps + pn[(half * 4 + r) * t:(half * 4 + r + 1) * t]
        ps_hi = ps.astype(BF)
        ps_lo = (ps - ps_hi.astype(F32)).astype(BF)
        imp = (lax.dot_general(ovl, ps_hi, _NT, preferred_element_type=F32)
               + lax.dot_general(ovl, ps_lo, _NT, preferred_element_type=F32))
        val = jnp.where(jb <= qblk, jnp.where(forced, FORCED_SCORE, imp), NEG_INF)
        slabs = [val[g * 8:(g + 1) * 8] for g in range(nsel // 8)]
        row8 = lax.broadcasted_iota(jnp.int32, (8, t), 0)
        cnt = [jnp.zeros((8, t), F32) for _ in slabs]
        for i in range(nsel):
            vi = val[i:i + 1, :]
            for g, sl in enumerate(slabs):
                if g > i // 8:
                    ahead = vi >= sl
                elif g < i // 8:
                    ahead = vi > sl
                else:
                    ahead = (vi > sl) | ((vi == sl) & (row8 > (i % 8)))
                cnt[g] = jnp.where(ahead, cnt[g] + 1.0, cnt[g])
        rank = jnp.concatenate(cnt, axis=0)
        chosen = jnp.where((rank < float(min(SLC_TOPK, nsel))) & (jb <= qblk), 1.0, 0.0)
        pad = jnp.zeros((LANES - nsel, t), F32)
        chosen = jnp.concatenate([chosen, pad], axis=0)
        sel_ref[0, gp, half] = jnp.concatenate([chosen[:, _lane_block(n)].T for n in range(t // LANES)],
                                               axis=0).astype(BF)


def _nsa_sel_kernel(q_ref, k_ref, v_ref, gl_ref, sel_ref, ind_ref, o_ref, m_ref, l_ref, acc_ref, *, t, tk):
    qi = pl.program_id(1)
    q_aug = []
    for gp in range(2):
        qs = _nsa_stack_q(q_ref[0, :, gp * 4 * LANES:(gp + 1) * 4 * LANES], t)
        for half in range(2):
            bias = jnp.where(sel_ref[0, gp, half].astype(F32) > 0.5, 0.0, NEG_INF).astype(BF)
            q_aug.append(jnp.concatenate([qs[half * 4 * t:(half + 1) * 4 * t],
                                          jnp.concatenate([bias] * 4, axis=0)], axis=1))
    _init_stats(m_ref, l_ref, acc_ref)
    jd = (qi * t) // tk

    def tile(j, causal, width=tk):
        start = pl.multiple_of(j * tk, tk)
        ind = ind_ref[pl.ds(start, width), :]
        if causal:
            qidx = qi * t + (lax.broadcasted_iota(jnp.int32, (4 * t, tk), 0) & (t - 1))
            kidx = j * tk + lax.broadcasted_iota(jnp.int32, (4 * t, tk), 1)
        ss = []
        for gp in range(2):
            k_aug = jnp.concatenate([k_ref[0, pl.ds(start, width), _lane_block(gp)], ind], axis=1)
            for half in range(2):
                s = lax.dot_general(q_aug[2 * gp + half], k_aug, _NT, preferred_element_type=F32)
                ss.append(jnp.where(kidx <= qidx, s, NEG_INF) if causal else s)
        for c in range(4):
            v = v_ref[0, pl.ds(start, width), _lane_block(c // 2)]
            _softmax_step(ss[c], v, m_ref.at[c], l_ref.at[c], acc_ref.at[c])

    tile(jd, True)

    def body(jj, carry):
        tile(2 * jj, False, 2 * tk)
        return carry

    lax.fori_loop(0, jd // 2, body, 0)

    @pl.when((jd & 1) == 1)
    def _():
        tile(jd - 1, False)

    for gp in range(2):
        o = jnp.concatenate([_normalised(l_ref.at[2 * gp + half], acc_ref.at[2 * gp + half]) for half in range(2)],
                            axis=0)
        _nsa_write(o, gl_ref[0, :, _lane_block(gp)], 1, o_ref, t, gp)


def _nsa_cmp_win_kernel(q_ref, kc_ref, vc_ref, k_ref, v_ref, gl_ref, ovl_ref, oc_ref, sel_ref, o_ref,
                        m_ref, l_ref, acc_ref, *, t):
    qi = pl.program_id(1)
    for gp in range(2):
        _nsa_cmp_body(qi, q_ref[0, :, gp * 4 * LANES:(gp + 1) * 4 * LANES], kc_ref[0, :, _lane_block(gp)],
                      vc_ref[0, :, _lane_block(gp)], gl_ref[0, :, _lane_block(gp)], ovl_ref, oc_ref, sel_ref, gp, t)
    qs = [_nsa_stack_q(q_ref[0, :, gp * 4 * LANES:(gp + 1) * 4 * LANES], t) for gp in range(2)]
    _init_stats(m_ref, l_ref, acc_ref)

    def tile(j, kind):
        start = pl.multiple_of(j * t, t)
        if kind != "full":
            row = lax.broadcasted_iota(jnp.int32, (8 * t, t), 0) & (t - 1)
            col = lax.broadcasted_iota(jnp.int32, (8 * t, t), 1)
            keep = (col <= row) if kind == "causal" else (col > row)
        for gp in range(2):
            k = k_ref[0, pl.ds(start, t), _lane_block(gp)]
            v = v_ref[0, pl.ds(start, t), _lane_block(gp)]
            s = lax.dot_general(qs[gp], k, _NT, preferred_element_type=F32)
            if kind != "full":
                s = jnp.where(keep, s, NEG_INF)
            _softmax_step(s, v, m_ref.at[gp], l_ref.at[gp], acc_ref.at[gp])

    tile(qi, "causal")

    @pl.when(qi >= 1)
    def _():
        tile(qi - 1, "full")

    @pl.when(qi >= 2)
    def _():
        tile(qi - 2, "tail")

    for gp in range(2):
        _nsa_write(_normalised(l_ref.at[gp], acc_ref.at[gp]), gl_ref[0, :, _lane_block(gp)], 2, o_ref, t, gp)


def _nsa_attention(pa, pf, kcmp, vcmp, ovl_t, batch, seq, t, tk):
    pa3 = pa.reshape(batch, seq, pa.shape[1])
    pf3 = pf.reshape(batch, seq, pf.shape[1])
    ncmp = kcmp.shape[1]
    o_shape = jax.ShapeDtypeStruct((batch, seq, D_MODEL), BF)
    ind = _block_indicator(seq, SLC_SHIFT)

    sem2 = _cparams(("parallel", "arbitrary"))
    kvw = 2 * LANES
    assert t == WINDOW // 2

    def row_spec(rows):
        return pl.BlockSpec((1, rows, D_MODEL), lambda b, i: (b, i, 0))

    def kv2_spec(col_block):
        return pl.BlockSpec((1, seq, kvw), lambda b, i: (b, 0, col_block))

    def gl2_spec(rows):
        return pl.BlockSpec((1, rows, kvw), lambda b, i: (b, i, 0))

    cmp_spec = pl.BlockSpec((1, ncmp, kvw), lambda b, i: (b, 0, 0))
    sel_spec = pl.BlockSpec((1, 2, 2, t, LANES), lambda b, i: (b, 0, 0, i, 0))
    o_c, sel, o_w = pl.pallas_call(
        functools.partial(_nsa_cmp_win_kernel, t=t),
        grid=(batch, seq // t),
        in_specs=[row_spec(t), cmp_spec, cmp_spec, kv2_spec(5), kv2_spec(7), gl2_spec(t),
                  pl.BlockSpec(ovl_t.shape, lambda b, i: (0, 0))],
        out_specs=[row_spec(t), sel_spec, row_spec(t)],
        out_shape=[o_shape, jax.ShapeDtypeStruct((batch, 2, 2, seq, LANES), BF), o_shape],
        scratch_shapes=_stats_scratch(2, 8 * t),
        compiler_params=sem2,
        name="nsa_compressed_window",
    )(pa3, kcmp, vcmp, pa3, pa3, pf3, ovl_t)

    ts = t
    o_s = pl.pallas_call(
        functools.partial(_nsa_sel_kernel, t=ts, tk=tk),
        grid=(batch, seq // ts),
        in_specs=[row_spec(ts), kv2_spec(4), kv2_spec(6), gl2_spec(ts),
                  pl.BlockSpec((1, 2, 2, ts, LANES), lambda b, i: (b, 0, 0, i, 0)),
                  pl.BlockSpec(ind.shape, lambda b, i: (0, 0))],
        out_specs=row_spec(ts),
        out_shape=o_shape,
        scratch_shapes=_stats_scratch(4, 4 * ts),
        compiler_params=sem2,
        name="nsa_selected",
    )(pa3, pa3, pa3, pf3, sel, ind)

    n = batch * seq
    return [o_c.reshape(n, D_MODEL), o_s.reshape(n, D_MODEL), o_w.reshape(n, D_MODEL)]


def _rope_tables(seq):
    half = ROT_DIM // 2
    inv_freq = ROPE_THETA ** (-jnp.arange(half, dtype=F32) / half)
    ang = jnp.arange(seq).astype(F32)[:, None] * inv_freq[None, :]
    cos, sin = jnp.cos(ang), jnp.sin(ang)
    d = np.arange(LANES) % HEAD_DIM
    idx = d % half
    cos_t = jnp.where(d[None, :] < ROT_DIM, cos[:, idx], 1.0)
    sa_t = jnp.where(d[None, :] < half, -sin[:, idx], 0.0)
    sb_t = jnp.where((d[None, :] >= half) & (d[None, :] < ROT_DIM), sin[:, idx], 0.0)
    return cos_t.astype(F32), sa_t.astype(F32), sb_t.astype(F32)


def _block_indicator(seq, shift):
    blk = np.arange(seq)[:, None] >> shift
    return jnp.asarray((blk == np.arange(LANES)[None, :]).astype(np.float32), BF)


def _nsa_q_perm():
    cols = []
    for gp in range(2):
        for r in range(4):
            for half in range(2):
                head = 8 * gp + 4 * half + r
                cols.extend(range(head * HEAD_DIM, (head + 1) * HEAD_DIM))
    return np.asarray(cols, np.int32)


def _nsa_gate_cols():
    src = -np.ones(2 * LANES, np.int32)
    for gp in range(2):
        for half in range(2):
            for r in range(4):
                for br in range(3):
                    src[gp * LANES + half * 12 + r * 3 + br] = (4 * (2 * gp + half) + r) * 3 + br
    return src


def kernel(x, norm_g, ffn_w_in, ffn_w_out, diff_w_in, diff_w_out, diff_lambda, diff_subln, moba_w_in, moba_w_out,
           nsa_w_in, nsa_w_out, nsa_cmp_pe, nsa_cmp_w1, nsa_cmp_w2):
    batch, seq, d = x.shape
    n = batch * seq
    tm = 512
    assert d == D_MODEL and seq % 512 == 0 and seq // SLC_BLOCK <= LANES and seq // MOBA_BLOCK >= 8
    h = x.reshape(n, d)
    tables = _rope_tables(seq)
    wi_all = ffn_w_in.astype(BF)
    wo_all = ffn_w_out.astype(BF)

    for i in range(DEPTH):
        g = norm_g[i]
        h = _ffn(h, g[0:2], wi_all, wo_all, i, 0, tm)

        kind, j = i % 3, i // 3
        if kind == 0:
            lambda_init = 0.8 - 0.6 * math.exp(-0.3 * i)
            plan = [(c * 256, 256, c < 8, 0, c * 256, None, Q_SCALE if c < 4 else None) for c in range(12)]
            (qkv,) = _proj(h, g[2:3], diff_w_in[j].astype(BF), tables, plan, [(3 * D_MODEL, BF, 1)], tm, seq)
            attn = _diff_attention(qkv, diff_lambda[j], diff_subln[j].reshape(1, LANES), batch, seq, 512,
                                   lambda_init)
            parts, w_out = [attn], diff_w_out[j]
        elif kind == 1:
            plan = [(c * 256, 256, c < 8, 0, c * 256, (c - 4) * 256 if 4 <= c < 8 else None,
                     Q_SCALE if c < 4 else None) for c in range(12)]
            qkv, kmean = _proj(h, g[2:3], moba_w_in[j].astype(BF), tables, plan, [(3 * D_MODEL, BF, 1)], tm, seq,
                               kmean_width=D_MODEL)
            attn = _moba_attention(qkv, kmean, batch, seq, 512)
            parts, w_out = [attn], moba_w_out[j]
        else:
            w = nsa_w_in[j]
            perm = _nsa_q_perm()
            kvw = NSA_GROUPS * HEAD_DIM
            base = NSA_HEADS * HEAD_DIM
            seg = {name: w[:, base + k * kvw: base + (k + 1) * kvw]
                   for k, name in enumerate(["kc", "vc", "ks", "vs", "kw", "vw"])}
            gsrc = _nsa_gate_cols()
            glog = w[:, base + 6 * kvw:]
            gate_w = jnp.where(gsrc[None, :] >= 0, glog[:, np.maximum(gsrc, 0)], 0.0)
            w_all = jnp.concatenate([w[:, perm], seg["ks"], seg["kw"], seg["kc"], seg["vs"], seg["vw"], seg["vc"],
                                     gate_w], axis=1).astype(BF)
            plan = [(c * 256, 256, True, 0, c * 256, None, Q_SCALE if c < 4 else None)
                    for c in range(6)]
            plan.append((1536, 256, True, 1, 0, None, None))
            plan.append((1792, 256, False, 0, 1536, None, None))
            plan.append((2048, 256, False, 0, 1792, None, None))
            plan.append((2304, 256, False, 2, 0, None, None))
            plan.append((2560, 256, False, 3, 0, None, None))
            pa, kc, vc, pf = _proj(h, g[2:3], w_all, tables, plan,
                                   [(2048, BF, 1), (kvw, F32, CMP_STRIDE), (kvw, F32, CMP_STRIDE),
                                    (2 * LANES, F32, 1)], tm, seq)

            ng = seq // CMP_STRIDE
            cmp_tm = []
            for a, src in enumerate((kc, vc)):
                pe, w1, w2 = _compress_weights(nsa_cmp_pe[j][a], nsa_cmp_w1[j][a], nsa_cmp_w2[j][a])
                cmp_tm.append(_compress(src.reshape(batch, ng, CMP_STRIDE * kvw), pe, w1, w2))

            nsel = seq // SLC_BLOCK
            cs = np.arange(ng)[:, None] * CMP_STRIDE
            bs = np.arange(nsel)[None, :] * SLC_BLOCK
            ovl = ((cs < bs + SLC_BLOCK) & (cs + CMP_LEN > bs)).astype(np.float32)
            ovl[ng - 1, :] = 0.0
            ovl_t = jnp.asarray(ovl.T, BF)
            parts = _nsa_attention(pa, pf, cmp_tm[0], cmp_tm[1], ovl_t, batch, seq, 256, 512)
            w_out = nsa_w_out[j][perm, :]

        h = _ffn(h, g[4:6], wi_all, wo_all, i, 1, tm, parts=parts, wp=w_out.astype(BF), gp=g[3:4])
    return h.reshape(batch, seq, d)
```

```python
import functools
import math

import jax
import jax.numpy as jnp
import numpy as np
from jax import lax
from jax.experimental import pallas as pl
from jax.experimental.pallas import tpu as pltpu

D_MODEL = 1024
DEPTH = 4
HEAD_DIM = 64
ROT_DIM = HEAD_DIM // 4
ROPE_THETA = 500000.0
NORM_EPS = 1e-6
NEG_INF = -1e30
REMOVED = -3e38

DIFF_HEADS = 8
DIFF_SUBLN_EPS = 1e-5
MOBA_BLOCK = 256
MOBA_SHIFT = 8
MOBA_TOPK = 3
NSA_HEADS = 16
NSA_GROUPS = 4
CMP_LEN = 32
CMP_STRIDE = 16
CMP_HIDDEN = 256
SLC_BLOCK = 64
SLC_SHIFT = 6
SLC_TOPK = 16
WINDOW = 512
FORCED_SCORE = 1e9
D_FF = 2816

LANES = 128
FF_CHUNK = 256
CHAINS = 4
Q_SCALE = HEAD_DIM ** -0.5 * math.log2(math.e)

BF = jnp.bfloat16
F32 = jnp.float32
VMEM_LIMIT = 56 * 1024 * 1024

_NT = (((1,), (1,)), ((), ()))


def _cparams(sem):
    return pltpu.CompilerParams(dimension_semantics=sem, vmem_limit_bytes=VMEM_LIMIT)


def _rms(x, g, eps):
    return x * lax.rsqrt(jnp.mean(x * x, axis=-1, keepdims=True) + eps) * g


def _resident(shape):
    nd = len(shape)
    return pl.BlockSpec(shape, lambda *_: (0,) * nd, pipeline_mode=pl.Buffered(1))


def _ffn_kernel(*refs, n_parts):
    h_ref = refs[0]
    parts = refs[1:1 + n_parts]
    if n_parts:
        wp_ref, gp_ref = refs[1 + n_parts:3 + n_parts]
        refs = refs[3 + n_parts:]
    else:
        refs = refs[1:]
    g_ref, wi_ref, wo_ref, o_ref, acc_ref = refs
    h = h_ref[...]
    if n_parts:
        a = parts[0][...]
        if n_parts > 1:
            a = a.astype(F32)
            for p in parts[1:]:
                a = a + p[...].astype(F32)
            a = a.astype(BF)
        h = h + _rms(jnp.dot(a, wp_ref[...], preferred_element_type=F32), gp_ref[...], NORM_EPS)
    xn = _rms(h, g_ref[0:1, :], NORM_EPS).astype(BF)
    for c in range(D_FF // FF_CHUNK):
        lo = c * FF_CHUNK
        gate = jnp.dot(xn, wi_ref[:, lo:lo + FF_CHUNK], preferred_element_type=F32)
        up = jnp.dot(xn, wi_ref[:, D_FF + lo:D_FF + lo + FF_CHUNK], preferred_element_type=F32)
        act = ((gate * (1.0 / (1.0 + jnp.exp(-gate)))) * up).astype(BF)
        part = jnp.dot(act, wo_ref[lo:lo + FF_CHUNK, :], preferred_element_type=F32)
        if c == 0:
            acc_ref[...] = part
        else:
            acc_ref[...] += part
    o_ref[...] = h + 0.5 * _rms(acc_ref[...], g_ref[1:2, :], NORM_EPS)


def _ffn(h, g2, wi_all, wo_all, layer, which, tm, parts=(), wp=None, gp=None):
    n, d = h.shape
    row = pl.BlockSpec((tm, d), lambda i: (i, 0))
    proj_args = [wp, gp] if parts else []

    def picked(w):
        return pl.BlockSpec((None, None) + w.shape[2:], lambda i: (layer, which, 0, 0), pipeline_mode=pl.Buffered(1))

    return pl.pallas_call(
        functools.partial(_ffn_kernel, n_parts=len(parts)),
        grid=(n // tm,),
        in_specs=([row] * (1 + len(parts)) + [_resident(a.shape) for a in proj_args]
                  + [_resident(g2.shape), picked(wi_all), picked(wo_all)]),
        out_specs=row,
        out_shape=jax.ShapeDtypeStruct((n, d), F32),
        scratch_shapes=[pltpu.VMEM((tm, d), F32)],
        compiler_params=_cparams(("parallel",)),
        name="ffn_halfstep",
    )(h, *parts, *proj_args, g2, wi_all, wo_all)


def _proj_kernel(h_ref, g_ref, w_ref, cos_ref, sa_ref, sb_ref, *refs, plan, tm, n_out):
    out_refs, slab_ref = refs[:-1], refs[-1]
    xn = _rms(h_ref[...], g_ref[...], NORM_EPS).astype(BF)
    cos = cos_ref[...]
    sa = sa_ref[...]
    sb = sb_ref[...]
    for col, width, rope, dest, off, km_off, scale in plan:
        y = jnp.dot(xn, w_ref[:, col:col + width], preferred_element_type=F32)
        for k in range(width // LANES):
            yk = y[:, k * LANES:(k + 1) * LANES]
            if rope:
                yk = yk * cos + pltpu.roll(yk, LANES - ROT_DIM // 2, 1) * sa + pltpu.roll(yk, ROT_DIM // 2, 1) * sb
            if scale is not None:
                yk = yk * scale
            o_ref = out_refs[dest]
            if o_ref.shape[0] == tm:
                o_ref[:, off + k * LANES:off + (k + 1) * LANES] = yk.astype(o_ref.dtype)
            else:
                grp = tm // o_ref.shape[0]
                slab_ref[...] = yk
                for tok in range(grp):
                    lo = tok * width + off + k * LANES
                    o_ref[:, lo:lo + LANES] = slab_ref[pl.ds(tok, tm // grp, stride=grp), :].astype(o_ref.dtype)
            if km_off is not None:
                km_ref = out_refs[n_out]
                for r in range(tm // MOBA_BLOCK):
                    blk = yk[r * MOBA_BLOCK:(r + 1) * MOBA_BLOCK, :]
                    km_ref[0, r:r + 1, km_off + k * LANES:km_off + (k + 1) * LANES] = jnp.mean(
                        blk, axis=0, keepdims=True)


def _proj(h, g, w, tables, plan, outs, tm, seq, kmean_width=None):
    n, d = h.shape
    per_seq = seq // tm
    out_shape = [jax.ShapeDtypeStruct((n // gr, wd * gr), dt) for wd, dt, gr in outs]
    out_specs = [pl.BlockSpec((tm // gr, wd * gr), lambda i: (i, 0)) for wd, _, gr in outs]
    if kmean_width is not None:
        nb = tm // MOBA_BLOCK
        out_shape.append(jax.ShapeDtypeStruct((n // tm, nb, kmean_width), F32))
        out_specs.append(pl.BlockSpec((1, nb, kmean_width), lambda i: (i, 0, 0)))
    tab_spec = pl.BlockSpec((tm, LANES), lambda i: (i % per_seq, 0))
    return pl.pallas_call(
        functools.partial(_proj_kernel, plan=tuple(plan), tm=tm, n_out=len(outs)),
        grid=(n // tm,),
        in_specs=[
            pl.BlockSpec((tm, d), lambda i: (i, 0)),
            _resident(g.shape),
            _resident(w.shape),
            tab_spec, tab_spec, tab_spec,
        ],
        out_specs=out_specs,
        out_shape=out_shape,
        scratch_shapes=[pltpu.VMEM((tm, LANES), F32)],
        compiler_params=_cparams(("parallel",)),
        name="norm_proj",
    )(h, g, w, *tables)


def _softmax_step(s, v, m_ref, l_ref, acc_ref):
    m_prev = m_ref[...]
    m_new = jnp.maximum(m_prev, jnp.max(s, axis=1, keepdims=True))
    alpha = jnp.exp2(m_prev - m_new)
    ps = [jnp.exp2(s[:, c * LANES:(c + 1) * LANES] - m_new) for c in range(s.shape[1] // LANES)]
    lsum = ps[0]
    for p in ps[1:]:
        lsum = lsum + p
    l_ref[...] = alpha * l_ref[...] + lsum
    p = jnp.concatenate([x.astype(BF) for x in ps], axis=1)
    acc_ref[...] = alpha * acc_ref[...] + jnp.dot(p, v, preferred_element_type=F32)
    m_ref[...] = m_new


def _init_stats(m_ref, l_ref, acc_ref):
    m_ref[...] = jnp.full(m_ref.shape, NEG_INF, F32)
    l_ref[...] = jnp.zeros(l_ref.shape, F32)
    acc_ref[...] = jnp.zeros(acc_ref.shape, F32)


def _normalised(l_ref, acc_ref):
    return acc_ref[...] * (1.0 / jnp.sum(l_ref[...], axis=1, keepdims=True))


def _stats_scratch(chains, rows):
    return [pltpu.VMEM((chains, rows, LANES), F32) for _ in range(3)]


def _split_halves(q):
    lane = lax.broadcasted_iota(jnp.int32, q.shape, 1)
    zero = jnp.zeros_like(q)
    return jnp.concatenate([jnp.where(lane < HEAD_DIM, q, zero), jnp.where(lane >= HEAD_DIM, q, zero)], axis=0)


def _lane_block(c):
    return slice(c * LANES, (c + 1) * LANES)


def _diff_kernel(q_ref, k_ref, v_ref, lam_ref, sg_ref, o_ref, m_ref, l_ref, acc_ref, *, t, lambda_init):
    qi = pl.program_id(2)
    qs = [_split_halves(q_ref[0, :, _lane_block(c)]) for c in range(CHAINS)]
    _init_stats(m_ref, l_ref, acc_ref)

    def tile(j, causal, width=t):
        start = pl.multiple_of(j * t, t)
        ss = [lax.dot_general(qs[c], k_ref[0, pl.ds(start, width), _lane_block(c)], _NT,
                              preferred_element_type=F32) for c in range(CHAINS)]
        if causal:
            row = lax.broadcasted_iota(jnp.int32, (2 * t, t), 0) & (t - 1)
            col = lax.broadcasted_iota(jnp.int32, (2 * t, t), 1)
            ss = [jnp.where(col <= row, s, NEG_INF) for s in ss]
        for c in range(CHAINS):
            v = v_ref[0, pl.ds(start, width), _lane_block(c)]
            _softmax_step(ss[c], v, m_ref.at[c], l_ref.at[c], acc_ref.at[c])

    tile(qi, True)

    def body(jj, carry):
        tile(2 * jj, False, 2 * t)
        return carry

    lax.fori_loop(0, qi // 2, body, 0)

    @pl.when((qi & 1) == 1)
    def _():
        tile(qi - 1, False)

    lam = lam_ref[...]
    lam_full = (jnp.exp(jnp.sum(lam[0:1] * lam[1:2], axis=1, keepdims=True))
                - jnp.exp(jnp.sum(lam[2:3] * lam[3:4], axis=1, keepdims=True)) + lambda_init)
    for c in range(CHAINS):
        o = _normalised(l_ref.at[c], acc_ref.at[c])
        od = o[:t] - lam_full * o[t:]
        od = _rms(od, sg_ref[...], DIFF_SUBLN_EPS) * (1.0 - lambda_init)
        o_ref[0, :, _lane_block(c)] = od.astype(BF)


def _diff_attention(qkv, lam, subln, batch, seq, t, lambda_init):
    nh = DIFF_HEADS
    ns = nh // CHAINS
    w = CHAINS * LANES
    qkv3 = qkv.reshape(batch, seq, 3 * nh * LANES)
    out = pl.pallas_call(
        functools.partial(_diff_kernel, t=t, lambda_init=lambda_init),
        grid=(batch, ns, seq // t),
        in_specs=[
            pl.BlockSpec((1, t, w), lambda b, h, i: (b, i, h)),
            pl.BlockSpec((1, seq, w), lambda b, h, i: (b, 0, ns + h)),
            pl.BlockSpec((1, seq, w), lambda b, h, i: (b, 0, 2 * ns + h)),
            pl.BlockSpec(lam.shape, lambda b, h, i: (0, 0)),
            pl.BlockSpec(subln.shape, lambda b, h, i: (0, 0)),
        ],
        out_specs=pl.BlockSpec((1, t, w), lambda b, h, i: (b, i, h)),
        out_shape=jax.ShapeDtypeStruct((batch, seq, nh * LANES), BF),
        scratch_shapes=_stats_scratch(CHAINS, 2 * t),
        compiler_params=_cparams(("parallel", "parallel", "arbitrary")),
        name="diff_attention",
    )(qkv3, qkv3, qkv3, lam, subln)
    return out.reshape(batch * seq, nh * LANES)


def _moba_select(q2, km, qi, t, nb):
    km_hi = km.astype(BF)
    km_lo = (km - km_hi.astype(F32)).astype(BF)
    gate = (lax.dot_general(km_hi, q2, _NT, preferred_element_type=F32)
            + lax.dot_general(km_lo, q2, _NT, preferred_element_type=F32))
    blk = lax.broadcasted_iota(jnp.int32, gate.shape, 0)
    blk_f = blk.astype(F32)
    own = (qi * t + (lax.broadcasted_iota(jnp.int32, gate.shape, 1) & (t - 1))) >> MOBA_SHIFT
    gate = jnp.where(blk < own, gate, NEG_INF)
    sel = jnp.zeros(gate.shape, F32)
    for _ in range(min(MOBA_TOPK, nb)):
        mx = jnp.max(gate, axis=0, keepdims=True)
        idx = jnp.min(jnp.where(gate == mx, blk_f, float(nb)), axis=0, keepdims=True)
        hit = blk_f == idx
        sel = jnp.where(hit & (mx > 0.5 * NEG_INF), 1.0, sel)
        gate = jnp.where(hit, REMOVED, gate)
    bias_t = jnp.where((sel > 0.5) | (blk == own), 0.0, NEG_INF)
    bias_t = jnp.concatenate([bias_t, jnp.zeros((LANES - nb, 2 * t), F32)], axis=0)
    return jnp.concatenate([bias_t[:, _lane_block(n)].T for n in range(2 * t // LANES)], axis=0).astype(BF)


def _moba_kernel(q_ref, k_ref, v_ref, km_ref, ind_ref, o_ref, m_ref, l_ref, acc_ref, *, t, nb):
    qi = pl.program_id(2)
    _init_stats(m_ref, l_ref, acc_ref)
    q_aug = []
    for c in range(CHAINS):
        q2 = _split_halves(q_ref[0, :, _lane_block(c)])
        bias = _moba_select(q2, km_ref[0, :, _lane_block(c)], qi, t, nb)
        q_aug.append(jnp.concatenate([q2, bias], axis=1))

    def tile(j, diagonal, width=t):
        start = pl.multiple_of(j * t, t)
        ind = ind_ref[pl.ds(start, width), :]
        if diagonal:
            row = lax.broadcasted_iota(jnp.int32, (2 * t, t), 0) & (t - 1)
            col = lax.broadcasted_iota(jnp.int32, (2 * t, t), 1)
            future = ((row >> MOBA_SHIFT) == (col >> MOBA_SHIFT)) & (col > row)
        for c in range(CHAINS):
            k_aug = jnp.concatenate([k_ref[0, pl.ds(start, width), _lane_block(c)], ind], axis=1)
            v = v_ref[0, pl.ds(start, width), _lane_block(c)]
            s = lax.dot_general(q_aug[c], k_aug, _NT, preferred_element_type=F32)
            if diagonal:
                s = jnp.where(future, NEG_INF, s)
            _softmax_step(s, v, m_ref.at[c], l_ref.at[c], acc_ref.at[c])

    tile(qi, True)

    def body(jj, carry):
        tile(2 * jj, False, 2 * t)
        return carry

    lax.fori_loop(0, qi // 2, body, 0)

    @pl.when((qi & 1) == 1)
    def _():
        tile(qi - 1, False)

    lane = lax.broadcasted_iota(jnp.int32, (t, LANES), 1)
    for c in range(CHAINS):
        o = _normalised(l_ref.at[c], acc_ref.at[c])
        o_ref[0, :, _lane_block(c)] = jnp.where(lane < HEAD_DIM, o[:t], o[t:]).astype(BF)


def _moba_attention(qkv, kmean, batch, seq, t):
    nb = seq // MOBA_BLOCK
    ns = D_MODEL // LANES // CHAINS
    w = CHAINS * LANES
    qkv3 = qkv.reshape(batch, seq, 3 * D_MODEL)
    km3 = kmean.reshape(batch, nb, D_MODEL)
    ind = _block_indicator(seq, MOBA_SHIFT)
    out = pl.pallas_call(
        functools.partial(_moba_kernel, t=t, nb=nb),
        grid=(batch, ns, seq // t),
        in_specs=[
            pl.BlockSpec((1, t, w), lambda b, p, i: (b, i, p)),
            pl.BlockSpec((1, seq, w), lambda b, p, i: (b, 0, ns + p)),
            pl.BlockSpec((1, seq, w), lambda b, p, i: (b, 0, 2 * ns + p)),
            pl.BlockSpec((1, nb, w), lambda b, p, i: (b, 0, p)),
            pl.BlockSpec(ind.shape, lambda b, p, i: (0, 0)),
        ],
        out_specs=pl.BlockSpec((1, t, w), lambda b, p, i: (b, i, p)),
        out_shape=jax.ShapeDtypeStruct((batch, seq, D_MODEL), BF),
        scratch_shapes=_stats_scratch(CHAINS, 2 * t),
        compiler_params=_cparams(("parallel", "parallel", "arbitrary")),
        name="moba_attention",
    )(qkv3, qkv3, qkv3, km3, ind)
    return out.reshape(batch * seq, D_MODEL)


def _compress_kernel(r_ref, pe_ref, w1_ref, w2_ref, o_ref):
    r = r_ref[0]
    ng = r.shape[0]
    a0 = (r + pe_ref[0]).astype(BF)
    a1 = (r + pe_ref[1]).astype(BF)
    y0 = jnp.dot(a0, w1_ref[0], preferred_element_type=F32)
    y1 = jnp.dot(a1, w1_ref[1], preferred_element_type=F32)
    pre = y0 + pltpu.roll(y1, ng - 1, 0)
    hid = pre * (1.0 / (1.0 + jnp.exp(-pre)))
    o_ref[0] = jnp.dot(hid.astype(BF), w2_ref[...], preferred_element_type=F32).astype(o_ref.dtype)


def _compress(r, pe, w1, w2):
    batch, ng, wd = r.shape
    wout = w2.shape[1]
    return pl.pallas_call(
        _compress_kernel,
        grid=(batch,),
        in_specs=[
            pl.BlockSpec((1, ng, wd), lambda b: (b, 0, 0)),
            _resident(pe.shape),
            _resident(w1.shape),
            _resident(w2.shape),
        ],
        out_specs=pl.BlockSpec((1, ng, wout), lambda b: (b, 0, 0)),
        out_shape=jax.ShapeDtypeStruct((batch, ng, wout), BF),
        compiler_params=_cparams(("parallel",)),
        name="nsa_compress",
    )(r, pe, w1, w2)


def _compress_weights(pe, w1, w2):
    g = NSA_GROUPS

    def block_diag(w):
        c = w.shape[-1]
        lead = [(0, 0)] * (w.ndim - 1)
        return jnp.concatenate([jnp.pad(w, lead + [(k * c, (g - 1 - k) * c)]) for k in range(g)], axis=-2)

    w1r = w1.astype(BF).reshape(2, CMP_STRIDE, HEAD_DIM, CMP_HIDDEN)
    w1b = block_diag(w1r).reshape(2, CMP_STRIDE * g * HEAD_DIM, g * CMP_HIDDEN)
    w2b = block_diag(w2.astype(BF))
    peb = jnp.broadcast_to(pe.reshape(2, CMP_STRIDE, 1, HEAD_DIM), (2, CMP_STRIDE, g, HEAD_DIM))
    return peb.reshape(2, 1, CMP_STRIDE * g * HEAD_DIM), w1b, w2b


def _nsa_stack_q(q, t):
    lane = lax.broadcasted_iota(jnp.int32, (t, LANES), 1)
    zero = jnp.zeros((t, LANES), q.dtype)
    parts = []
    for half in range(2):
        keep = (lane < HEAD_DIM) if half == 0 else (lane >= HEAD_DIM)
        for r in range(4):
            parts.append(jnp.where(keep, q[:, r * LANES:(r + 1) * LANES], zero))
    return jnp.concatenate(parts, axis=0)


def _nsa_write(o, gl, branch, o_ref, t, gp=0):
    lane = lax.broadcasted_iota(jnp.int32, (t, LANES), 1)
    for r in range(4):
        c0 = r * 3 + branch
        c1 = 12 + r * 3 + branch
        g0 = 1.0 / (1.0 + jnp.exp(-gl[:, c0:c0 + 1]))
        g1 = 1.0 / (1.0 + jnp.exp(-gl[:, c1:c1 + 1]))
        blk = jnp.where(lane < HEAD_DIM, g0 * o[r * t:(r + 1) * t], g1 * o[(4 + r) * t:(5 + r) * t])
        o_ref[0, :, _lane_block(4 * gp + r)] = blk.astype(BF)


def _nsa_cmp_body(qi, q, kc, vc, gl, ovl_ref, o_ref, sel_ref, gp, t):
    qs = _nsa_stack_q(q, t)
    s = lax.dot_general(qs, kc, _NT, preferred_element_type=F32)
    n_idx = lax.broadcasted_iota(jnp.int32, s.shape, 1)
    qidx = qi * t + (lax.broadcasted_iota(jnp.int32, s.shape, 0) & (t - 1))
    mask = (n_idx * CMP_STRIDE + (CMP_LEN - 1)) <= qidx
    sm = jnp.where(mask, s, NEG_INF)
    p = jnp.where(mask, jnp.exp2(sm - jnp.max(sm, axis=1, keepdims=True)), 0.0)
    l = jnp.sum(p, axis=1, keepdims=True)
    pn = p * (1.0 / jnp.where(l > 0.0, l, 1.0))
    o = jnp.dot(pn.astype(BF), vc, preferred_element_type=F32)
    _nsa_write(o, gl, 0, o_ref, t, gp)

    nsel = ovl_ref.shape[0]
    jb = lax.broadcasted_iota(jnp.int32, (nsel, t), 0)
    qblk =(qi * t + lax.broadcasted_iota(jnp.int32, (nsel, t), 1)) >> SLC_SHIFT
    forced = (jb == 0) | (jb == qblk) | (jb == qblk - 1)
    ovl = ovl_ref[...]
    for half in range(2):
        ps = pn[(half * 4) * t:(half * 4 + 1) * t]
        for r in range(1, 4):
            ps = ps + pn[(half * 4 + r) * t:(half * 4 + r + 1) * t]
        ps_hi = ps.astype(BF)
        ps_lo = (ps - ps_hi.astype(F32)).astype(BF)
        imp = (lax.dot_general(ovl, ps_hi, _NT, preferred_element_type=F32)
               + lax.dot_general(ovl, ps_lo, _NT, preferred_element_type=F32))
        val = jnp.where(jb <= qblk, jnp.where(forced, FORCED_SCORE, imp), NEG_INF)
        slabs = [val[g * 8:(g + 1) * 8] for g in range(nsel // 8)]
        row8 = lax.broadcasted_iota(jnp.int32, (8, t), 0)
        cnt = [jnp.zeros((8, t), F32) for _ in slabs]
        for i in range(nsel):
            vi = val[i:i + 1, :]
            for g, sl in enumerate(slabs):
                if g > i // 8:
                    ahead = vi >= sl
                elif g < i // 8:
                    ahead = vi > sl
                else:
                    ahead = (vi > sl) | ((vi == sl) & (row8 > (i % 8)))
                cnt[g] = jnp.where(ahead, cnt[g] + 1.0, cnt[g])
        rank = jnp.concatenate(cnt, axis=0)
        chosen = jnp.where((rank < float(min(SLC_TOPK, nsel))) & (jb <= qblk), 1.0, 0.0)
        pad = jnp.zeros((LANES - nsel, t), F32)
        chosen = jnp.concatenate([chosen, pad], axis=0)
        sel_ref[0, gp, half] = jnp.concatenate([chosen[:, _lane_block(n)].T for n in range(t // LANES)],
                                               axis=0).astype(BF)


def _nsa_sel_kernel(q_ref, k_ref, v_ref, gl_ref, sel_ref, ind_ref, o_ref, m_ref, l_ref, acc_ref, *, t, tk):
    qi = pl.program_id(1)
    q_aug = []
    for gp in range(2):
        qs = _nsa_stack_q(q_ref[0, :, gp * 4 * LANES:(gp + 1) * 4 * LANES], t)
        for half in range(2):
            bias = jnp.where(sel_ref[0, gp, half].astype(F32) > 0.5, 0.0, NEG_INF).astype(BF)
            q_aug.append(jnp.concatenate([qs[half * 4 * t:(half + 1) * 4 * t],
                                          jnp.concatenate([bias] * 4, axis=0)], axis=1))
    _init_stats(m_ref, l_ref, acc_ref)
    jd = (qi * t) // tk

    def tile(j, causal, width=tk):
        start = pl.multiple_of(j * tk, tk)
        ind = ind_ref[pl.ds(start, width), :]
        if causal:
            qidx = qi * t + (lax.broadcasted_iota(jnp.int32, (4 * t, tk), 0) & (t - 1))
            kidx = j * tk + lax.broadcasted_iota(jnp.int32, (4 * t, tk), 1)
        ss = []
        for gp in range(2):
            k_aug = jnp.concatenate([k_ref[0, pl.ds(start, width), _lane_block(gp)], ind], axis=1)
            for half in range(2):
                s = lax.dot_general(q_aug[2 * gp + half], k_aug, _NT, preferred_element_type=F32)
                ss.append(jnp.where(kidx <= qidx, s, NEG_INF) if causal else s)
        for c in range(4):
            v = v_ref[0, pl.ds(start, width), _lane_block(c // 2)]
            _softmax_step(ss[c], v, m_ref.at[c], l_ref.at[c], acc_ref.at[c])

    tile(jd, True)

    def body(jj, carry):
        tile(2 * jj, False, 2 * tk)
        return carry

    lax.fori_loop(0, jd // 2, body, 0)

    @pl.when((jd & 1) == 1)
    def _():
        tile(jd - 1, False)

    for gp in range(2):
        o = jnp.concatenate([_normalised(l_ref.at[2 * gp + half], acc_ref.at[2 * gp + half]) for half in range(2)],
                            axis=0)
        _nsa_write(o, gl_ref[0, :, _lane_block(gp)], 1, o_ref, t, gp)


def _nsa_cmp_win_kernel(q_ref, kc_ref, vc_ref, k_ref, v_ref, gl_ref, ovl_ref, oc_ref, sel_ref, o_ref,
                        m_ref, l_ref, acc_ref, *, t):
    qi = pl.program_id(1)
    for gp in range(2):
        _nsa_cmp_body(qi, q_ref[0, :, gp * 4 * LANES:(gp + 1) * 4 * LANES], kc_ref[0, :, _lane_block(gp)],
                      vc_ref[0, :, _lane_block(gp)], gl_ref[0, :, _lane_block(gp)], ovl_ref, oc_ref, sel_ref, gp, t)
    qs = [_nsa_stack_q(q_ref[0, :, gp * 4 * LANES:(gp + 1) * 4 * LANES], t) for gp in range(2)]
    _init_stats(m_ref, l_ref, acc_ref)

    def tile(j, kind):
        start = pl.multiple_of(j * t, t)
        if kind != "full":
            row = lax.broadcasted_iota(jnp.int32, (8 * t, t), 0) & (t - 1)
            col = lax.broadcasted_iota(jnp.int32, (8 * t, t), 1)
            keep = (col <= row) if kind == "causal" else (col > row)
        for gp in range(2):
            k = k_ref[0, pl.ds(start, t), _lane_block(gp)]
            v = v_ref[0, pl.ds(start, t), _lane_block(gp)]
            s = lax.dot_general(qs[gp], k, _NT, preferred_element_type=F32)
            if kind != "full":
                s = jnp.where(keep, s, NEG_INF)
            _softmax_step(s, v, m_ref.at[gp], l_ref.at[gp], acc_ref.at[gp])

    tile(qi, "causal")

    @pl.when(qi >= 1)
    def _():
        tile(qi - 1, "full")

    @pl.when(qi >= 2)
    def _():
        tile(qi - 2, "tail")

    for gp in range(2):
        _nsa_write(_normalised(l_ref.at[gp], acc_ref.at[gp]), gl_ref[0, :, _lane_block(gp)], 2, o_ref, t, gp)


def _nsa_attention(pa, pf, kcmp, vcmp, ovl_t, batch, seq, t, tk):
    pa3 = pa.reshape(batch, seq, pa.shape[1])
    pf3 = pf.reshape(batch, seq, pf.shape[1])
    ncmp = kcmp.shape[1]
    o_shape = jax.ShapeDtypeStruct((batch, seq, D_MODEL), BF)
    ind = _block_indicator(seq, SLC_SHIFT)

    sem2 = _cparams(("parallel", "arbitrary"))
    kvw = 2 * LANES
    assert t == WINDOW // 2

    def row_spec(rows):
        return pl.BlockSpec((1, rows, D_MODEL), lambda b, i: (b, i, 0))

    def kv2_spec(col_block):
        return pl.BlockSpec((1, seq, kvw), lambda b, i: (b, 0, col_block))

    def gl2_spec(rows):
        return pl.BlockSpec((1, rows, kvw), lambda b, i: (b, i, 0))

    cmp_spec = pl.BlockSpec((1, ncmp, kvw), lambda b, i: (b, 0, 0))
    sel_spec = pl.BlockSpec((1, 2, 2, t, LANES), lambda b, i: (b, 0, 0, i, 0))
    o_c, sel, o_w = pl.pallas_call(
        functools.partial(_nsa_cmp_win_kernel, t=t),
        grid=(batch, seq // t),
        in_specs=[row_spec(t), cmp_spec, cmp_spec, kv2_spec(5), kv2_spec(7), gl2_spec(t),
                  pl.BlockSpec(ovl_t.shape, lambda b, i: (0, 0))],
        out_specs=[row_spec(t), sel_spec, row_spec(t)],
        out_shape=[o_shape, jax.ShapeDtypeStruct((batch, 2, 2, seq, LANES), BF), o_shape],
        scratch_shapes=_stats_scratch(2, 8 * t),
        compiler_params=sem2,
        name="nsa_compressed_window",
    )(pa3, kcmp, vcmp, pa3, pa3, pf3, ovl_t)

    ts = t
    o_s = pl.pallas_call(
        functools.partial(_nsa_sel_kernel, t=ts, tk=tk),
        grid=(batch, seq // ts),
        in_specs=[row_spec(ts), kv2_spec(4), kv2_spec(6), gl2_spec(ts),
                  pl.BlockSpec((1, 2, 2, ts, LANES), lambda b, i: (b, 0, 0, i, 0)),
                  pl.BlockSpec(ind.shape, lambda b, i: (0, 0))],
        out_specs=row_spec(ts),
        out_shape=o_shape,
        scratch_shapes=_stats_scratch(4, 4 * ts),
        compiler_params=sem2,
        name="nsa_selected",
    )(pa3, pa3, pa3, pf3, sel, ind)

    n = batch * seq
    return [o_c.reshape(n, D_MODEL), o_s.reshape(n, D_MODEL), o_w.reshape(n, D_MODEL)]


def _rope_tables(seq):
    half = ROT_DIM // 2
    inv_freq = ROPE_THETA ** (-jnp.arange(half, dtype=F32) / half)
    ang = jnp.arange(seq).astype(F32)[:, None] * inv_freq[None, :]
    cos, sin = jnp.cos(ang), jnp.sin(ang)
    d = np.arange(LANES) % HEAD_DIM
    idx = d % half
    cos_t = jnp.where(d[None, :] < ROT_DIM, cos[:, idx], 1.0)
    sa_t = jnp.where(d[None, :] < half, -sin[:, idx], 0.0)
    sb_t = jnp.where((d[None, :] >= half) & (d[None, :] < ROT_DIM), sin[:, idx], 0.0)
    return cos_t.astype(F32), sa_t.astype(F32), sb_t.astype(F32)


def _block_indicator(seq, shift):
    blk = np.arange(seq)[:, None] >> shift
    return jnp.asarray((blk == np.arange(LANES)[None, :]).astype(np.float32), BF)


def _nsa_q_perm():
    cols = []
    for gp in range(2):
        for r in range(4):
            for half in range(2):
                head = 8 * gp + 4 * half + r
                cols.extend(range(head * HEAD_DIM, (head + 1) * HEAD_DIM))
    return np.asarray(cols, np.int32)


def _nsa_gate_cols():
    src = -np.ones(2 * LANES, np.int32)
    for gp in range(2):
        for half in range(2):
            for r in range(4):
                for br in range(3):
                    src[gp * LANES + half * 12 + r * 3 + br] = (4 * (2 * gp + half) + r) * 3 + br
    return src


def kernel(x, norm_g, ffn_w_in, ffn_w_out, diff_w_in, diff_w_out, diff_lambda, diff_subln, moba_w_in, moba_w_out,
           nsa_w_in, nsa_w_out, nsa_cmp_pe, nsa_cmp_w1, nsa_cmp_w2):
    batch, seq, d = x.shape
    n = batch * seq
    tm = 512
    assert d == D_MODEL and seq % 512 == 0 and seq // SLC_BLOCK <= LANES and seq // MOBA_BLOCK >= 8
    h = x.reshape(n, d)
    tables = _rope_tables(seq)
    wi_all = ffn_w_in.astype(BF)
    wo_all = ffn_w_out.astype(BF)

    for i in range(DEPTH):
        g = norm_g[i]
        h = _ffn(h, g[0:2], wi_all, wo_all, i, 0, tm)

        kind, j = i % 3, i // 3
        if kind == 0:
            lambda_init = 0.8 - 0.6 * math.exp(-0.3 * i)
            plan = [(c * 256, 256, c < 8, 0, c * 256, None, Q_SCALE if c < 4 else None) for c in range(12)]
            (qkv,) = _proj(h, g[2:3], diff_w_in[j].astype(BF), tables, plan, [(3 * D_MODEL, BF, 1)], 2 * tm, seq)
            attn = _diff_attention(qkv, diff_lambda[j], diff_subln[j].reshape(1, LANES), batch, seq, 512,
                                   lambda_init)
            parts, w_out = [attn], diff_w_out[j]
        elif kind == 1:
            plan = [(c * 256, 256, c < 8, 0, c * 256, (c - 4) * 256 if 4 <= c < 8 else None,
                     Q_SCALE if c < 4 else None) for c in range(12)]
            qkv, kmean = _proj(h, g[2:3], moba_w_in[j].astype(BF), tables, plan, [(3 * D_MODEL, BF, 1)], 2 * tm, seq,
                               kmean_width=D_MODEL)
            attn = _moba_attention(qkv, kmean, batch, seq, 512)
            parts, w_out = [attn], moba_w_out[j]
        else:
            w = nsa_w_in[j]
            perm = _nsa_q_perm()
            kvw = NSA_GROUPS * HEAD_DIM
            base = NSA_HEADS * HEAD_DIM
            seg = {name: w[:, base + k * kvw: base + (k + 1) * kvw]
                   for k, name in enumerate(["kc", "vc", "ks", "vs", "kw", "vw"])}
            gsrc = _nsa_gate_cols()
            glog = w[:, base + 6 * kvw:]
            gate_w = jnp.where(gsrc[None, :] >= 0, glog[:, np.maximum(gsrc, 0)], 0.0)
            w_all = jnp.concatenate([w[:, perm], seg["ks"], seg["kw"], seg["kc"], seg["vs"], seg["vw"], seg["vc"],
                                     gate_w], axis=1).astype(BF)
            plan = [(c * 256, 256, True, 0, c * 256, None, Q_SCALE if c < 4 else None)
                    for c in range(6)]
            plan.append((1536, 256, True, 1, 0, None, None))
            plan.append((1792, 256, False, 0, 1536, None, None))
            plan.append((2048, 256, False, 0, 1792, None, None))
            plan.append((2304, 256, False, 2, 0, None, None))
            plan.append((2560, 256, False, 3, 0, None, None))
            pa, kc, vc, pf = _proj(h, g[2:3], w_all, tables, plan,
                                   [(2048, BF, 1), (kvw, F32, CMP_STRIDE), (kvw, F32, CMP_STRIDE),
                                    (2 * LANES, F32, 1)], 2 * tm, seq)

            ng = seq // CMP_STRIDE
            cmp_tm = []
            for a, src in enumerate((kc, vc)):
                pe, w1, w2 = _compress_weights(nsa_cmp_pe[j][a], nsa_cmp_w1[j][a], nsa_cmp_w2[j][a])
                cmp_tm.append(_compress(src.reshape(batch, ng, CMP_STRIDE * kvw), pe, w1, w2))

            nsel = seq // SLC_BLOCK
            cs = np.arange(ng)[:, None] * CMP_STRIDE
            bs = np.arange(nsel)[None, :] * SLC_BLOCK
            ovl = ((cs < bs + SLC_BLOCK) & (cs + CMP_LEN > bs)).astype(np.float32)
            ovl[ng - 1, :] = 0.0
            ovl_t = jnp.asarray(ovl.T, BF)
            parts = _nsa_attention(pa, pf, cmp_tm[0], cmp_tm[1], ovl_t, batch, seq, 256, 512)
            w_out = nsa_w_out[j][perm, :]

        h = _ffn(h, g[4:6], wi_all, wo_all, i, 1, tm, parts=parts, wp=w_out.astype(BF), gp=g[3:4])
    return h.reshape(batch, seq, d)
```
